```python
import math
import jax, jax.numpy as jnp
from jax import lax
import numpy as np

D_MODEL = 2048
BATCH = 2
SEQ = 4096
DEPTH = 4
DEC_BATCH = 128
DEC_SEQ = 8
PAST_LEN = 8192
PAGE_SIZE = 128

MIX_W = 512
N_BRANCH = 4
GLA_HEADS = 4
GLA_DK = 64
GLA_DV = 128
GLA_GATE_RANK = 16
GLA_TAU = 16.0
GLA_CHUNK = 32
S5_GROUP = 16
S5_GROUPS = MIX_W // S5_GROUP
S5_STATE = 64
S5_DT_MIN = 1e-3
S5_DT_MAX = 1e-1
POOL_WINDOWS = (2, 4, 8, 16)
POOL_GROUP = MIX_W // len(POOL_WINDOWS)
POOL_BUF = max(POOL_WINDOWS) - 1
MLA_HEADS = 4
MLA_NOPE = 128
MLA_ROPE = 64
MLA_V = 128
MLA_Q_RANK = 384
MLA_KV_RANK = 128
MLA_SCALE = (MLA_NOPE + MLA_ROPE) ** -0.5
ROPE_THETA = 10000.0
Q_BLOCK = 128
D_FF = 5632
CONV_W = 3
ALPHA = (2 * DEPTH) ** 0.25
BETA = (8 * DEPTH) ** -0.25
LN_EPS = 1e-5
RMS_EPS = 1e-6

IN_SIZES = (GLA_HEADS * GLA_DK, GLA_HEADS * GLA_DK, GLA_HEADS * GLA_DV, MIX_W, GLA_GATE_RANK,
            MIX_W, MIX_W, MLA_Q_RANK, MLA_KV_RANK, MLA_ROPE, N_BRANCH * D_MODEL)
IN_OFFSETS = tuple(int(v) for v in np.cumsum(IN_SIZES)[:-1])
IN_COLS = int(sum(IN_SIZES))

kernel_name = 'hybrid_gla_s5_pool_mla_step'


def layer_norm(x, g, b):
    xf = x.astype(jnp.float32)
    mu = jnp.mean(xf, -1, keepdims=True)
    var = jnp.mean(jnp.square(xf - mu), -1, keepdims=True)
    return ((xf - mu) * lax.rsqrt(var + LN_EPS) * g.astype(jnp.float32) + b.astype(jnp.float32)).astype(x.dtype)


def rms_norm(x, g):
    xf = x.astype(jnp.float32)
    y = xf * lax.rsqrt(jnp.mean(jnp.square(xf), -1, keepdims=True) + RMS_EPS)
    return (y * g.astype(jnp.float32)).astype(x.dtype)


def rope(x, pos):
    half = x.shape[-1] // 2
    inv = ROPE_THETA ** (-jnp.arange(half, dtype=jnp.float32) / half)
    ang = pos.astype(jnp.float32)[:, None] * inv
    shp = (pos.shape[0],) + (1,) * (x.ndim - 3) + (half,)
    cos = jnp.cos(ang).reshape(shp).astype(x.dtype)
    sin = jnp.sin(ang).reshape(shp).astype(x.dtype)
    x1, x2 = x[..., :half], x[..., half:]
    return jnp.concatenate([x1 * cos - x2 * sin, x1 * sin + x2 * cos], -1)


def gla_chunked(q, k, v, log_a, s0):
    B, T, H, K = q.shape
    V = v.shape[-1]
    C = min(GLA_CHUNK, T)
    pad = (-T) % C
    if pad:
        pw = ((0, 0), (0, pad), (0, 0), (0, 0))
        q, k, v, log_a = [jnp.pad(t, pw) for t in (q, k, v, log_a)]
    n = (T + pad) // C
    q, k, log_a = [t.astype(jnp.float32).reshape(B, n, C, H, K) for t in (q, k, log_a)]
    v = v.astype(jnp.float32).reshape(B, n, C, H, V)
    bc = jnp.cumsum(log_a, axis=2)
    causal = jnp.tril(jnp.ones((C, C), bool))
    decay = jnp.where(causal[:, :, None, None],
                      jnp.exp(jnp.minimum(bc[:, :, :, None] - bc[:, :, None, :], 0.0)), 0.0)
    scores = jnp.einsum('bnthk,bntshk,bnshk->bnhts', q, decay, k)
    o_intra = jnp.einsum('bnhts,bnshv->bnthv', scores, v)
    q_in = q * jnp.exp(bc)
    k_out = k * jnp.exp(bc[:, :, -1:] - bc)
    a_tot = jnp.exp(bc[:, :, -1])

    def step(S, inp):
        qi, ko, vi, at = inp
        o = jnp.einsum('bchk,bhkv->bchv', qi, S)
        S = at[..., None] * S + jnp.einsum('bchk,bchv->bhkv', ko, vi)
        return S, o

    xs = (jnp.moveaxis(q_in, 1, 0), jnp.moveaxis(k_out, 1, 0), jnp.moveaxis(v, 1, 0), jnp.moveaxis(a_tot, 1, 0))
    s_fin, o_inter = lax.scan(step, s0.astype(jnp.float32), xs)
    o = (o_intra + jnp.moveaxis(o_inter, 0, 1)).reshape(B, n * C, H, V)[:, :T]
    return o, s_fin


def s5_mixer(u, h0_re, h0_im, a_re, a_im, log_dt, b_re, b_im, c_re, c_im, d, w_glu, b_glu):
    B, T, _ = u.shape
    f32 = jnp.float32
    uf = u.astype(f32)
    ug = uf.reshape(B, T, S5_GROUPS, S5_GROUP)
    lr, li = a_re.astype(f32), a_im.astype(f32)
    dt = jnp.exp(log_dt.astype(f32))[:, None]
    mag = jnp.exp(lr * dt)
    ab_re, ab_im = mag * jnp.cos(li * dt), mag * jnp.sin(li * dt)
    den = lr * lr + li * li
    nr, ni = ab_re - 1.0, ab_im
    f_re = (nr * lr + ni * li) / den
    f_im = (ni * lr - nr * li) / den
    bu_re = jnp.einsum('gnj,btgj->btgn', b_re.astype(f32), ug)
    bu_im = jnp.einsum('gnj,btgj->btgn', b_im.astype(f32), ug)
    x_re = f_re * bu_re - f_im * bu_im
    x_im = f_re * bu_im + f_im * bu_re
    h0r, h0i = h0_re.astype(f32), h0_im.astype(f32)
    x_re = x_re.at[:, 0].add(ab_re * h0r - ab_im * h0i)
    x_im = x_im.at[:, 0].add(ab_re * h0i + ab_im * h0r)
    ar = jnp.broadcast_to(ab_re, x_re.shape)
    ai = jnp.broadcast_to(ab_im, x_re.shape)

    def combine(e1, e2):
        ar1, ai1, br1, bi1 = e1
        ar2, ai2, br2, bi2 = e2
        return (ar2 * ar1 - ai2 * ai1, ar2 * ai1 + ai2 * ar1,
                ar2 * br1 - ai2 * bi1 + br2, ar2 * bi1 + ai2 * br1 + bi2)

    _, _, h_re, h_im = lax.associative_scan(combine, (ar, ai, x_re, x_im), axis=1)
    y = jnp.einsum('gjn,btgn->btgj', c_re.astype(f32), h_re) - jnp.einsum('gjn,btgn->btgj', c_im.astype(f32), h_im)
    y = y.reshape(B, T, MIX_W) + d.astype(f32) * uf
    y = jax.nn.gelu(y)
    y = y * jax.nn.sigmoid(y @ w_glu.astype(f32) + b_glu.astype(f32))
    return y.astype(u.dtype), h_re[:, -1].astype(u.dtype), h_im[:, -1].astype(u.dtype)


def pool_mixer(u, prefix, pos0, w_pool, scale):
    B, T, _ = u.shape
    up = jnp.concatenate([prefix, u], 1)
    cs = jnp.cumsum(up.astype(jnp.float32), axis=1)
    cs = jnp.concatenate([jnp.zeros((B, 1, MIX_W), jnp.float32), cs], 1)
    P = POOL_BUF
    pos = pos0 + jnp.arange(T)
    outs = []
    for gi, w in enumerate(POOL_WINDOWS):
        lo, hi = gi * POOL_GROUP, (gi + 1) * POOL_GROUP
        s = cs[:, P + 1:P + 1 + T, lo:hi] - cs[:, P + 1 - w:P + 1 - w + T, lo:hi]
        cnt = jnp.minimum(pos + 1, w).astype(jnp.float32)
        outs.append(s / cnt[None, :, None] - u[..., lo:hi].astype(jnp.float32))
    dlt = jnp.stack(outs, axis=2)
    y = jnp.einsum('btgc,gcd->btgd', dlt, w_pool.astype(jnp.float32)).reshape(B, T, MIX_W)
    y = y * scale.astype(jnp.float32)
    return y.astype(u.dtype), up[:, -P:]


def mla_attend(q_lat, q_rope, c_kv, k_rope, q_pos, k_pos):
    s = jnp.einsum('bqhc,bkc->bhqk', q_lat, c_kv) + jnp.einsum('bqhr,bkr->bhqk', q_rope, k_rope)
    s = s.astype(jnp.float32) * MLA_SCALE
    s = jnp.where(k_pos[None, :] <= q_pos[:, None], s, -jnp.inf)
    p = jax.nn.softmax(s, axis=-1).astype(c_kv.dtype)
    return jnp.einsum('bhqk,bkc->bqhc', p, c_kv)


def mla_blocks(q_lat, q_rope, c_kv, k_rope, q_pos, k_pos):
    B, Tq, H, C = q_lat.shape
    if Tq <= Q_BLOCK or Tq % Q_BLOCK:
        return mla_attend(q_lat, q_rope, c_kv, k_rope, q_pos, k_pos)
    nb = Tq // Q_BLOCK
    ql = q_lat.reshape(B, nb, Q_BLOCK, H, C).swapaxes(0, 1)
    qr = q_rope.reshape(B, nb, Q_BLOCK, H, q_rope.shape[-1]).swapaxes(0, 1)
    qp = q_pos.reshape(nb, Q_BLOCK)
    out = lax.map(lambda a: mla_attend(a[0], a[1], c_kv, k_rope, a[2], k_pos), (ql, qr, qp))
    return out.swapaxes(0, 1).reshape(B, Tq, H, C)


def token_mixers(x, pos0, past_c, past_kr, gla_s0, s5_h0r, s5_h0i, pool_prefix, p):
    B, T, _ = x.shape
    pos = pos0 + jnp.arange(T)
    z = x @ p['w_in']
    gq, gk, gv, gog, ga, su, pu, mq, mkv, mkr, gt = jnp.split(z, IN_OFFSETS, axis=-1)
    q = gq.reshape(B, T, GLA_HEADS, GLA_DK) * (GLA_DK ** -0.5)
    k = gk.reshape(B, T, GLA_HEADS, GLA_DK)
    v = gv.reshape(B, T, GLA_HEADS, GLA_DV)
    log_a = jax.nn.log_sigmoid((ga @ p['w_gla_gate'] + p['b_gla_gate']).astype(jnp.float32)) / GLA_TAU
    o, s_gla = gla_chunked(q, k, v, log_a.reshape(B, T, GLA_HEADS, GLA_DK), gla_s0)
    o = rms_norm(o, p['gla_norm_g'].reshape(GLA_HEADS, GLA_DV)).reshape(B, T, MIX_W)
    ya = (o * jax.nn.silu(gog.astype(jnp.float32))).astype(x.dtype)
    yb, h_re, h_im = s5_mixer(su, s5_h0r, s5_h0i, p['s5_a_re'], p['s5_a_im'], p['s5_log_dt'], p['s5_b_re'],
                              p['s5_b_im'], p['s5_c_re'], p['s5_c_im'], p['s5_d'], p['w_s5_glu'], p['b_s5_glu'])
    yc, pool_new = pool_mixer(pu, pool_prefix, pos0, p['w_pool'], p['pool_scale'])
    qd = (rms_norm(mq, p['mla_q_norm']) @ p['w_q_up']).reshape(B, T, MLA_HEADS, MLA_NOPE + MLA_ROPE)
    q_nope, q_rope = qd[..., :MLA_NOPE], rope(qd[..., MLA_NOPE:], pos)
    c_new = rms_norm(mkv, p['mla_kv_norm'])
    kr_new = rope(mkr, pos)
    wkv = p['w_kv_up'].reshape(MLA_KV_RANK, MLA_HEADS, MLA_NOPE + MLA_V)
    w_uk, w_uv = wkv[..., :MLA_NOPE], wkv[..., MLA_NOPE:]
    q_lat = jnp.einsum('bthd,chd->bthc', q_nope, w_uk)
    if past_c is None:
        c_all, kr_all, k_pos = c_new, kr_new, pos
    else:
        c_all = jnp.concatenate([past_c, c_new], 1)
        kr_all = jnp.concatenate([past_kr, kr_new], 1)
        k_pos = jnp.arange(past_c.shape[1] + T)
    o_lat = mla_blocks(q_lat, q_rope, c_all, kr_all, pos, k_pos)
    yd = jnp.einsum('bthc,chd->bthd', o_lat, w_uv).reshape(B, T, MIX_W)
    br = jnp.stack([ya, yb, yc, yd], axis=2)
    gates = jax.nn.sigmoid(gt + p['b_gates']).reshape(B, T, N_BRANCH, D_MODEL)
    merged = jnp.sum(jnp.einsum('btgw,gwd->btgd', br, p['w_branch']) * gates, axis=2)
    out = merged @ p['w_out']
    return out, (c_new, kr_new, s_gla.astype(x.dtype), h_re, h_im, pool_new)


def conv_ffn(x, buf, p):
    T = x.shape[1]
    hg = x @ p['w_ffn_up']
    h, g = hg[..., :D_FF], hg[..., D_FF:]
    hp = jnp.concatenate([buf, h], 1)
    conv = p['ffn_conv_b'] + sum(hp[:, j:j + T] * p['ffn_conv_w'][j] for j in range(CONV_W))
    a = jax.nn.gelu(conv) * g
    return a @ p['w_ffn_down'], hp[:, -(CONV_W - 1):]


def trunk(x, pos0, cache_c, cache_kr, page_table, gla0, s5r0, s5i0, pool0, conv0, params):
    outs = ([], [], [], [], [], [], [])
    for l in range(DEPTH):
        p = {name: arr[l] for name, arr in params.items()}
        if cache_c is None:
            past_c, past_kr = None, None
        else:
            nb = page_table.shape[0]
            past_c = cache_c[l][page_table].reshape(nb, -1, MLA_KV_RANK)
            past_kr = cache_kr[l][page_table].reshape(nb, -1, MLA_ROPE)
        mix, (c_new, kr_new, s_gla, h_re, h_im, pbuf) = token_mixers(
            x, pos0, past_c, past_kr, gla0[l], s5r0[l], s5i0[l], pool0[l], p)
        x = layer_norm(ALPHA * x + mix, p['ln1_g'], p['ln1_b'])
        f, cbuf = conv_ffn(x, conv0[l], p)
        x = layer_norm(ALPHA * x + f, p['ln2_g'], p['ln2_b'])
        for lst, val in zip(outs, (c_new, kr_new, s_gla, h_re, h_im, pbuf, cbuf)):
            lst.append(val)
    c_s, kr_s, gla_s, re_s, im_s, pool_s, conv_s = [jnp.stack(o, 0) for o in outs]
    return x, c_s, kr_s, gla_s, re_s, im_s, pool_s, conv_s


def setup_inputs(seed: int = 0) -> dict:
    key = jax.random.key(seed)
    ks = iter(jax.random.split(key, 64))
    f32 = jnp.float32
    nrm = lambda shape, s=1.0: jax.random.normal(next(ks), shape, f32) * s
    n_pages = PAST_LEN // PAGE_SIZE
    n_phys = (DEC_BATCH * n_pages * 5) // 4
    page_table = jax.random.permutation(next(ks), n_phys)[:DEC_BATCH * n_pages].reshape(DEC_BATCH, n_pages).astype(jnp.int32)
    L = DEPTH
    n_idx = jnp.arange(S5_STATE, dtype=f32)
    return {
        'x_prompt': nrm((BATCH, SEQ, D_MODEL)),
        'x_sample': nrm((DEC_BATCH, DEC_SEQ, D_MODEL)),
        'cache_kv_latent': nrm((L, n_phys, PAGE_SIZE, MLA_KV_RANK)),
        'cache_k_rope': nrm((L, n_phys, PAGE_SIZE, MLA_ROPE)),
        'page_table': page_table,
        'state_gla': nrm((L, DEC_BATCH, GLA_HEADS, GLA_DK, GLA_DV), 0.3),
        'state_s5_re': nrm((L, DEC_BATCH, S5_GROUPS, S5_STATE), 0.1),
        'state_s5_im': nrm((L, DEC_BATCH, S5_GROUPS, S5_STATE), 0.1),
        'state_pool': nrm((L, DEC_BATCH, POOL_BUF, MIX_W)),
        'state_ffn_conv': nrm((L, DEC_BATCH, CONV_W - 1, D_FF)),
        'ln1_g': 1.0 + nrm((L, D_MODEL), 0.01),
        'ln1_b': nrm((L, D_MODEL), 0.01),
        'w_in': nrm((L, D_MODEL, IN_COLS), D_MODEL ** -0.5),
        'b_gates': nrm((L, N_BRANCH * D_MODEL), 0.02),
        'w_gla_gate': nrm((L, GLA_GATE_RANK, GLA_HEADS * GLA_DK), GLA_GATE_RANK ** -0.5),
        'b_gla_gate': nrm((L, GLA_HEADS * GLA_DK), 0.1),
        'gla_norm_g': 1.0 + nrm((L, GLA_HEADS * GLA_DV), 0.01),
        's5_a_re': -0.5 + nrm((L, S5_GROUPS, S5_STATE), 0.01),
        's5_a_im': math.pi * n_idx + nrm((L, S5_GROUPS, S5_STATE), 0.01),
        's5_log_dt': jax.random.uniform(next(ks), (L, S5_GROUPS), f32, math.log(S5_DT_MIN), math.log(S5_DT_MAX)),
        's5_b_re': nrm((L, S5_GROUPS, S5_STATE, S5_GROUP), (2 * S5_GROUP) ** -0.5),
        's5_b_im': nrm((L, S5_GROUPS, S5_STATE, S5_GROUP), (2 * S5_GROUP) ** -0.5),
        's5_c_re': nrm((L, S5_GROUPS, S5_GROUP, S5_STATE), (2 * S5_STATE) ** -0.5),
        's5_c_im': nrm((L, S5_GROUPS, S5_GROUP, S5_STATE), (2 * S5_STATE) ** -0.5),
        's5_d': nrm((L, MIX_W)),
        'w_s5_glu': nrm((L, MIX_W, MIX_W), MIX_W ** -0.5),
        'b_s5_glu': nrm((L, MIX_W), 0.02),
        'w_pool': nrm((L, len(POOL_WINDOWS), POOL_GROUP, POOL_GROUP), POOL_GROUP ** -0.5),
        'pool_scale': 1.0 + nrm((L, MIX_W), 0.1),
        'mla_q_norm': 1.0 + nrm((L, MLA_Q_RANK), 0.01),
        'w_q_up': nrm((L, MLA_Q_RANK, MLA_HEADS * (MLA_NOPE + MLA_ROPE)), MLA_Q_RANK ** -0.5),
        'mla_kv_norm': 1.0 + nrm((L, MLA_KV_RANK), 0.01),
        'w_kv_up': nrm((L, MLA_KV_RANK, MLA_HEADS * (MLA_NOPE + MLA_V)), MLA_KV_RANK ** -0.5),
        'w_branch': nrm((L, N_BRANCH, MIX_W, D_MODEL), BETA * MIX_W ** -0.5),
        'w_out': nrm((L, D_MODEL, D_MODEL), BETA * D_MODEL ** -0.5),
        'ln2_g': 1.0 + nrm((L, D_MODEL), 0.01),
        'ln2_b': nrm((L, D_MODEL), 0.01),
        'w_ffn_up': nrm((L, D_MODEL, 2 * D_FF), D_MODEL ** -0.5),
        'ffn_conv_w': nrm((L, CONV_W, D_FF), CONV_W ** -0.5),
        'ffn_conv_b': nrm((L, D_FF), 0.02),
        'w_ffn_down': nrm((L, D_FF, D_MODEL), BETA * D_FF ** -0.5),
    }


def reference(x_prompt, x_sample, cache_kv_latent, cache_k_rope, page_table, state_gla, state_s5_re, state_s5_im,
              state_pool, state_ffn_conv, ln1_g, ln1_b, w_in, b_gates, w_gla_gate, b_gla_gate, gla_norm_g,
              s5_a_re, s5_a_im, s5_log_dt, s5_b_re, s5_b_im, s5_c_re, s5_c_im, s5_d, w_s5_glu, b_s5_glu,
              w_pool, pool_scale, mla_q_norm, w_q_up, mla_kv_norm, w_kv_up, w_branch, w_out, ln2_g, ln2_b,
              w_ffn_up, ffn_conv_w, ffn_conv_b, w_ffn_down):
    params = {'ln1_g': ln1_g, 'ln1_b': ln1_b, 'w_in': w_in, 'b_gates': b_gates, 'w_gla_gate': w_gla_gate,
              'b_gla_gate': b_gla_gate, 'gla_norm_g': gla_norm_g, 's5_a_re': s5_a_re, 's5_a_im': s5_a_im,
              's5_log_dt': s5_log_dt, 's5_b_re': s5_b_re, 's5_b_im': s5_b_im, 's5_c_re': s5_c_re,
              's5_c_im': s5_c_im, 's5_d': s5_d, 'w_s5_glu': w_s5_glu, 'b_s5_glu': b_s5_glu, 'w_pool': w_pool,
              'pool_scale': pool_scale, 'mla_q_norm': mla_q_norm, 'w_q_up': w_q_up, 'mla_kv_norm': mla_kv_norm,
              'w_kv_up': w_kv_up, 'w_branch': w_branch, 'w_out': w_out, 'ln2_g': ln2_g, 'ln2_b': ln2_b,
              'w_ffn_up': w_ffn_up, 'ffn_conv_w': ffn_conv_w, 'ffn_conv_b': ffn_conv_b, 'w_ffn_down': w_ffn_down}
    B = x_prompt.shape[0]
    dt = x_prompt.dtype
    gla0 = jnp.zeros((DEPTH, B, GLA_HEADS, GLA_DK, GLA_DV), dt)
    s50 = jnp.zeros((DEPTH, B, S5_GROUPS, S5_STATE), dt)
    pool0 = jnp.zeros((DEPTH, B, POOL_BUF, MIX_W), dt)
    conv0 = jnp.zeros((DEPTH, B, CONV_W - 1, D_FF), dt)
    y_prompt, c_p, kr_p, gla_p, re_p, im_p, pool_p, conv_p = trunk(
        x_prompt, 0, None, None, None, gla0, s50, s50, pool0, conv0, params)
    past_len = page_table.shape[1] * PAGE_SIZE
    y_sample, c_s, kr_s, gla_s, re_s, im_s, pool_s, conv_s = trunk(
        x_sample, past_len, cache_kv_latent, cache_k_rope, page_table, state_gla, state_s5_re, state_s5_im,
        state_pool, state_ffn_conv, params)
    return (y_prompt, y_sample, c_p, kr_p, gla_p, re_p, im_p, pool_p, conv_p,
            c_s, kr_s, gla_s, re_s, im_s, pool_s, conv_s)
```

```python
import functools
import math

import jax
import jax.numpy as jnp
from jax import lax
from jax.experimental import pallas as pl
from jax.experimental.pallas import tpu as pltpu

F32 = jnp.float32
BF16 = jnp.bfloat16
HIGHEST = lax.Precision.HIGHEST

MIX_W = 512
N_BRANCH = 4
GLA_HEADS, GLA_DK, GLA_DV, GLA_GATE_RANK, GLA_TAU = 4, 64, 128, 16, 16.0
S5_GROUP, S5_GROUPS, S5_STATE = 16, 32, 64
S5_W = S5_GROUPS * S5_STATE
POOL_WINDOWS = (2, 4, 8, 16)
POOL_GROUP = 128
POOL_BUF = 15
MLA_HEADS, MLA_NOPE, MLA_ROPE, MLA_V, MLA_Q_RANK, MLA_KV_RANK = 4, 128, 64, 128, 384, 128
MLA_SCALE = (MLA_NOPE + MLA_ROPE) ** -0.5
MLA_QK = 256
ROPE_THETA = 10000.0
PAGE_SIZE = 128
CONV_W = 3
LN_EPS = 1e-5
RMS_EPS = 1e-6

Z_Q, Z_K, Z_V, Z_OG, Z_SU, Z_PU, Z_MQ, Z_MKV, Z_MKR, Z_MKR_ROT, Z_GA = (
    0, 256, 512, 1024, 1536, 2048, 2560, 2944, 3072, 3136, 3200)
Z_COLS = 3328

V7X_VMEM_BYTES = 64 * 1024 * 1024
VMEM_LIMIT = 48 * 1024 * 1024
SUBLANES = 8


def _cp(sem, vmem=VMEM_LIMIT):
    return pltpu.CompilerParams(dimension_semantics=sem, vmem_limit_bytes=vmem)


def _tile(dim, pref):
    t = min(dim, pref)
    while dim % t:
        t //= 2
    return t


def _sigmoid(x):
    return 1.0 / (1.0 + jnp.exp(-x))


def _gelu_tanh(x):
    c = math.sqrt(2.0 / math.pi)
    return 0.5 * x * (1.0 + jnp.tanh(c * (x + 0.044715 * (x * x * x))))


def _log_sigmoid(x):
    return jnp.minimum(x, 0.0) - jnp.log(1.0 + jnp.exp(-jnp.abs(x)))


def _layer_norm(y, g, b):
    mu = jnp.mean(y, -1, keepdims=True)
    d = y - mu
    var = jnp.mean(d * d, -1, keepdims=True)
    return d * lax.rsqrt(var + LN_EPS) * g + b


def _rms(x, g):
    return x * lax.rsqrt(jnp.mean(x * x, -1, keepdims=True) + RMS_EPS) * g


def _dot(a, b):
    return jnp.dot(a, b, preferred_element_type=F32)


def _dot_t(a, b):
    return lax.dot_general(a, b, (((1,), (1,)), ((), ())), preferred_element_type=F32)


def _dot_ta(a, b, **kw):
    return lax.dot_general(a, b, (((0,), (0,)), ((), ())), preferred_element_type=F32, **kw)


def _mm_kernel(x_ref, w_ref, o_ref):
    o_ref[...] = _dot(x_ref[...], w_ref[...]).astype(o_ref.dtype)


def matmul(x, w, tm, tn, out_dtype=F32):
    m, k = x.shape
    n = w.shape[1]
    return pl.pallas_call(
        _mm_kernel,
        grid=(m // tm, n // tn),
        in_specs=[pl.BlockSpec((tm, k), lambda i, j: (i, 0)),
                  pl.BlockSpec((k, tn), lambda i, j: (0, j))],
        out_specs=pl.BlockSpec((tm, tn), lambda i, j: (i, j)),
        out_shape=jax.ShapeDtypeStruct((m, n), out_dtype),
        compiler_params=_cp(("parallel", "parallel")),
        name="in_proj",
    )(x, w)


def _gla_kernel(q_ref, k_ref, v_ref, og_ref, ga_ref, wg_ref, bg_ref, ng_ref, s0_ref,
                ya_ref, sout_ref, s_ref, *, L):
    c = pl.program_id(1)

    @pl.when(c == 0)
    def _():
        s_ref[...] = s0_ref[0]

    pre = jnp.dot(ga_ref[...], wg_ref[...], precision=HIGHEST, preferred_element_type=F32) + bg_ref[...]
    log_a = _log_sigmoid(pre) * (1.0 / GLA_TAU)
    row = lax.broadcasted_iota(jnp.int32, (L, L), 0)
    col = lax.broadcasted_iota(jnp.int32, (L, L), 1)
    causal = row >= col
    bc = jnp.dot(causal.astype(F32), log_a, precision=HIGHEST, preferred_element_type=F32)
    bc_last = bc[L - 1:L, :]
    q = q_ref[...] * (GLA_DK ** -0.5)
    k = k_ref[...]
    q_in = q * jnp.exp(bc)
    k_in = k * jnp.exp(-bc)
    k_out = k * jnp.exp(bc_last - bc)
    ones = jnp.ones((L, GLA_DV), F32)
    for h in range(GLA_HEADS):
        ks = slice(h * GLA_DK, (h + 1) * GLA_DK)
        vs = slice(h * GLA_DV, (h + 1) * GLA_DV)
        qh = q_in[:, ks].astype(BF16)
        vh = v_ref[:, vs].astype(BF16)
        s_old = s_ref[h]
        scores = jnp.where(causal, _dot_t(qh, k_in[:, ks].astype(BF16)), 0.0)
        o = _dot(scores.astype(BF16), vh) + _dot(qh, s_old.astype(BF16))
        a_tot = jnp.exp(_dot_ta(log_a[:, ks], ones, precision=HIGHEST))
        s_ref[h] = a_tot * s_old + _dot_ta(k_out[:, ks].astype(BF16), vh)
        o = o * lax.rsqrt(jnp.mean(o * o, -1, keepdims=True) + RMS_EPS) * ng_ref[:, vs]
        g = og_ref[:, vs]
        ya_ref[:, vs] = (o * (g * _sigmoid(g))).astype(ya_ref.dtype)

    @pl.when(c == pl.num_programs(1) - 1)
    def _():
        sout_ref[0] = s_ref[...]


def gla(z, wg_pad, bg, ng, s0, nb, t, L):
    nc = t // L
    rb = lambda b, c: b * nc + c
    hk = GLA_HEADS * GLA_DK
    return pl.pallas_call(
        functools.partial(_gla_kernel, L=L),
        grid=(nb, nc),
        in_specs=[
            pl.BlockSpec((L, hk), lambda b, c: (rb(b, c), Z_Q // hk)),
            pl.BlockSpec((L, hk), lambda b, c: (rb(b, c), Z_K // hk)),
            pl.BlockSpec((L, MIX_W), lambda b, c: (rb(b, c), Z_V // MIX_W)),
            pl.BlockSpec((L, MIX_W), lambda b, c: (rb(b, c), Z_OG // MIX_W)),
            pl.BlockSpec((L, 128), lambda b, c: (rb(b, c), Z_GA // 128)),
            pl.BlockSpec((128, hk), lambda b, c: (0, 0)),
            pl.BlockSpec((1, hk), lambda b, c: (0, 0)),
            pl.BlockSpec((1, MIX_W), lambda b, c: (0, 0)),
            pl.BlockSpec((1, GLA_HEADS, GLA_DK, GLA_DV), lambda b, c: (b, 0, 0, 0)),
        ],
        out_specs=[
            pl.BlockSpec((L, MIX_W), lambda b, c: (rb(b, c), 0)),
            pl.BlockSpec((1, GLA_HEADS, GLA_DK, GLA_DV), lambda b, c: (b, 0, 0, 0)),
        ],
        out_shape=[jax.ShapeDtypeStruct((nb * t, MIX_W), BF16),
                   jax.ShapeDtypeStruct((nb, GLA_HEADS, GLA_DK, GLA_DV), F32)],
        scratch_shapes=[pltpu.VMEM((GLA_HEADS, GLA_DK, GLA_DV), F32)],
        compiler_params=_cp(("parallel", "arbitrary")),
        name="gla",
    )(z, z, z, z, z, wg_pad, bg, ng, s0)


TAB_P8, TAB_T1, TAB_T2, TAB_T4, TAB_F = 0, 2, 4, 6, 8
N_TAB = 10


def _s5_tab_kernel(are_ref, aim_ref, ldt_ref, tab_ref):
    lr = are_ref[0]
    li = aim_ref[0]
    dt = jnp.exp(ldt_ref[0])
    mag = jnp.exp(lr * dt)
    a_re = mag * jnp.cos(li * dt)
    a_im = mag * jnp.sin(li * dt)
    shp = (SUBLANES, S5_W)
    r = lax.broadcasted_iota(jnp.int32, shp, 0)
    pows = [(a_re, a_im)]
    for _ in range(SUBLANES - 1):
        pr, pi = pows[-1]
        pows.append((pr * a_re - pi * a_im, pr * a_im + pi * a_re))
    p8r = jnp.zeros(shp, F32)
    p8i = jnp.zeros(shp, F32)
    for j, (pr, pi) in enumerate(pows):
        p8r = jnp.where(r == j, pr, p8r)
        p8i = jnp.where(r == j, pi, p8i)
    tab_ref[0, TAB_P8] = p8r
    tab_ref[0, TAB_P8 + 1] = p8i
    for s, idx in ((1, TAB_T1), (2, TAB_T2), (4, TAB_T4)):
        pr, pi = pows[s - 1]
        tab_ref[0, idx] = jnp.where(r >= s, pr, 0.0)
        tab_ref[0, idx + 1] = jnp.where(r >= s, pi, 0.0)
    den = lr * lr + li * li
    nr, ni = a_re - 1.0, a_im
    tab_ref[0, TAB_F] = jnp.broadcast_to((nr * lr + ni * li) / den, shp)
    tab_ref[0, TAB_F + 1] = jnp.broadcast_to((ni * lr - nr * li) / den, shp)


def s5_tables(a_re, a_im, log_dt):
    depth = a_re.shape[0]
    flat = lambda a: a.reshape(depth, 1, S5_W)
    ldt = jnp.broadcast_to(log_dt[:, :, None], (depth, S5_GROUPS, S5_STATE))
    spec = pl.BlockSpec((1, 1, S5_W), lambda l: (l, 0, 0))
    return pl.pallas_call(
        _s5_tab_kernel,
        grid=(depth,),
        in_specs=[spec, spec, spec],
        out_specs=pl.BlockSpec((1, N_TAB, SUBLANES, S5_W), lambda l: (l, 0, 0, 0)),
        out_shape=jax.ShapeDtypeStruct((depth, N_TAB, SUBLANES, S5_W), F32),
        compiler_params=_cp(("parallel",)),
        name="s5_tables",
    )(flat(a_re), flat(a_im), flat(ldt))


def _s5_block_scan(xr, xi, tab_ref, cr, ci):
    for s, idx in ((1, TAB_T1), (2, TAB_T2), (4, TAB_T4)):
        tr = tab_ref[idx]
        ti = tab_ref[idx + 1]
        sr = pltpu.roll(xr, s, 0)
        si = pltpu.roll(xi, s, 0)
        xr, xi = xr + (tr * sr - ti * si), xi + (tr * si + ti * sr)
    pr = tab_ref[TAB_P8]
    pi = tab_ref[TAB_P8 + 1]
    hr = xr + (pr * cr - pi * ci)
    hi = xi + (pr * ci + pi * cr)
    return hr, hi


def _s5_kernel(u_ref, tab_ref, wbr_ref, wbi_ref, wcr_ref, wci_ref, d_ref, wglu_ref, bglu_ref,
               h0r_ref, h0i_ref, yb_ref, hr_out_ref, hi_out_ref, xr_ref, xi_ref, cr_ref, ci_ref,
               *, R, chained):
    half_u = MIX_W // 2
    half_s = S5_W // 2
    u = u_ref[...]
    ub = u.astype(BF16)
    fr = tab_ref[TAB_F][0:1]
    fi = tab_ref[TAB_F + 1][0:1]
    for hf in range(2):
        us = ub[:, hf * half_u:(hf + 1) * half_u]
        ss = slice(hf * half_s, (hf + 1) * half_s)
        bur = _dot(us, wbr_ref[hf])
        bui = _dot(us, wbi_ref[hf])
        xr_ref[:, ss] = fr[:, ss] * bur - fi[:, ss] * bui
        xi_ref[:, ss] = fr[:, ss] * bui + fi[:, ss] * bur

    nblk = R // SUBLANES
    if chained:
        @pl.when(pl.program_id(1) == 0)
        def _():
            cr_ref[...] = h0r_ref[0]
            ci_ref[...] = h0i_ref[0]

        def body(j, carry):
            cr, ci = carry
            rows = pl.ds(pl.multiple_of(j * SUBLANES, SUBLANES), SUBLANES)
            hr, hi = _s5_block_scan(xr_ref[rows, :], xi_ref[rows, :], tab_ref, cr, ci)
            xr_ref[rows, :] = hr
            xi_ref[rows, :] = hi
            return hr[SUBLANES - 1:SUBLANES], hi[SUBLANES - 1:SUBLANES]

        cr, ci = lax.fori_loop(0, nblk, body, (cr_ref[...], ci_ref[...]))
        cr_ref[...] = cr
        ci_ref[...] = ci

        @pl.when(pl.program_id(1) == pl.num_programs(1) - 1)
        def _():
            hr_out_ref[0] = cr
            hi_out_ref[0] = ci
    else:
        for j in range(nblk):
            rows = slice(j * SUBLANES, (j + 1) * SUBLANES)
            hr, hi = _s5_block_scan(xr_ref[rows, :], xi_ref[rows, :], tab_ref,
                                    h0r_ref[j:j + 1, :], h0i_ref[j:j + 1, :])
            xr_ref[rows, :] = hr
            xi_ref[rows, :] = hi
            hr_out_ref[j:j + 1, :] = hr[SUBLANES - 1:SUBLANES]
            hi_out_ref[j:j + 1, :] = hi[SUBLANES - 1:SUBLANES]

    for hf in range(2):
        ss = slice(hf * half_s, (hf + 1) * half_s)
        y = _dot(xr_ref[:, ss].astype(BF16), wcr_ref[hf]) - _dot(xi_ref[:, ss].astype(BF16), wci_ref[hf])
        us = slice(hf * half_u, (hf + 1) * half_u)
        xr_ref[:, us] = _gelu_tanh(y + d_ref[:, us] * u[:, us])
    y = xr_ref[:, 0:MIX_W]
    gate = _sigmoid(_dot(y.astype(BF16), wglu_ref[...]) + bglu_ref[...])
    yb_ref[...] = (y * gate).astype(yb_ref.dtype)


def s5(z, tab_l, wbr, wbi, wcr, wci, d, wglu, bglu, h0r, h0i, nb, t, chained, R):
    if chained:
        nc = t // R
        grid = (nb, nc)
        rb = lambda b, c: b * nc + c
        h0 = (h0r.reshape(nb, 1, S5_W), h0i.reshape(nb, 1, S5_W))
        st_spec = pl.BlockSpec((1, 1, S5_W), lambda b, c: (b, 0, 0))
        st_shape = jax.ShapeDtypeStruct((nb, 1, S5_W), F32)
    else:
        assert t == SUBLANES
        nseq = R // SUBLANES
        grid = (nb // nseq, 1)
        rb = lambda b, c: b
        h0 = (h0r, h0i)
        st_spec = pl.BlockSpec((nseq, S5_W), lambda b, c: (b, 0))
        st_shape = jax.ShapeDtypeStruct((nb, S5_W), F32)
    const = lambda shape: pl.BlockSpec(shape, lambda b, c: (0,) * len(shape))
    yb, hr, hi = pl.pallas_call(
        functools.partial(_s5_kernel, R=R, chained=chained),
        grid=grid,
        in_specs=[
            pl.BlockSpec((R, MIX_W), lambda b, c: (rb(b, c), Z_SU // MIX_W)),
            const((N_TAB, SUBLANES, S5_W)),
            const((2, MIX_W // 2, S5_W // 2)), const((2, MIX_W // 2, S5_W // 2)),
            const((2, S5_W // 2, MIX_W // 2)), const((2, S5_W // 2, MIX_W // 2)),
            const((1, MIX_W)), const((MIX_W, MIX_W)), const((1, MIX_W)),
            st_spec, st_spec,
        ],
        out_specs=[pl.BlockSpec((R, MIX_W), lambda b, c: (rb(b, c), 0)), st_spec, st_spec],
        out_shape=[jax.ShapeDtypeStruct((nb * t, MIX_W), BF16), st_shape, st_shape],
        scratch_shapes=[pltpu.VMEM((R, S5_W), F32), pltpu.VMEM((R, S5_W), F32),
                        pltpu.VMEM((1, S5_W), F32), pltpu.VMEM((1, S5_W), F32)],
        compiler_params=_cp(("parallel", "arbitrary")),
        name="s5",
    )(z, tab_l, wbr, wbi, wcr, wci, d, wglu, bglu, *h0)
    return yb, hr.reshape(nb, S5_GROUPS, S5_STATE), hi.reshape(nb, S5_GROUPS, S5_STATE)


HALO = 16


def _pool_windows(ext_ref, base, n, pos, wp_ref, sc_ref, out_ref, out_rows):
    for g, w in enumerate(POOL_WINDOWS):
        cs = slice(g * POOL_GROUP, (g + 1) * POOL_GROUP)
        u = ext_ref[pl.ds(base, n), cs]
        s = u
        for j in range(1, w):
            s = s + ext_ref[pl.ds(base - j, n), cs]
        cnt = jnp.minimum(pos + 1, w).astype(F32)
        dlt = s / cnt - u
        y = _dot(dlt.astype(BF16), wp_ref[g]) * sc_ref[:, cs]
        out_ref[out_rows, cs] = y.astype(out_ref.dtype)


def _pool_chain_kernel(u_ref, wp_ref, sc_ref, pre_ref, yc_ref, buf_ref, ext_ref, *, L, pos0):
    c = pl.program_id(1)

    @pl.when(c == 0)
    def _():
        ext_ref[0:HALO, :] = pre_ref[0]

    ext_ref[HALO:HALO + L, :] = u_ref[...]
    pos = pos0 + c * L + lax.broadcasted_iota(jnp.int32, (L, 1), 0)
    _pool_windows(ext_ref, HALO, L, pos, wp_ref, sc_ref, yc_ref, slice(None))
    tail = ext_ref[L:L + HALO, :]
    ext_ref[0:HALO, :] = tail

    @pl.when(c == pl.num_programs(1) - 1)
    def _():
        buf_ref[0] = tail


def _pool_seq_kernel(u_ref, wp_ref, sc_ref, pre_ref, yc_ref, buf_ref, ext_ref, *, nseq, pos0):
    t = SUBLANES
    pos = pos0 + lax.broadcasted_iota(jnp.int32, (t, 1), 0)
    for j in range(nseq):
        ext_ref[0:HALO, :] = pre_ref[j]
        ext_ref[HALO:HALO + t, :] = u_ref[j * t:(j + 1) * t, :]
        _pool_windows(ext_ref, HALO, t, pos, wp_ref, sc_ref, yc_ref, slice(j * t, (j + 1) * t))
        buf_ref[j] = ext_ref[t:t + HALO, :]


def pool(z, wp, sc, prefix, nb, t, pos0, chained, L):
    pre = jnp.pad(prefix, ((0, 0), (1, 0), (0, 0)))
    const = lambda shape: pl.BlockSpec(shape, lambda b, c: (0,) * len(shape))
    if chained:
        nc = t // L
        grid = (nb, nc)
        kern = functools.partial(_pool_chain_kernel, L=L, pos0=pos0)
        u_spec = pl.BlockSpec((L, MIX_W), lambda b, c: (b * nc + c, Z_PU // MIX_W))
        y_spec = pl.BlockSpec((L, MIX_W), lambda b, c: (b * nc + c, 0))
        st_spec = pl.BlockSpec((1, HALO, MIX_W), lambda b, c: (b, 0, 0))
        ext_rows = HALO + L
    else:
        assert t == SUBLANES
        nseq = L // t
        grid = (nb // nseq, 1)
        kern = functools.partial(_pool_seq_kernel, nseq=nseq, pos0=pos0)
        u_spec = pl.BlockSpec((L, MIX_W), lambda b, c: (b, Z_PU // MIX_W))
        y_spec = pl.BlockSpec((L, MIX_W), lambda b, c: (b, 0))
        st_spec = pl.BlockSpec((nseq, HALO, MIX_W), lambda b, c: (b, 0, 0))
        ext_rows = HALO + t
    yc, buf = pl.pallas_call(
        kern,
        grid=grid,
        in_specs=[u_spec, const((len(POOL_WINDOWS), POOL_GROUP, POOL_GROUP)), const((1, MIX_W)), st_spec],
        out_specs=[y_spec, st_spec],
        out_shape=[jax.ShapeDtypeStruct((nb * t, MIX_W), BF16),
                   jax.ShapeDtypeStruct((nb, HALO, MIX_W), F32)],
        scratch_shapes=[pltpu.VMEM((ext_rows, MIX_W), F32)],
        compiler_params=_cp(("parallel", "arbitrary")),
        name="pool",
    )(z, wp, sc, pre)
    return yc, buf[:, 1:, :]


def _mla_prep_kernel(zq_ref, zr_ref, cq_ref, sq_ref, qg_ref, kg_ref, wq_ref, wuk_ref,
                     qcat_ref, kcat_ref, c_ref, kr_ref):
    nq = MLA_HEADS * MLA_NOPE
    nr = MLA_HEADS * MLA_ROPE
    mq = zq_ref[:, 0:MLA_Q_RANK]
    mkv = zq_ref[:, MLA_Q_RANK:MLA_Q_RANK + MLA_KV_RANK]
    qd = _dot(_rms(mq, qg_ref[...]).astype(BF16), wq_ref[...])
    cos4 = cq_ref[...]
    sin4 = sq_ref[...]
    q_rope = qd[:, nq:nq + nr] * cos4 + qd[:, nq + nr:nq + 2 * nr] * sin4
    zpad = jnp.zeros((qd.shape[0], MLA_QK - MLA_KV_RANK - MLA_ROPE), qcat_ref.dtype)
    for h in range(MLA_HEADS):
        q_lat = _dot(qd[:, h * MLA_NOPE:(h + 1) * MLA_NOPE].astype(BF16), wuk_ref[h])
        qcat_ref[h, :, 0:MLA_KV_RANK] = q_lat.astype(qcat_ref.dtype)
        qcat_ref[h, :, MLA_KV_RANK:MLA_KV_RANK + MLA_ROPE] = (
            q_rope[:, h * MLA_ROPE:(h + 1) * MLA_ROPE].astype(qcat_ref.dtype))
        qcat_ref[h, :, MLA_KV_RANK + MLA_ROPE:MLA_QK] = zpad
    c_new = _rms(mkv, kg_ref[...])
    kr_new = (zr_ref[:, 0:MLA_ROPE] * cos4[:, 0:MLA_ROPE]
              + zr_ref[:, MLA_ROPE:2 * MLA_ROPE] * sin4[:, 0:MLA_ROPE])
    c_ref[...] = c_new
    kr_ref[...] = kr_new
    kcat_ref[:, 0:MLA_KV_RANK] = c_new.astype(kcat_ref.dtype)
    kcat_ref[:, MLA_KV_RANK:MLA_KV_RANK + MLA_ROPE] = kr_new.astype(kcat_ref.dtype)
    kcat_ref[:, MLA_KV_RANK + MLA_ROPE:MLA_QK] = zpad


def mla_prep(z, cos4, sin4, qg, kg, wq, wuk, L, cat_dtype):
    n = z.shape[0]
    nt = cos4.shape[0] // L
    const = lambda shape: pl.BlockSpec(shape, lambda i: (0,) * len(shape))
    tab_spec = pl.BlockSpec((L, MLA_HEADS * MLA_ROPE), lambda i: (i % nt, 0))
    return pl.pallas_call(
        _mla_prep_kernel,
        grid=(n // L,),
        in_specs=[
            pl.BlockSpec((L, MIX_W), lambda i: (i, Z_MQ // MIX_W)),
            pl.BlockSpec((L, 128), lambda i: (i, Z_MKR // 128)),
            tab_spec, tab_spec,
            const((1, MLA_Q_RANK)), const((1, MLA_KV_RANK)),
            const((MLA_Q_RANK, MLA_HEADS * (MLA_NOPE + 2 * MLA_ROPE))),
            const((MLA_HEADS, MLA_NOPE, MLA_KV_RANK)),
        ],
        out_specs=[
            pl.BlockSpec((MLA_HEADS, L, MLA_QK), lambda i: (0, i, 0)),
            pl.BlockSpec((L, MLA_QK), lambda i: (i, 0)),
            pl.BlockSpec((L, MLA_KV_RANK), lambda i: (i, 0)),
            pl.BlockSpec((L, MLA_ROPE), lambda i: (i, 0)),
        ],
        out_shape=[
            jax.ShapeDtypeStruct((MLA_HEADS, n, MLA_QK), cat_dtype),
            jax.ShapeDtypeStruct((n, MLA_QK), cat_dtype),
            jax.ShapeDtypeStruct((n, MLA_KV_RANK), F32),
            jax.ShapeDtypeStruct((n, MLA_ROPE), F32),
        ],
        compiler_params=_cp(("parallel",)),
        name="mla_prep",
    )(z, z, cos4, sin4, qg, kg, wq, wuk)


def _softmax_step(s, v, m_ref, l_ref, acc_ref):
    m_old = m_ref[...]
    m_new = jnp.maximum(m_old, jnp.max(s, -1, keepdims=True))
    alpha = jnp.exp(m_old - m_new)
    p = jnp.exp(s - m_new)
    l_ref[...] = alpha * l_ref[...] + jnp.sum(p, -1, keepdims=True)
    acc_ref[...] = alpha * acc_ref[...] + _dot(p.astype(BF16), v)
    m_ref[...] = m_new


def _causal_mask(tq, tk):
    row = lax.broadcasted_iota(jnp.int32, (MLA_HEADS * tq, tk), 0)
    key = lax.broadcasted_iota(jnp.int32, (MLA_HEADS * tq, tk), 1)
    head = sum((row >= h * tq).astype(jnp.int32) for h in range(1, MLA_HEADS))
    return key <= row - head * tq


def _attn_init(m_ref, l_ref, acc_ref):
    m_ref[...] = jnp.full(m_ref.shape, -jnp.inf, F32)
    l_ref[...] = jnp.zeros(l_ref.shape, F32)
    acc_ref[...] = jnp.zeros(acc_ref.shape, F32)


def _attn_finish(acc_ref, l_ref, wuv_ref, yd_ref, rows):
    o = acc_ref[...] / l_ref[...]
    for h in range(MLA_HEADS):
        oh = o[h * rows:(h + 1) * rows, :].astype(BF16)
        yd_ref[:, h * MLA_V:(h + 1) * MLA_V] = _dot(oh, wuv_ref[h]).astype(yd_ref.dtype)


def _mla_causal_kernel(q_ref, k_ref, wuv_ref, yd_ref, m_ref, l_ref, acc_ref, *, tq):
    qi = pl.program_id(1)
    kj = pl.program_id(2)

    @pl.when(kj == 0)
    def _():
        _attn_init(m_ref, l_ref, acc_ref)

    def step(masked):
        q = q_ref[...].reshape(MLA_HEADS * tq, MLA_QK)
        k = k_ref[...]
        s = _dot_t(q, k) * MLA_SCALE
        if masked:
            s = jnp.where(_causal_mask(tq, tq), s, -jnp.inf)
        _softmax_step(s, k[:, 0:MLA_KV_RANK], m_ref, l_ref, acc_ref)

    @pl.when(kj < qi)
    def _():
        step(False)

    @pl.when(kj == qi)
    def _():
        step(True)
        _attn_finish(acc_ref, l_ref, wuv_ref, yd_ref, tq)


def mla_causal(qcat, kcat, wuv, nb, t, tq):
    nq = t // tq
    return pl.pallas_call(
        functools.partial(_mla_causal_kernel, tq=tq),
        grid=(nb, nq, nq),
        in_specs=[
            pl.BlockSpec((MLA_HEADS, tq, MLA_QK), lambda b, i, j: (0, b * nq + i, 0)),
            pl.BlockSpec((tq, MLA_QK), lambda b, i, j: (b * nq + jnp.minimum(i, j), 0)),
            pl.BlockSpec((MLA_HEADS, MLA_KV_RANK, MLA_V), lambda b, i, j: (0, 0, 0)),
        ],
        out_specs=pl.BlockSpec((tq, MIX_W), lambda b, i, j: (b * nq + i, 0)),
        out_shape=jax.ShapeDtypeStruct((nb * t, MIX_W), BF16),
        scratch_shapes=[pltpu.VMEM((MLA_HEADS * tq, 1), F32), pltpu.VMEM((MLA_HEADS * tq, 1), F32),
                        pltpu.VMEM((MLA_HEADS * tq, MLA_KV_RANK), F32)],
        compiler_params=_cp(("parallel", "parallel", "arbitrary")),
        name="mla_causal",
    )(qcat, kcat, wuv)


def _mla_paged_kernel(pt_ref, q_ref, kn_ref, cache_c_ref, cache_kr_ref, wuv_ref, yd_ref,
                      cbuf_ref, krbuf_ref, sem_ref, m_ref, l_ref, acc_ref, *, layer, n_pages, kchunk):
    b = pl.program_id(0)
    nb = pl.num_programs(0)
    t = SUBLANES
    past = n_pages * PAGE_SIZE

    def page_copies(bb, slot, p):
        page = pt_ref[bb * n_pages + p]
        rows = pl.ds(p * PAGE_SIZE, PAGE_SIZE)
        return (pltpu.make_async_copy(cache_c_ref.at[layer, page], cbuf_ref.at[slot, rows, :], sem_ref.at[0, slot]),
                pltpu.make_async_copy(cache_kr_ref.at[layer, page], krbuf_ref.at[slot, rows, :], sem_ref.at[1, slot]))

    def fetch(bb, slot):
        def body(p, _):
            for cp in page_copies(bb, slot, p):
                cp.start()
            return 0
        lax.fori_loop(0, n_pages, body, 0)

    def wait(bb, slot):
        def body(p, _):
            for cp in page_copies(bb, slot, p):
                cp.wait()
            return 0
        lax.fori_loop(0, n_pages, body, 0)

    slot = b % 2

    @pl.when(b == 0)
    def _():
        fetch(b, slot)

    @pl.when(b + 1 < nb)
    def _():
        fetch(b + 1, 1 - slot)

    wait(b, slot)

    _attn_init(m_ref, l_ref, acc_ref)
    q = q_ref[...].reshape(MLA_HEADS * t, MLA_QK).astype(BF16)
    q_lat = q[:, 0:MLA_KV_RANK]
    q_rope = q[:, MLA_KV_RANK:MLA_KV_RANK + MLA_ROPE]

    def chunk(i, _):
        rows = pl.ds(pl.multiple_of(i * kchunk, kchunk), kchunk)
        c = cbuf_ref[slot, rows, :].astype(BF16)
        kr = krbuf_ref[slot, rows, :].astype(BF16)
        s = (_dot_t(q_lat, c) + _dot_t(q_rope, kr)) * MLA_SCALE
        _softmax_step(s, c, m_ref, l_ref, acc_ref)
        return 0

    lax.fori_loop(0, past // kchunk, chunk, 0)

    kn = kn_ref[...].astype(BF16)
    s = _dot_t(q, kn) * MLA_SCALE
    s = jnp.where(_causal_mask(t, t), s, -jnp.inf)
    _softmax_step(s, kn[:, 0:MLA_KV_RANK], m_ref, l_ref, acc_ref)
    _attn_finish(acc_ref, l_ref, wuv_ref, yd_ref, t)


def mla_paged(qcat, kcat, cache_c, cache_kr, page_table, wuv, layer):
    nb, n_pages = page_table.shape
    t = SUBLANES
    past = n_pages * PAGE_SIZE
    kchunk = _tile(past, 1024)
    grid_spec = pltpu.PrefetchScalarGridSpec(
        num_scalar_prefetch=1,
        grid=(nb,),
        in_specs=[
            pl.BlockSpec((MLA_HEADS, t, MLA_QK), lambda b, pt: (0, b, 0)),
            pl.BlockSpec((t, MLA_QK), lambda b, pt: (b, 0)),
            pl.BlockSpec(memory_space=pl.ANY),
            pl.BlockSpec(memory_space=pl.ANY),
            pl.BlockSpec((MLA_HEADS, MLA_KV_RANK, MLA_V), lambda b, pt: (0, 0, 0)),
        ],
        out_specs=pl.BlockSpec((t, MIX_W), lambda b, pt: (b, 0)),
        scratch_shapes=[
            pltpu.VMEM((2, past, MLA_KV_RANK), F32),
            pltpu.VMEM((2, past, MLA_ROPE), F32),
            pltpu.SemaphoreType.DMA((2, 2)),
            pltpu.VMEM((MLA_HEADS * t, 1), F32), pltpu.VMEM((MLA_HEADS * t, 1), F32),
            pltpu.VMEM((MLA_HEADS * t, MLA_KV_RANK), F32),
        ],
    )
    return pl.pallas_call(
        functools.partial(_mla_paged_kernel, layer=layer, n_pages=n_pages, kchunk=kchunk),
        grid_spec=grid_spec,
        out_shape=jax.ShapeDtypeStruct((nb * t, MIX_W), F32),
        compiler_params=_cp(("arbitrary",)),
        name="mla_paged",
    )(page_table.reshape(-1), qcat, kcat, cache_c, cache_kr, wuv)


def _merge_kernel(x_ref, ya_ref, yb_ref, yc_ref, yd_ref, wg_ref, bg_ref, wb_ref, o_ref):
    x = x_ref[...]
    acc = None
    for g, br in enumerate((ya_ref, yb_ref, yc_ref, yd_ref)):
        gate = _sigmoid(_dot(x, wg_ref[g]) + bg_ref[g])
        term = _dot(br[...].astype(BF16), wb_ref[g]) * gate
        acc = term if acc is None else acc + term
    o_ref[...] = acc.astype(o_ref.dtype)


def merge(xb, ys, wg, bg, wb, tm, tn):
    n, d = xb.shape
    row = lambda shape: pl.BlockSpec(shape, lambda j, i: (i, 0))
    return pl.pallas_call(
        _merge_kernel,
        grid=(d // tn, n // tm),
        in_specs=[row((tm, d)), row((tm, MIX_W)), row((tm, MIX_W)), row((tm, MIX_W)), row((tm, MIX_W)),
                  pl.BlockSpec((N_BRANCH, d, tn), lambda j, i: (0, 0, j)),
                  pl.BlockSpec((N_BRANCH, 1, tn), lambda j, i: (0, 0, j)),
                  pl.BlockSpec((N_BRANCH, MIX_W, tn), lambda j, i: (0, 0, j))],
        out_specs=pl.BlockSpec((tm, tn), lambda j, i: (i, j)),
        out_shape=jax.ShapeDtypeStruct((n, d), BF16),
        compiler_params=_cp(("parallel", "parallel")),
        name="merge",
    )(xb, *ys, wg, bg, wb)


def _out_ln_kernel(m_ref, w_ref, x_ref, g_ref, b_ref, o_ref, ob_ref, *, alpha):
    y = alpha * x_ref[...] + _dot(m_ref[...], w_ref[...])
    y = _layer_norm(y, g_ref[...], b_ref[...])
    o_ref[...] = y
    ob_ref[...] = y.astype(ob_ref.dtype)


def out_ln(merged, w_out, x, g, b, alpha, tm):
    n, d = x.shape
    row = lambda dt: pl.BlockSpec((tm, d), lambda i: (i, 0))
    const = lambda shape: pl.BlockSpec(shape, lambda i: (0, 0))
    return pl.pallas_call(
        functools.partial(_out_ln_kernel, alpha=alpha),
        grid=(n // tm,),
        in_specs=[row(BF16), const((d, d)), row(F32), const((1, d)), const((1, d))],
        out_specs=[row(F32), row(BF16)],
        out_shape=[jax.ShapeDtypeStruct((n, d), F32), jax.ShapeDtypeStruct((n, d), BF16)],
        compiler_params=_cp(("parallel",)),
        name="out_ln",
    )(merged, w_out, x, g, b)


FFN_HALO = 16


def _ffn_kernel(x_ref, halo_ref, wh_ref, wg_ref, wd_ref, cw_ref, cb_ref, st_ref, g_ref, b_ref,
                o_ref, ob_ref, hl_ref, xcat_ref, hs_ref, acc_ref, *, tm, alpha, chained, tiles_per_seq):
    i = pl.program_id(0)
    f = pl.program_id(1)

    @pl.when(f == 0)
    def _():
        acc_ref[...] = jnp.zeros(acc_ref.shape, F32)
        xcat_ref[FFN_HALO:, :] = x_ref[...].astype(BF16)
        if chained:
            seq_start = (i % tiles_per_seq) == 0
            xcat_ref[0:FFN_HALO, :] = jnp.where(seq_start, 0.0, halo_ref[...]).astype(BF16)

    w0 = cw_ref[0:1, :]
    w1 = cw_ref[1:2, :]
    w2 = cw_ref[2:3, :]
    if chained:
        hs_ref[...] = _dot(xcat_ref[...], wh_ref[...])
        h = hs_ref[FFN_HALO:, :]
        hm1 = hs_ref[pl.ds(FFN_HALO - 1, tm), :]
        hm2 = hs_ref[pl.ds(FFN_HALO - 2, tm), :]
        hl_ref[0] = hs_ref[tm + FFN_HALO - SUBLANES:, :]
        conv = cb_ref[...] + w0 * hm2 + w1 * hm1 + w2 * h
    else:
        t = SUBLANES
        nseq = tm // t
        tf = wh_ref.shape[1]
        h = _dot(xcat_ref[FFN_HALO:, :], wh_ref[...]).reshape(nseq, t, tf)
        r = lax.broadcasted_iota(jnp.int32, (nseq, t, tf), 1)
        b0 = st_ref[:, 0:1, :]
        b1 = st_ref[:, 1:2, :]
        hm1 = jnp.where(r >= 1, pltpu.roll(h, 1, 1), b1)
        hm2 = jnp.where(r >= 2, pltpu.roll(h, 2, 1), jnp.where(r == 1, b1, b0))
        hl_ref[...] = h[:, t - (CONV_W - 1):, :]
        conv = (cb_ref[...] + w0 * hm2 + w1 * hm1 + w2 * h).reshape(tm, tf)
    gate = _dot(xcat_ref[FFN_HALO:, :], wg_ref[...])
    a = (_gelu_tanh(conv) * gate).astype(BF16)
    acc_ref[...] += _dot(a, wd_ref[...])

    @pl.when(f == pl.num_programs(1) - 1)
    def _():
        y = _layer_norm(alpha * x_ref[...] + acc_ref[...], g_ref[...], b_ref[...])
        o_ref[...] = y
        ob_ref[...] = y.astype(ob_ref.dtype)


def conv_ffn(x, w_up, w_down, cw, cb, state, g, b, alpha, nb, t, chained, tm, tf):
    n, d = x.shape
    ff = w_down.shape[0]
    nf = ff // tf
    const = lambda shape: pl.BlockSpec(shape, lambda i, f: (0, 0))
    hb = tm // FFN_HALO
    if chained:
        tiles_per_seq = t // tm
        st = jnp.zeros((1, CONV_W - 1, tf), F32)
        st_spec = pl.BlockSpec((1, CONV_W - 1, tf), lambda i, f: (0, 0, 0))
        hl_shape = jax.ShapeDtypeStruct((n // tm, SUBLANES, ff), F32)
        hl_spec = pl.BlockSpec((1, SUBLANES, tf), lambda i, f: (i, 0, f))
    else:
        assert t == SUBLANES
        tiles_per_seq = 1
        st = state
        st_spec = pl.BlockSpec((tm // t, CONV_W - 1, tf), lambda i, f: (i, 0, f))
        hl_shape = jax.ShapeDtypeStruct((nb, CONV_W - 1, ff), F32)
        hl_spec = pl.BlockSpec((tm // t, CONV_W - 1, tf), lambda i, f: (i, 0, f))
    o, ob, hl = pl.pallas_call(
        functools.partial(_ffn_kernel, tm=tm, alpha=alpha, chained=chained, tiles_per_seq=tiles_per_seq),
        grid=(n // tm, nf),
        in_specs=[
            pl.BlockSpec((tm, d), lambda i, f: (i, 0)),
            pl.BlockSpec((FFN_HALO, d), lambda i, f: (jnp.maximum(i * hb - 1, 0), 0)),
            pl.BlockSpec((d, tf), lambda i, f: (0, f)),
            pl.BlockSpec((d, tf), lambda i, f: (0, nf + f)),
            pl.BlockSpec((tf, d), lambda i, f: (f, 0)),
            pl.BlockSpec((CONV_W, tf), lambda i, f: (0, f)),
            pl.BlockSpec((1, tf), lambda i, f: (0, f)),
            st_spec, const((1, d)), const((1, d)),
        ],
        out_specs=[pl.BlockSpec((tm, d), lambda i, f: (i, 0)), pl.BlockSpec((tm, d), lambda i, f: (i, 0)), hl_spec],
        out_shape=[jax.ShapeDtypeStruct((n, d), F32), jax.ShapeDtypeStruct((n, d), BF16), hl_shape],
        scratch_shapes=[pltpu.VMEM((tm + FFN_HALO, d), BF16), pltpu.VMEM((tm + FFN_HALO, tf), F32),
                        pltpu.VMEM((tm, d), F32)],
        compiler_params=_cp(("parallel", "arbitrary")),
        name="conv_ffn",
    )(x, x, w_up, w_up, w_down, cw, cb, st, g, b)
    if chained:
        hl = hl[tiles_per_seq - 1::tiles_per_seq, SUBLANES - (CONV_W - 1):, :]
    return o, ob, hl


def _rope_tables(pos0, t):
    half = MLA_ROPE // 2
    inv = ROPE_THETA ** (-jnp.arange(half, dtype=F32) / half)
    ang = (pos0 + jnp.arange(t)).astype(F32)[:, None] * inv
    cos, sin = jnp.cos(ang), jnp.sin(ang)
    cosf = jnp.concatenate([cos, cos], -1)
    sinf = jnp.concatenate([-sin, sin], -1)
    return jnp.tile(cosf, (1, MLA_HEADS)), jnp.tile(sinf, (1, MLA_HEADS))


def _rot_half_cols(w):
    half = w.shape[-1] // 2
    return jnp.concatenate([w[..., half:], w[..., :half]], -1)


def _prep_layer(p, d_model):
    hk = GLA_HEADS * GLA_DK
    sizes = (hk, hk, GLA_HEADS * GLA_DV, MIX_W, GLA_GATE_RANK, MIX_W, MIX_W, MLA_Q_RANK, MLA_KV_RANK, MLA_ROPE)
    offs = [0]
    for s in sizes:
        offs.append(offs[-1] + s)
    w_in = p['w_in']
    seg = lambda i: w_in[:, offs[i]:offs[i + 1]]
    gq, gk, gv, gog, ga, su, pu, mq, mkv, mkr = [seg(i) for i in range(10)]
    pad = jnp.zeros((d_model, Z_COLS - Z_GA - GLA_GATE_RANK), w_in.dtype)
    w_z = jnp.concatenate([gq, gk, gv, gog, su, pu, mq, mkv, mkr, _rot_half_cols(mkr), ga, pad], 1).astype(BF16)
    w_gate = w_in[:, offs[10]:].reshape(d_model, N_BRANCH, d_model).transpose(1, 0, 2).astype(BF16)
    out = dict(w_z=w_z, w_gate=w_gate, b_gate=p['b_gates'].reshape(N_BRANCH, 1, d_model))
    out['gla_wg'] = jnp.zeros((128, hk), F32).at[:GLA_GATE_RANK].set(p['w_gla_gate'])
    out['gla_bg'] = p['b_gla_gate'].reshape(1, hk)
    out['gla_ng'] = p['gla_norm_g'].reshape(1, MIX_W)

    def blockdiag_in(bm):
        g2 = S5_GROUPS // 2
        bm = bm.reshape(2, g2, S5_STATE, S5_GROUP)
        eye = jnp.eye(g2, dtype=bm.dtype)
        w = jnp.einsum('hgnj,gk->hgjkn', bm, eye)
        return w.reshape(2, g2 * S5_GROUP, g2 * S5_STATE).astype(BF16)

    def blockdiag_out(cm):
        g2 = S5_GROUPS // 2
        cm = cm.reshape(2, g2, S5_GROUP, S5_STATE)
        eye = jnp.eye(g2, dtype=cm.dtype)
        w = jnp.einsum('hgjn,gk->hgnkj', cm, eye)
        return w.reshape(2, g2 * S5_STATE, g2 * S5_GROUP).astype(BF16)

    out['s5_wbr'], out['s5_wbi'] = blockdiag_in(p['s5_b_re']), blockdiag_in(p['s5_b_im'])
    out['s5_wcr'], out['s5_wci'] = blockdiag_out(p['s5_c_re']), blockdiag_out(p['s5_c_im'])
    out['s5_d'] = p['s5_d'].reshape(1, MIX_W)
    out['s5_wglu'] = p['w_s5_glu'].astype(BF16)
    out['s5_bglu'] = p['b_s5_glu'].reshape(1, MIX_W)
    out['pool_w'] = p['w_pool'].astype(BF16)
    out['pool_sc'] = p['pool_scale'].reshape(1, MIX_W)
    wq = p['w_q_up'].reshape(MLA_Q_RANK, MLA_HEADS, MLA_NOPE + MLA_ROPE)
    wq_nope = wq[..., :MLA_NOPE].reshape(MLA_Q_RANK, -1)
    wq_rope = wq[..., MLA_NOPE:]
    out['mla_wq'] = jnp.concatenate(
        [wq_nope, wq_rope.reshape(MLA_Q_RANK, -1), _rot_half_cols(wq_rope).reshape(MLA_Q_RANK, -1)], 1).astype(BF16)
    wkv = p['w_kv_up'].reshape(MLA_KV_RANK, MLA_HEADS, MLA_NOPE + MLA_V)
    out['mla_wuk'] = wkv[..., :MLA_NOPE].transpose(1, 2, 0).astype(BF16)
    out['mla_wuv'] = wkv[..., MLA_NOPE:].transpose(1, 0, 2).astype(BF16)
    out['mla_qg'] = p['mla_q_norm'].reshape(1, MLA_Q_RANK)
    out['mla_kg'] = p['mla_kv_norm'].reshape(1, MLA_KV_RANK)
    out['w_branch'] = p['w_branch'].astype(BF16)
    out['w_out'] = p['w_out'].astype(BF16)
    out['w_up'] = p['w_ffn_up'].astype(BF16)
    out['w_down'] = p['w_ffn_down'].astype(BF16)
    out['conv_w'] = p['ffn_conv_w']
    out['conv_b'] = p['ffn_conv_b'].reshape(1, -1)
    for nm in ('ln1_g', 'ln1_b', 'ln2_g', 'ln2_b'):
        out[nm] = p[nm].reshape(1, d_model)
    return out


def _layer(x, xb, lw, tab_l, st, nb, t, pos0, layer, cache, alpha, chained):
    n, d = x.shape
    z = matmul(xb, lw['w_z'], _tile(n, 1024), 256)
    if chained:
        ya, s_gla = gla(z, lw['gla_wg'], lw['gla_bg'], lw['gla_ng'], st['gla'], nb, t, _tile(t, 64))
        s5_rows = _tile(t, 256)
        seq_rows = _tile(t, 256)
    else:
        ya, s_gla = gla(z, lw['gla_wg'], lw['gla_bg'], lw['gla_ng'], st['gla'], nb, t, t)
        s5_rows = _tile(n, 128)
        seq_rows = _tile(n, 128)
    yb, h_re, h_im = s5(z, tab_l, lw['s5_wbr'], lw['s5_wbi'], lw['s5_wcr'], lw['s5_wci'], lw['s5_d'],
                        lw['s5_wglu'], lw['s5_bglu'], st['s5_re'].reshape(nb, S5_W), st['s5_im'].reshape(nb, S5_W),
                        nb, t, chained, s5_rows)
    yc, pbuf = pool(z, lw['pool_w'], lw['pool_sc'], st['pool'], nb, t, pos0, chained, seq_rows)
    cos4, sin4 = _rope_tables(pos0, t)
    if chained:
        qcat, kcat, c_new, kr_new = mla_prep(z, cos4, sin4, lw['mla_qg'], lw['mla_kg'], lw['mla_wq'],
                                             lw['mla_wuk'], _tile(t, 256), BF16)
        yd = mla_causal(qcat, kcat, lw['mla_wuv'], nb, t, _tile(t, 256))
    else:
        qcat, kcat, c_new, kr_new = mla_prep(z, cos4, sin4, lw['mla_qg'], lw['mla_kg'], lw['mla_wq'],
                                             lw['mla_wuk'], t, F32)
        yd = mla_paged(qcat, kcat, cache[0], cache[1], cache[2], lw['mla_wuv'], layer)
    merged = merge(xb, (ya, yb, yc, yd), lw['w_gate'], lw['b_gate'], lw['w_branch'], _tile(n, 512), _tile(d, 512))
    x1, _ = out_ln(merged, lw['w_out'], x, lw['ln1_g'], lw['ln1_b'], alpha, _tile(n, 256))
    ff = lw['w_down'].shape[0]
    tm = _tile(t, 512) if chained else _tile(n, 512)
    x2, x2b, cbuf = conv_ffn(x1, lw['w_up'], lw['w_down'], lw['conv_w'], lw['conv_b'], st['conv'],
                             lw['ln2_g'], lw['ln2_b'], alpha, nb, t, chained, tm, _tile(ff, 512))
    new = (c_new.reshape(nb, t, MLA_KV_RANK), kr_new.reshape(nb, t, MLA_ROPE), s_gla, h_re, h_im, pbuf, cbuf)
    return x2, x2b, new


def _trunk(x3, lws, tabs, states, pos0, cache, alpha, chained):
    nb, t, d = x3.shape
    x = x3.reshape(nb * t, d)
    xb = x.astype(BF16)
    outs = [[] for _ in range(7)]
    for l, lw in enumerate(lws):
        st = {k: v[l] for k, v in states.items()}
        x, xb, new = _layer(x, xb, lw, tabs[l], st, nb, t, pos0, l, cache, alpha, chained)
        for lst, val in zip(outs, new):
            lst.append(val)
    return (x.reshape(nb, t, d),) + tuple(jnp.stack(o, 0) for o in outs)


def kernel(x_prompt, x_sample, cache_kv_latent, cache_k_rope, page_table, state_gla, state_s5_re, state_s5_im,
           state_pool, state_ffn_conv, ln1_g, ln1_b, w_in, b_gates, w_gla_gate, b_gla_gate, gla_norm_g,
           s5_a_re, s5_a_im, s5_log_dt, s5_b_re, s5_b_im, s5_c_re, s5_c_im, s5_d, w_s5_glu, b_s5_glu,
           w_pool, pool_scale, mla_q_norm, w_q_up, mla_kv_norm, w_kv_up, w_branch, w_out, ln2_g, ln2_b,
           w_ffn_up, ffn_conv_w, ffn_conv_b, w_ffn_down):
    params = {'ln1_g': ln1_g, 'ln1_b': ln1_b, 'w_in': w_in, 'b_gates': b_gates, 'w_gla_gate': w_gla_gate,
              'b_gla_gate': b_gla_gate, 'gla_norm_g': gla_norm_g, 's5_b_re': s5_b_re, 's5_b_im': s5_b_im,
              's5_c_re': s5_c_re, 's5_c_im': s5_c_im, 's5_d': s5_d, 'w_s5_glu': w_s5_glu, 'b_s5_glu': b_s5_glu,
              'w_pool': w_pool, 'pool_scale': pool_scale, 'mla_q_norm': mla_q_norm, 'w_q_up': w_q_up,
              'mla_kv_norm': mla_kv_norm, 'w_kv_up': w_kv_up, 'w_branch': w_branch, 'w_out': w_out,
              'ln2_g': ln2_g, 'ln2_b': ln2_b, 'w_ffn_up': w_ffn_up, 'ffn_conv_w': ffn_conv_w,
              'ffn_conv_b': ffn_conv_b, 'w_ffn_down': w_ffn_down}
    depth, d_model = w_in.shape[0], w_in.shape[1]
    d_ff = ffn_conv_w.shape[-1]
    alpha = (2 * depth) ** 0.25
    lws = [_prep_layer({k: v[l] for k, v in params.items()}, d_model) for l in range(depth)]
    tabs = s5_tables(s5_a_re, s5_a_im, s5_log_dt)

    nb_p, t_p, _ = x_prompt.shape
    zeros = lambda *shape: jnp.zeros((depth, nb_p) + shape, F32)
    st_p = dict(gla=zeros(GLA_HEADS, GLA_DK, GLA_DV), s5_re=zeros(S5_GROUPS, S5_STATE),
                s5_im=zeros(S5_GROUPS, S5_STATE), pool=zeros(POOL_BUF, MIX_W), conv=zeros(CONV_W - 1, d_ff))
    res_p = _trunk(x_prompt, lws, tabs, st_p, 0, None, alpha, True)

    past_len = page_table.shape[1] * PAGE_SIZE
    st_s = dict(gla=state_gla, s5_re=state_s5_re, s5_im=state_s5_im, pool=state_pool, conv=state_ffn_conv)
    res_s = _trunk(x_sample, lws, tabs, st_s, past_len, (cache_kv_latent, cache_k_rope, page_table), alpha, False)
    return (res_p[0], res_s[0]) + res_p[1:] + res_s[1:]
```

```python
import functools
import math

import jax
import jax.numpy as jnp
from jax import lax
from jax.experimental import pallas as pl
from jax.experimental.pallas import tpu as pltpu

F32 = jnp.float32
BF16 = jnp.bfloat16
HIGHEST = lax.Precision.HIGHEST

MIX_W = 512
N_BRANCH = 4
GLA_HEADS, GLA_DK, GLA_DV, GLA_GATE_RANK, GLA_TAU = 4, 64, 128, 16, 16.0
S5_GROUP, S5_GROUPS, S5_STATE = 16, 32, 64
S5_W = S5_GROUPS * S5_STATE
POOL_WINDOWS = (2, 4, 8, 16)
POOL_GROUP = 128
POOL_BUF = 15
MLA_HEADS, MLA_NOPE, MLA_ROPE, MLA_V, MLA_Q_RANK, MLA_KV_RANK = 4, 128, 64, 128, 384, 128
MLA_SCALE = (MLA_NOPE + MLA_ROPE) ** -0.5
MLA_QK = 256
ROPE_THETA = 10000.0
PAGE_SIZE = 128
CONV_W = 3
LN_EPS = 1e-5
RMS_EPS = 1e-6

Z_Q, Z_K, Z_V, Z_OG, Z_SU, Z_PU, Z_MQ, Z_MKV, Z_MKR, Z_MKR_ROT, Z_GA = (
    0, 256, 512, 1024, 1536, 2048, 2560, 2944, 3072, 3136, 3200)
Z_COLS = 3328

V7X_VMEM_BYTES = 64 * 1024 * 1024
VMEM_LIMIT = 48 * 1024 * 1024
SUBLANES = 8


def _cp(sem, vmem=VMEM_LIMIT):
    return pltpu.CompilerParams(dimension_semantics=sem, vmem_limit_bytes=vmem)


def _tile(dim, pref):
    t = min(dim, pref)
    while dim % t:
        t //= 2
    return t


def _sigmoid(x):
    return 1.0 / (1.0 + jnp.exp(-x))


def _gelu_tanh(x):
    c = math.sqrt(2.0 / math.pi)
    return 0.5 * x * (1.0 + jnp.tanh(c * (x + 0.044715 * (x * x * x))))


def _log_sigmoid(x):
    return jnp.minimum(x, 0.0) - jnp.log(1.0 + jnp.exp(-jnp.abs(x)))


def _layer_norm(y, g, b):
    mu = jnp.mean(y, -1, keepdims=True)
    d = y - mu
    var = jnp.mean(d * d, -1, keepdims=True)
    return d * lax.rsqrt(var + LN_EPS) * g + b


def _rms(x, g):
    return x * lax.rsqrt(jnp.mean(x * x, -1, keepdims=True) + RMS_EPS) * g


def _dot(a, b):
    return jnp.dot(a, b, preferred_element_type=F32)


def _dot_t(a, b):
    return lax.dot_general(a, b, (((1,), (1,)), ((), ())), preferred_element_type=F32)


def _dot_ta(a, b, **kw):
    return lax.dot_general(a, b, (((0,), (0,)), ((), ())), preferred_element_type=F32, **kw)


def _mm_kernel(x_ref, w_ref, o_ref):
    o_ref[...] = _dot(x_ref[...], w_ref[...]).astype(o_ref.dtype)


def matmul(x, w, tm, tn, out_dtype=F32):
    m, k = x.shape
    n = w.shape[1]
    return pl.pallas_call(
        _mm_kernel,
        grid=(m // tm, n // tn),
        in_specs=[pl.BlockSpec((tm, k), lambda i, j: (i, 0)),
                  pl.BlockSpec((k, tn), lambda i, j: (0, j))],
        out_specs=pl.BlockSpec((tm, tn), lambda i, j: (i, j)),
        out_shape=jax.ShapeDtypeStruct((m, n), out_dtype),
        compiler_params=_cp(("parallel", "parallel")),
        name="in_proj",
    )(x, w)


def _gla_kernel(q_ref, k_ref, v_ref, og_ref, ga_ref, wg_ref, bg_ref, ng_ref, s0_ref,
                ya_ref, st_ref, *, L, nseq):
    R = nseq * L

    @pl.when(pl.program_id(1) == 0)
    def _():
        st_ref[...] = s0_ref[...]

    pre = jnp.dot(ga_ref[...], wg_ref[...], precision=HIGHEST, preferred_element_type=F32) + bg_ref[...]
    log_a = _log_sigmoid(pre) * (1.0 / GLA_TAU)
    row = lax.broadcasted_iota(jnp.int32, (R, R), 0)
    col = lax.broadcasted_iota(jnp.int32, (R, R), 1)
    tri = jnp.where(row >= col, 1.0, 0.0)
    if nseq > 1:
        seq_of = lambda x: sum((x >= j * L).astype(jnp.int32) for j in range(1, nseq))
        same = jnp.where(seq_of(row) == seq_of(col), 1.0, 0.0)
        tri = tri * same
        tot = jnp.dot(same, log_a, precision=HIGHEST, preferred_element_type=F32)
    causal = tri > 0.5
    bc = jnp.dot(tri, log_a, precision=HIGHEST, preferred_element_type=F32)
    if nseq == 1:
        tot = bc[L - 1:L, :]
    q = q_ref[...] * (GLA_DK ** -0.5)
    k = k_ref[...]
    q_in = q * jnp.exp(bc)
    k_in = k * jnp.exp(-bc)
    k_out = k * jnp.exp(tot - bc)
    ones = jnp.ones((L, GLA_DV), F32)
    for h in range(GLA_HEADS):
        ks = slice(h * GLA_DK, (h + 1) * GLA_DK)
        vs = slice(h * GLA_DV, (h + 1) * GLA_DV)
        v = v_ref[:, vs]
        vh = v.astype(BF16)
        scores = jnp.where(causal, _dot_t(q_in[:, ks].astype(BF16), k_in[:, ks].astype(BF16)), 0.0)
        o = _dot(scores.astype(BF16), vh)
        o_inter = []
        for j in range(nseq):
            rs = slice(j * L, (j + 1) * L)
            s_old = st_ref[j, h]
            o_inter.append(_dot(q_in[rs, ks].astype(BF16), s_old.astype(BF16)))
            a_tot = jnp.exp(_dot_ta(log_a[rs, ks], ones, precision=HIGHEST))
            st_ref[j, h] = a_tot * s_old + _dot_ta(k_out[rs, ks].astype(BF16), v[rs].astype(BF16))
        o = o + (o_inter[0] if nseq == 1 else jnp.concatenate(o_inter, 0))
        o = o * lax.rsqrt(jnp.mean(o * o, -1, keepdims=True) + RMS_EPS) * ng_ref[:, vs]
        g = og_ref[:, vs]
        ya_ref[:, vs] = (o * (g * _sigmoid(g))).astype(ya_ref.dtype)


def gla(z, wg_pad, bg, ng, s0, nb, t, L, nseq):
    nc = t // L
    assert nseq == 1 or nc == 1
    R = nseq * L
    rb = lambda b, c: b * nc + c
    hk = GLA_HEADS * GLA_DK
    st_spec = pl.BlockSpec((nseq, GLA_HEADS, GLA_DK, GLA_DV), lambda b, c: (b, 0, 0, 0))
    return pl.pallas_call(
        functools.partial(_gla_kernel, L=L, nseq=nseq),
        grid=(nb // nseq, nc),
        in_specs=[
            pl.BlockSpec((R, hk), lambda b, c: (rb(b, c), Z_Q // hk)),
            pl.BlockSpec((R, hk), lambda b, c: (rb(b, c), Z_K // hk)),
            pl.BlockSpec((R, MIX_W), lambda b, c: (rb(b, c), Z_V // MIX_W)),
            pl.BlockSpec((R, MIX_W), lambda b, c: (rb(b, c), Z_OG // MIX_W)),
            pl.BlockSpec((R, 128), lambda b, c: (rb(b, c), Z_GA // 128)),
            pl.BlockSpec((128, hk), lambda b, c: (0, 0)),
            pl.BlockSpec((1, hk), lambda b, c: (0, 0)),
            pl.BlockSpec((1, MIX_W), lambda b, c: (0, 0)),
            st_spec,
        ],
        out_specs=[pl.BlockSpec((R, MIX_W), lambda b, c: (rb(b, c), 0)), st_spec],
        out_shape=[jax.ShapeDtypeStruct((nb * t, MIX_W), BF16),
                   jax.ShapeDtypeStruct((nb, GLA_HEADS, GLA_DK, GLA_DV), F32)],
        compiler_params=_cp(("parallel", "arbitrary")),
        name="gla",
    )(z, z, z, z, z, wg_pad, bg, ng, s0)


TAB_P8, TAB_T1, TAB_T2, TAB_T4, TAB_F = 0, 2, 4, 6, 8
N_TAB = 10


def _s5_tab_kernel(are_ref, aim_ref, ldt_ref, tab_ref):
    lr = are_ref[0]
    li = aim_ref[0]
    dt = jnp.exp(ldt_ref[0])
    mag = jnp.exp(lr * dt)
    a_re = mag * jnp.cos(li * dt)
    a_im = mag * jnp.sin(li * dt)
    shp = (SUBLANES, S5_W)
    r = lax.broadcasted_iota(jnp.int32, shp, 0)
    pows = [(a_re, a_im)]
    for _ in range(SUBLANES - 1):
        pr, pi = pows[-1]
        pows.append((pr * a_re - pi * a_im, pr * a_im + pi * a_re))
    p8r = jnp.zeros(shp, F32)
    p8i = jnp.zeros(shp, F32)
    for j, (pr, pi) in enumerate(pows):
        p8r = jnp.where(r == j, pr, p8r)
        p8i = jnp.where(r == j, pi, p8i)
    tab_ref[0, TAB_P8] = p8r
    tab_ref[0, TAB_P8 + 1] = p8i
    for s, idx in ((1, TAB_T1), (2, TAB_T2), (4, TAB_T4)):
        pr, pi = pows[s - 1]
        tab_ref[0, idx] = jnp.where(r >= s, pr, 0.0)
        tab_ref[0, idx + 1] = jnp.where(r >= s, pi, 0.0)
    den = lr * lr + li * li
    nr, ni = a_re - 1.0, a_im
    tab_ref[0, TAB_F] = jnp.broadcast_to((nr * lr + ni * li) / den, shp)
    tab_ref[0, TAB_F + 1] = jnp.broadcast_to((ni * lr - nr * li) / den, shp)


def s5_tables(a_re, a_im, log_dt):
    depth = a_re.shape[0]
    flat = lambda a: a.reshape(depth, 1, S5_W)
    ldt = jnp.broadcast_to(log_dt[:, :, None], (depth, S5_GROUPS, S5_STATE))
    spec = pl.BlockSpec((1, 1, S5_W), lambda l: (l, 0, 0))
    return pl.pallas_call(
        _s5_tab_kernel,
        grid=(depth,),
        in_specs=[spec, spec, spec],
        out_specs=pl.BlockSpec((1, N_TAB, SUBLANES, S5_W), lambda l: (l, 0, 0, 0)),
        out_shape=jax.ShapeDtypeStruct((depth, N_TAB, SUBLANES, S5_W), F32),
        compiler_params=_cp(("parallel",)),
        name="s5_tables",
    )(flat(a_re), flat(a_im), flat(ldt))


def _s5_block_scan(xr, xi, tab_ref, cr, ci):
    for s, idx in ((1, TAB_T1), (2, TAB_T2), (4, TAB_T4)):
        tr = tab_ref[idx]
        ti = tab_ref[idx + 1]
        sr = pltpu.roll(xr, s, 0)
        si = pltpu.roll(xi, s, 0)
        xr, xi = xr + (tr * sr - ti * si), xi + (tr * si + ti * sr)
    pr = tab_ref[TAB_P8]
    pi = tab_ref[TAB_P8 + 1]
    hr = xr + (pr * cr - pi * ci)
    hi = xi + (pr * ci + pi * cr)
    return hr, hi


def _s5_kernel(u_ref, tab_ref, wbr_ref, wbi_ref, wcr_ref, wci_ref, d_ref, wglu_ref, bglu_ref,
               h0r_ref, h0i_ref, yb_ref, hr_out_ref, hi_out_ref, xr_ref, xi_ref, cr_ref, ci_ref,
               *, R, chained):
    half_u = MIX_W // 2
    half_s = S5_W // 2
    u = u_ref[...]
    ub = u.astype(BF16)
    fr = tab_ref[TAB_F][0:1]
    fi = tab_ref[TAB_F + 1][0:1]
    for hf in range(2):
        us = ub[:, hf * half_u:(hf + 1) * half_u]
        ss = slice(hf * half_s, (hf + 1) * half_s)
        bur = _dot(us, wbr_ref[hf])
        bui = _dot(us, wbi_ref[hf])
        xr_ref[:, ss] = fr[:, ss] * bur - fi[:, ss] * bui
        xi_ref[:, ss] = fr[:, ss] * bui + fi[:, ss] * bur

    nblk = R // SUBLANES
    if chained:
        @pl.when(pl.program_id(1) == 0)
        def _():
            cr_ref[...] = h0r_ref[0]
            ci_ref[...] = h0i_ref[0]

        def body(j, carry):
            cr, ci = carry
            rows = pl.ds(pl.multiple_of(j * SUBLANES, SUBLANES), SUBLANES)
            hr, hi = _s5_block_scan(xr_ref[rows, :], xi_ref[rows, :], tab_ref, cr, ci)
            xr_ref[rows, :] = hr
            xi_ref[rows, :] = hi
            return hr[SUBLANES - 1:SUBLANES], hi[SUBLANES - 1:SUBLANES]

        cr, ci = lax.fori_loop(0, nblk, body, (cr_ref[...], ci_ref[...]))
        cr_ref[...] = cr
        ci_ref[...] = ci

        @pl.when(pl.program_id(1) == pl.num_programs(1) - 1)
        def _():
            hr_out_ref[0] = cr
            hi_out_ref[0] = ci
    else:
        for j in range(nblk):
            rows = slice(j * SUBLANES, (j + 1) * SUBLANES)
            hr, hi = _s5_block_scan(xr_ref[rows, :], xi_ref[rows, :], tab_ref,
                                    h0r_ref[j:j + 1, :], h0i_ref[j:j + 1, :])
            xr_ref[rows, :] = hr
            xi_ref[rows, :] = hi
            hr_out_ref[j:j + 1, :] = hr[SUBLANES - 1:SUBLANES]
            hi_out_ref[j:j + 1, :] = hi[SUBLANES - 1:SUBLANES]

    for hf in range(2):
        ss = slice(hf * half_s, (hf + 1) * half_s)
        y = _dot(xr_ref[:, ss].astype(BF16), wcr_ref[hf]) - _dot(xi_ref[:, ss].astype(BF16), wci_ref[hf])
        us = slice(hf * half_u, (hf + 1) * half_u)
        xr_ref[:, us] = _gelu_tanh(y + d_ref[:, us] * u[:, us])
    y = xr_ref[:, 0:MIX_W]
    gate = _sigmoid(_dot(y.astype(BF16), wglu_ref[...]) + bglu_ref[...])
    yb_ref[...] = (y * gate).astype(yb_ref.dtype)


def s5(z, tab_l, wbr, wbi, wcr, wci, d, wglu, bglu, h0r, h0i, nb, t, chained, R):
    if chained:
        nc = t // R
        grid = (nb, nc)
        rb = lambda b, c: b * nc + c
        h0 = (h0r.reshape(nb, 1, S5_W), h0i.reshape(nb, 1, S5_W))
        st_spec = pl.BlockSpec((1, 1, S5_W), lambda b, c: (b, 0, 0))
        st_shape = jax.ShapeDtypeStruct((nb, 1, S5_W), F32)
    else:
        assert t == SUBLANES
        nseq = R // SUBLANES
        grid = (nb // nseq, 1)
        rb = lambda b, c: b
        h0 = (h0r, h0i)
        st_spec = pl.BlockSpec((nseq, S5_W), lambda b, c: (b, 0))
        st_shape = jax.ShapeDtypeStruct((nb, S5_W), F32)
    const = lambda shape: pl.BlockSpec(shape, lambda b, c: (0,) * len(shape))
    yb, hr, hi = pl.pallas_call(
        functools.partial(_s5_kernel, R=R, chained=chained),
        grid=grid,
        in_specs=[
            pl.BlockSpec((R, MIX_W), lambda b, c: (rb(b, c), Z_SU // MIX_W)),
            const((N_TAB, SUBLANES, S5_W)),
            const((2, MIX_W // 2, S5_W // 2)), const((2, MIX_W // 2, S5_W // 2)),
            const((2, S5_W // 2, MIX_W // 2)), const((2, S5_W // 2, MIX_W // 2)),
            const((1, MIX_W)), const((MIX_W, MIX_W)), const((1, MIX_W)),
            st_spec, st_spec,
        ],
        out_specs=[pl.BlockSpec((R, MIX_W), lambda b, c: (rb(b, c), 0)), st_spec, st_spec],
        out_shape=[jax.ShapeDtypeStruct((nb * t, MIX_W), BF16), st_shape, st_shape],
        scratch_shapes=[pltpu.VMEM((R, S5_W), F32), pltpu.VMEM((R, S5_W), F32),
                        pltpu.VMEM((1, S5_W), F32), pltpu.VMEM((1, S5_W), F32)],
        compiler_params=_cp(("parallel", "arbitrary")),
        name="s5",
    )(z, tab_l, wbr, wbi, wcr, wci, d, wglu, bglu, *h0)
    return yb, hr.reshape(nb, S5_GROUPS, S5_STATE), hi.reshape(nb, S5_GROUPS, S5_STATE)


HALO = 16


def _pool_windows(ext_ref, base, n, pos, wp_ref, sc_ref, out_ref, out_rows):
    for g, w in enumerate(POOL_WINDOWS):
        cs = slice(g * POOL_GROUP, (g + 1) * POOL_GROUP)
        u = ext_ref[pl.ds(base, n), cs]
        s = u
        for j in range(1, w):
            s = s + ext_ref[pl.ds(base - j, n), cs]
        cnt = jnp.minimum(pos + 1, w).astype(F32)
        dlt = s / cnt - u
        y = _dot(dlt.astype(BF16), wp_ref[g]) * sc_ref[:, cs]
        out_ref[out_rows, cs] = y.astype(out_ref.dtype)


def _pool_chain_kernel(u_ref, wp_ref, sc_ref, pre_ref, yc_ref, buf_ref, ext_ref, *, L, pos0):
    c = pl.program_id(1)

    @pl.when(c == 0)
    def _():
        ext_ref[0:HALO, :] = pre_ref[0]

    ext_ref[HALO:HALO + L, :] = u_ref[...]
    pos = pos0 + c * L + lax.broadcasted_iota(jnp.int32, (L, 1), 0)
    _pool_windows(ext_ref, HALO, L, pos, wp_ref, sc_ref, yc_ref, slice(None))
    tail = ext_ref[L:L + HALO, :]
    ext_ref[0:HALO, :] = tail

    @pl.when(c == pl.num_programs(1) - 1)
    def _():
        buf_ref[0] = tail


def _pool_seq_kernel(u_ref, wp_ref, sc_ref, pre_ref, yc_ref, buf_ref, ext_ref, *, nseq, pos0):
    t = SUBLANES
    pos = pos0 + lax.broadcasted_iota(jnp.int32, (t, 1), 0)
    for j in range(nseq):
        ext_ref[0:HALO, :] = pre_ref[j]
        ext_ref[HALO:HALO + t, :] = u_ref[j * t:(j + 1) * t, :]
        _pool_windows(ext_ref, HALO, t, pos, wp_ref, sc_ref, yc_ref, slice(j * t, (j + 1) * t))
        buf_ref[j] = ext_ref[t:t + HALO, :]


def pool(z, wp, sc, prefix, nb, t, pos0, chained, L):
    pre = jnp.pad(prefix, ((0, 0), (1, 0), (0, 0)))
    const = lambda shape: pl.BlockSpec(shape, lambda b, c: (0,) * len(shape))
    if chained:
        nc = t // L
        grid = (nb, nc)
        kern = functools.partial(_pool_chain_kernel, L=L, pos0=pos0)
        u_spec = pl.BlockSpec((L, MIX_W), lambda b, c: (b * nc + c, Z_PU // MIX_W))
        y_spec = pl.BlockSpec((L, MIX_W), lambda b, c: (b * nc + c, 0))
        st_spec = pl.BlockSpec((1, HALO, MIX_W), lambda b, c: (b, 0, 0))
        ext_rows = HALO + L
    else:
        assert t == SUBLANES
        nseq = L // t
        grid = (nb // nseq, 1)
        kern = functools.partial(_pool_seq_kernel, nseq=nseq, pos0=pos0)
        u_spec = pl.BlockSpec((L, MIX_W), lambda b, c: (b, Z_PU // MIX_W))
        y_spec = pl.BlockSpec((L, MIX_W), lambda b, c: (b, 0))
        st_spec = pl.BlockSpec((nseq, HALO, MIX_W), lambda b, c: (b, 0, 0))
        ext_rows = HALO + t
    yc, buf = pl.pallas_call(
        kern,
        grid=grid,
        in_specs=[u_spec, const((len(POOL_WINDOWS), POOL_GROUP, POOL_GROUP)), const((1, MIX_W)), st_spec],
        out_specs=[y_spec, st_spec],
        out_shape=[jax.ShapeDtypeStruct((nb * t, MIX_W), BF16),
                   jax.ShapeDtypeStruct((nb, HALO, MIX_W), F32)],
        scratch_shapes=[pltpu.VMEM((ext_rows, MIX_W), F32)],
        compiler_params=_cp(("parallel", "arbitrary")),
        name="pool",
    )(z, wp, sc, pre)
    return yc, buf[:, 1:, :]


def _mla_prep_kernel(zq_ref, zr_ref, cq_ref, sq_ref, qg_ref, kg_ref, wq_ref, wuk_ref,
                     qcat_ref, kcat_ref, c_ref, kr_ref):
    nq = MLA_HEADS * MLA_NOPE
    nr = MLA_HEADS * MLA_ROPE
    mq = zq_ref[:, 0:MLA_Q_RANK]
    mkv = zq_ref[:, MLA_Q_RANK:MLA_Q_RANK + MLA_KV_RANK]
    qd = _dot(_rms(mq, qg_ref[...]).astype(BF16), wq_ref[...])
    cos4 = cq_ref[...]
    sin4 = sq_ref[...]
    q_rope = (qd[:, nq:nq + nr] * cos4 + qd[:, nq + nr:nq + 2 * nr] * sin4) * MLA_SCALE
    zpad = jnp.zeros((qd.shape[0], MLA_QK - MLA_KV_RANK - MLA_ROPE), qcat_ref.dtype)
    for h in range(MLA_HEADS):
        q_lat = _dot(qd[:, h * MLA_NOPE:(h + 1) * MLA_NOPE].astype(BF16), wuk_ref[h]) * MLA_SCALE
        qcat_ref[h, :, 0:MLA_KV_RANK] = q_lat.astype(qcat_ref.dtype)
        qcat_ref[h, :, MLA_KV_RANK:MLA_KV_RANK + MLA_ROPE] = (
            q_rope[:, h * MLA_ROPE:(h + 1) * MLA_ROPE].astype(qcat_ref.dtype))
        qcat_ref[h, :, MLA_KV_RANK + MLA_ROPE:MLA_QK] = zpad
    c_new = _rms(mkv, kg_ref[...])
    kr_new = (zr_ref[:, 0:MLA_ROPE] * cos4[:, 0:MLA_ROPE]
              + zr_ref[:, MLA_ROPE:2 * MLA_ROPE] * sin4[:, 0:MLA_ROPE])
    c_ref[...] = c_new
    kr_ref[...] = kr_new
    kcat_ref[:, 0:MLA_KV_RANK] = c_new.astype(kcat_ref.dtype)
    kcat_ref[:, MLA_KV_RANK:MLA_KV_RANK + MLA_ROPE] = kr_new.astype(kcat_ref.dtype)
    kcat_ref[:, MLA_KV_RANK + MLA_ROPE:MLA_QK] = zpad


def mla_prep(z, cos4, sin4, qg, kg, wq, wuk, L, cat_dtype):
    n = z.shape[0]
    nt = cos4.shape[0] // L
    const = lambda shape: pl.BlockSpec(shape, lambda i: (0,) * len(shape))
    tab_spec = pl.BlockSpec((L, MLA_HEADS * MLA_ROPE), lambda i: (i % nt, 0))
    return pl.pallas_call(
        _mla_prep_kernel,
        grid=(n // L,),
        in_specs=[
            pl.BlockSpec((L, MIX_W), lambda i: (i, Z_MQ // MIX_W)),
            pl.BlockSpec((L, 128), lambda i: (i, Z_MKR // 128)),
            tab_spec, tab_spec,
            const((1, MLA_Q_RANK)), const((1, MLA_KV_RANK)),
            const((MLA_Q_RANK, MLA_HEADS * (MLA_NOPE + 2 * MLA_ROPE))),
            const((MLA_HEADS, MLA_NOPE, MLA_KV_RANK)),
        ],
        out_specs=[
            pl.BlockSpec((MLA_HEADS, L, MLA_QK), lambda i: (0, i, 0)),
            pl.BlockSpec((L, MLA_QK), lambda i: (i, 0)),
            pl.BlockSpec((L, MLA_KV_RANK), lambda i: (i, 0)),
            pl.BlockSpec((L, MLA_ROPE), lambda i: (i, 0)),
        ],
        out_shape=[
            jax.ShapeDtypeStruct((MLA_HEADS, n, MLA_QK), cat_dtype),
            jax.ShapeDtypeStruct((n, MLA_QK), cat_dtype),
            jax.ShapeDtypeStruct((n, MLA_KV_RANK), F32),
            jax.ShapeDtypeStruct((n, MLA_ROPE), F32),
        ],
        compiler_params=_cp(("parallel",)),
        name="mla_prep",
    )(z, z, cos4, sin4, qg, kg, wq, wuk)


NEG_BIG = -1e30


def _causal_mask(tq, tk):
    row = lax.broadcasted_iota(jnp.int32, (MLA_HEADS * tq, tk), 0)
    key = lax.broadcasted_iota(jnp.int32, (MLA_HEADS * tq, tk), 1)
    head = sum((row >= h * tq).astype(jnp.int32) for h in range(1, MLA_HEADS))
    return key <= row - head * tq


def _mla_causal_kernel(q_ref, k_ref, wuv_ref, yd_ref, m_ref, l_ref, acc_ref, *, tb):
    qi = pl.program_id(1)
    rows = MLA_HEADS * tb
    m_ref[...] = jnp.full(m_ref.shape, NEG_BIG, F32)
    l_ref[...] = jnp.zeros(l_ref.shape, F32)
    acc_ref[...] = jnp.zeros(acc_ref.shape, F32)
    q = q_ref[...].reshape(rows, MLA_QK)

    def block(kj, masked):
        k = k_ref[pl.ds(pl.multiple_of(kj * tb, tb), tb), :]
        s = _dot_t(q, k)
        if masked:
            s = jnp.where(_causal_mask(tb, tb), s, NEG_BIG)
        m_old = m_ref[...]
        m_new = jnp.maximum(m_old, jnp.max(s, -1, keepdims=True))
        alpha = jnp.exp(m_old - m_new)
        p = jnp.exp(s - jnp.tile(m_new, (1, tb // 128)))
        l_ref[...] = alpha * l_ref[...] + jnp.sum(p, -1, keepdims=True)
        acc_ref[...] = alpha * acc_ref[...] + _dot(p.astype(BF16), k[:, 0:MLA_KV_RANK])
        m_ref[...] = m_new

    def body(kj, carry):
        block(kj, False)
        return carry

    lax.fori_loop(0, qi, body, 0)
    block(qi, True)
    o = acc_ref[...] / l_ref[...]
    for h in range(MLA_HEADS):
        oh = o[h * tb:(h + 1) * tb, :].astype(BF16)
        yd_ref[:, h * MLA_V:(h + 1) * MLA_V] = _dot(oh, wuv_ref[h]).astype(yd_ref.dtype)


def mla_causal(qcat, kcat, wuv, nb, t, tb):
    nq = t // tb
    rows = MLA_HEADS * tb
    return pl.pallas_call(
        functools.partial(_mla_causal_kernel, tb=tb),
        grid=(nb, nq),
        in_specs=[
            pl.BlockSpec((MLA_HEADS, tb, MLA_QK), lambda b, i: (0, b * nq + i, 0)),
            pl.BlockSpec((t, MLA_QK), lambda b, i: (b, 0)),
            pl.BlockSpec((MLA_HEADS, MLA_KV_RANK, MLA_V), lambda b, i: (0, 0, 0)),
        ],
        out_specs=pl.BlockSpec((tb, MIX_W), lambda b, i: (b * nq + i, 0)),
        out_shape=jax.ShapeDtypeStruct((nb * t, MIX_W), BF16),
        scratch_shapes=[pltpu.VMEM((rows, 128), F32), pltpu.VMEM((rows, 128), F32),
                        pltpu.VMEM((rows, MLA_KV_RANK), F32)],
        compiler_params=_cp(("parallel", "arbitrary")),
        name="mla_causal",
    )(qcat, kcat, wuv)


PAGED_GROUP = 4


def _mla_paged_kernel(pt_ref, q_ref, kn_ref, cache_c_ref, cache_krt_ref, wuv_ref, yd_ref,
                      cbuf_ref, krbuf_ref, sem_ref, cb16_ref, s_ref, *, layer, n_pages, kchunk):
    g = pl.program_id(0)
    ng = pl.num_programs(0)
    t = SUBLANES
    hr = MLA_HEADS * t
    past = n_pages * PAGE_SIZE
    nchunks = past // kchunk
    ppc = kchunk // PAGE_SIZE

    def page_copies(base, slot, p, page_of):
        out = []
        for ab in range(2):
            page = page_of(base + ab, p)
            out.append(pltpu.make_async_copy(
                cache_c_ref.at[layer, page],
                cbuf_ref.at[slot, pl.ds(p * PAGE_SIZE, PAGE_SIZE), pl.ds(ab * MLA_KV_RANK, MLA_KV_RANK)],
                sem_ref.at[0, slot]))
            out.append(pltpu.make_async_copy(
                cache_krt_ref.at[layer, page],
                krbuf_ref.at[slot, pl.ds(ab * MLA_ROPE, MLA_ROPE), pl.ds(p * PAGE_SIZE, PAGE_SIZE)],
                sem_ref.at[1, slot]))
        return out

    table_page = lambda seq, p: pt_ref[seq * n_pages + p]

    def issue(base, slot, chunk):
        for p in range(chunk * ppc, (chunk + 1) * ppc):
            for cp in page_copies(base, slot, p, table_page):
                cp.start()

    def wait_all(slot):
        for p in range(n_pages):
            for cp in page_copies(0, slot, p, lambda seq, p: 0):
                cp.wait()

    @pl.when(g == 0)
    def _():
        for chunk in range(nchunks):
            issue(0, 0, chunk)

    qf = q_ref[...]
    knf = kn_ref[...]
    mask_new = _causal_mask(t, t)
    next_base = jnp.minimum(PAGED_GROUP * (g + 1), PAGED_GROUP * (ng - 1))

    for half in range(2):
        slot = half
        wait_all(slot)
        fill_base, fill_slot = (PAGED_GROUP * g + 2, 1) if half == 0 else (next_base, 0)
        qs = [qf[:, (2 * half + ab) * t:(2 * half + ab + 1) * t, :].reshape(hr, MLA_QK) for ab in range(2)]
        kns = [knf[(2 * half + ab) * t:(2 * half + ab + 1) * t, :].astype(BF16) for ab in range(2)]
        z_lat = jnp.zeros((hr, MLA_KV_RANK), F32)
        z_rope = jnp.zeros((hr, MLA_ROPE), F32)
        rope = slice(MLA_KV_RANK, MLA_KV_RANK + MLA_ROPE)
        q_lat = jnp.concatenate([jnp.concatenate([qs[0][:, 0:MLA_KV_RANK], z_lat], 1),
                                 jnp.concatenate([z_lat, qs[1][:, 0:MLA_KV_RANK]], 1)], 0).astype(BF16)
        q_rope = jnp.concatenate([jnp.concatenate([qs[0][:, rope], z_rope], 1),
                                  jnp.concatenate([z_rope, qs[1][:, rope]], 1)], 0).astype(BF16)
        for chunk in range(nchunks):
            ks = slice(chunk * kchunk, (chunk + 1) * kchunk)
            cb = cbuf_ref[slot, ks, :].astype(BF16)
            cb16_ref[ks, :] = cb
            krb = krbuf_ref[slot, :, ks].astype(BF16)
            s_ref[:, ks] = _dot_t(q_lat, cb) + _dot(q_rope, krb)
            issue(fill_base, fill_slot, chunk)
        s_new = jnp.concatenate(
            [jnp.where(mask_new, _dot_t(qs[ab].astype(BF16), kns[ab]), NEG_BIG) for ab in range(2)], 0)
        s_past = s_ref[...]
        m = jnp.maximum(jnp.max(s_past, -1, keepdims=True), jnp.max(s_new, -1, keepdims=True))
        p_past = jnp.exp(s_past - m)
        p_new = jnp.exp(s_new - m)
        l = jnp.sum(p_past, -1, keepdims=True) + jnp.sum(p_new, -1, keepdims=True)
        o_pair = _dot(p_past.astype(BF16), cb16_ref[...])
        o = []
        for ab in range(2):
            rs = slice(ab * hr, (ab + 1) * hr)
            o_ab = (o_pair[rs, ab * MLA_KV_RANK:(ab + 1) * MLA_KV_RANK]
                    + _dot(p_new[rs].astype(BF16), kns[ab][:, 0:MLA_KV_RANK]))
            o.append(o_ab / l[rs])
        for h in range(MLA_HEADS):
            oh = jnp.concatenate([o[0][h * t:(h + 1) * t], o[1][h * t:(h + 1) * t]], 0).astype(BF16)
            yd_ref[2 * half * t:(2 * half + 2) * t, h * MLA_V:(h + 1) * MLA_V] = (
                _dot(oh, wuv_ref[h]).astype(yd_ref.dtype))

    @pl.when(g == ng - 1)
    def _():
        wait_all(0)


def mla_paged(qcat, kcat, cache_c, cache_krt, page_table, wuv, layer):
    nb, n_pages = page_table.shape
    assert nb % PAGED_GROUP == 0
    t = SUBLANES
    past = n_pages * PAGE_SIZE
    kchunk = _tile(past, 2048)
    rows = PAGED_GROUP * t
    grid_spec = pltpu.PrefetchScalarGridSpec(
        num_scalar_prefetch=1,
        grid=(nb // PAGED_GROUP,),
        in_specs=[
            pl.BlockSpec((MLA_HEADS, rows, MLA_QK), lambda g, pt: (0, g, 0)),
            pl.BlockSpec((rows, MLA_QK), lambda g, pt: (g, 0)),
            pl.BlockSpec(memory_space=pl.ANY),
            pl.BlockSpec(memory_space=pl.ANY),
            pl.BlockSpec((MLA_HEADS, MLA_KV_RANK, MLA_V), lambda g, pt: (0, 0, 0)),
        ],
        out_specs=pl.BlockSpec((rows, MIX_W), lambda g, pt: (g, 0)),
        scratch_shapes=[
            pltpu.VMEM((2, past, 2 * MLA_KV_RANK), F32),
            pltpu.VMEM((2, 2 * MLA_ROPE, past), F32),
            pltpu.SemaphoreType.DMA((2, 2)),
            pltpu.VMEM((past, 2 * MLA_KV_RANK), BF16),
            pltpu.VMEM((2 * MLA_HEADS * t, past), F32),
        ],
    )
    return pl.pallas_call(
        functools.partial(_mla_paged_kernel, layer=layer, n_pages=n_pages, kchunk=kchunk),
        grid_spec=grid_spec,
        out_shape=jax.ShapeDtypeStruct((nb * t, MIX_W), F32),
        compiler_params=_cp(("arbitrary",)),
        name="mla_paged",
    )(page_table.reshape(-1), qcat, kcat, cache_c, cache_krt, wuv)


def _merge_kernel(x_ref, *refs):
    brs, wgs, bgs = refs[0:N_BRANCH], refs[N_BRANCH:2 * N_BRANCH], refs[2 * N_BRANCH:3 * N_BRANCH]
    wb_ref, o_ref = refs[3 * N_BRANCH:]
    x = x_ref[...]
    acc = None
    for g in range(N_BRANCH):
        gate = _sigmoid(_dot(x, wgs[g][...]) + bgs[g][...])
        term = _dot(brs[g][...].astype(BF16), wb_ref[g]) * gate
        acc = term if acc is None else acc + term
    o_ref[...] = acc.astype(o_ref.dtype)


def merge(xb, ys, wg, bg, wb, tm, tn):
    n, d = xb.shape
    nj = d // tn
    row = lambda shape: pl.BlockSpec(shape, lambda j, i: (i, 0))
    gate_cols = lambda rows: [pl.BlockSpec((rows, tn), lambda j, i, g=g: (0, g * nj + j)) for g in range(N_BRANCH)]
    return pl.pallas_call(
        _merge_kernel,
        grid=(nj, n // tm),
        in_specs=[row((tm, d))] + [row((tm, MIX_W))] * N_BRANCH + gate_cols(d) + gate_cols(1)
        + [pl.BlockSpec((N_BRANCH, MIX_W, tn), lambda j, i: (0, 0, j))],
        out_specs=pl.BlockSpec((tm, tn), lambda j, i: (i, j)),
        out_shape=jax.ShapeDtypeStruct((n, d), BF16),
        compiler_params=_cp(("parallel", "parallel")),
        name="merge",
    )(xb, *ys, *([wg] * N_BRANCH), *([bg] * N_BRANCH), wb)


def _out_ln_kernel(m_ref, w_ref, x_ref, g_ref, b_ref, o_ref, ob_ref, *, alpha):
    y = alpha * x_ref[...] + _dot(m_ref[...], w_ref[...])
    y = _layer_norm(y, g_ref[...], b_ref[...])
    o_ref[...] = y
    ob_ref[...] = y.astype(ob_ref.dtype)


def out_ln(merged, w_out, x, g, b, alpha, tm):
    n, d = x.shape
    row = lambda dt: pl.BlockSpec((tm, d), lambda i: (i, 0))
    const = lambda shape: pl.BlockSpec(shape, lambda i: (0, 0))
    return pl.pallas_call(
        functools.partial(_out_ln_kernel, alpha=alpha),
        grid=(n // tm,),
        in_specs=[row(BF16), const((d, d)), row(F32), const((1, d)), const((1, d))],
        out_specs=[row(F32), row(BF16)],
        out_shape=[jax.ShapeDtypeStruct((n, d), F32), jax.ShapeDtypeStruct((n, d), BF16)],
        compiler_params=_cp(("parallel",)),
        name="out_ln",
    )(merged, w_out, x, g, b)


FFN_HALO = 16


def _ffn_kernel(x_ref, halo_ref, wh_ref, wg_ref, wd_ref, cw_ref, cb_ref, st_ref, g_ref, b_ref,
                o_ref, ob_ref, hl_ref, xcat_ref, hs_ref, acc_ref, *, tm, alpha, chained, tiles_per_seq):
    i = pl.program_id(0)
    f = pl.program_id(1)

    @pl.when(f == 0)
    def _():
        acc_ref[...] = jnp.zeros(acc_ref.shape, F32)
        xcat_ref[FFN_HALO:, :] = x_ref[...].astype(BF16)
        if chained:
            seq_start = (i % tiles_per_seq) == 0
            xcat_ref[0:FFN_HALO, :] = jnp.where(seq_start, 0.0, halo_ref[...]).astype(BF16)

    w0 = cw_ref[0:1, :]
    w1 = cw_ref[1:2, :]
    w2 = cw_ref[2:3, :]
    if chained:
        hs_ref[...] = _dot(xcat_ref[...], wh_ref[...])
        h = hs_ref[FFN_HALO:, :]
        hm1 = hs_ref[pl.ds(FFN_HALO - 1, tm), :]
        hm2 = hs_ref[pl.ds(FFN_HALO - 2, tm), :]
        hl_ref[0] = hs_ref[tm + FFN_HALO - SUBLANES:, :]
        conv = cb_ref[...] + w0 * hm2 + w1 * hm1 + w2 * h
    else:
        t = SUBLANES
        nseq = tm // t
        tf = wh_ref.shape[1]
        h = _dot(xcat_ref[FFN_HALO:, :], wh_ref[...]).reshape(nseq, t, tf)
        r = lax.broadcasted_iota(jnp.int32, (nseq, t, tf), 1)
        b0 = st_ref[:, 0:1, :]
        b1 = st_ref[:, 1:2, :]
        hm1 = jnp.where(r >= 1, pltpu.roll(h, 1, 1), b1)
        hm2 = jnp.where(r >= 2, pltpu.roll(h, 2, 1), jnp.where(r == 1, b1, b0))
        hl_ref[...] = h[:, t - (CONV_W - 1):, :]
        conv = (cb_ref[...] + w0 * hm2 + w1 * hm1 + w2 * h).reshape(tm, tf)
    gate = _dot(xcat_ref[FFN_HALO:, :], wg_ref[...])
    a = (_gelu_tanh(conv) * gate).astype(BF16)
    acc_ref[...] += _dot(a, wd_ref[...])

    @pl.when(f == pl.num_programs(1) - 1)
    def _():
        y = _layer_norm(alpha * x_ref[...] + acc_ref[...], g_ref[...], b_ref[...])
        o_ref[...] = y
        ob_ref[...] = y.astype(ob_ref.dtype)


def conv_ffn(x, w_up, w_down, cw, cb, state, g, b, alpha, nb, t, chained, tm, tf):
    n, d = x.shape
    ff = w_down.shape[0]
    nf = ff // tf
    const = lambda shape: pl.BlockSpec(shape, lambda i, f: (0, 0))
    hb = tm // FFN_HALO
    if chained:
        tiles_per_seq = t // tm
        st = jnp.zeros((1, CONV_W - 1, tf), F32)
        st_spec = pl.BlockSpec((1, CONV_W - 1, tf), lambda i, f: (0, 0, 0))
        hl_shape = jax.ShapeDtypeStruct((n // tm, SUBLANES, ff), F32)
        hl_spec = pl.BlockSpec((1, SUBLANES, tf), lambda i, f: (i, 0, f))
    else:
        assert t == SUBLANES
        tiles_per_seq = 1
        st = state
        st_spec = pl.BlockSpec((tm // t, CONV_W - 1, tf), lambda i, f: (i, 0, f))
        hl_shape = jax.ShapeDtypeStruct((nb, CONV_W - 1, ff), F32)
        hl_spec = pl.BlockSpec((tm // t, CONV_W - 1, tf), lambda i, f: (i, 0, f))
    o, ob, hl = pl.pallas_call(
        functools.partial(_ffn_kernel, tm=tm, alpha=alpha, chained=chained, tiles_per_seq=tiles_per_seq),
        grid=(n // tm, nf),
        in_specs=[
            pl.BlockSpec((tm, d), lambda i, f: (i, 0)),
            pl.BlockSpec((FFN_HALO, d), lambda i, f: (jnp.maximum(i * hb - 1, 0), 0)),
            pl.BlockSpec((d, tf), lambda i, f: (0, f)),
            pl.BlockSpec((d, tf), lambda i, f: (0, nf + f)),
            pl.BlockSpec((tf, d), lambda i, f: (f, 0)),
            pl.BlockSpec((CONV_W, tf), lambda i, f: (0, f)),
            pl.BlockSpec((1, tf), lambda i, f: (0, f)),
            st_spec, const((1, d)), const((1, d)),
        ],
        out_specs=[pl.BlockSpec((tm, d), lambda i, f: (i, 0)), pl.BlockSpec((tm, d), lambda i, f: (i, 0)), hl_spec],
        out_shape=[jax.ShapeDtypeStruct((n, d), F32), jax.ShapeDtypeStruct((n, d), BF16), hl_shape],
        scratch_shapes=[pltpu.VMEM((tm + FFN_HALO, d), BF16), pltpu.VMEM((tm + FFN_HALO, tf), F32),
                        pltpu.VMEM((tm, d), F32)],
        compiler_params=_cp(("parallel", "arbitrary")),
        name="conv_ffn",
    )(x, x, w_up, w_up, w_down, cw, cb, st, g, b)
    if chained:
        hl = hl[tiles_per_seq - 1::tiles_per_seq, SUBLANES - (CONV_W - 1):, :]
    return o, ob, hl


def _rope_tables(pos0, t):
    half = MLA_ROPE // 2
    inv = ROPE_THETA ** (-jnp.arange(half, dtype=F32) / half)
    ang = (pos0 + jnp.arange(t)).astype(F32)[:, None] * inv
    cos, sin = jnp.cos(ang), jnp.sin(ang)
    cosf = jnp.concatenate([cos, cos], -1)
    sinf = jnp.concatenate([-sin, sin], -1)
    return jnp.tile(cosf, (1, MLA_HEADS)), jnp.tile(sinf, (1, MLA_HEADS))


def _rot_half_cols(w):
    half = w.shape[-1] // 2
    return jnp.concatenate([w[..., half:], w[..., :half]], -1)


def _prep_layer(p, d_model):
    hk = GLA_HEADS * GLA_DK
    sizes = (hk, hk, GLA_HEADS * GLA_DV, MIX_W, GLA_GATE_RANK, MIX_W, MIX_W, MLA_Q_RANK, MLA_KV_RANK, MLA_ROPE)
    offs = [0]
    for s in sizes:
        offs.append(offs[-1] + s)
    w_in = p['w_in']
    seg = lambda i: w_in[:, offs[i]:offs[i + 1]]
    gq, gk, gv, gog, ga, su, pu, mq, mkv, mkr = [seg(i) for i in range(10)]
    pad = jnp.zeros((d_model, Z_COLS - Z_GA - GLA_GATE_RANK), w_in.dtype)
    w_z = jnp.concatenate([gq, gk, gv, gog, su, pu, mq, mkv, mkr, _rot_half_cols(mkr), ga, pad], 1).astype(BF16)
    w_gate = w_in[:, offs[10]:].astype(BF16)
    out = dict(w_z=w_z, w_gate=w_gate, b_gate=p['b_gates'].reshape(1, N_BRANCH * d_model))
    out['gla_wg'] = jnp.zeros((128, hk), F32).at[:GLA_GATE_RANK].set(p['w_gla_gate'])
    out['gla_bg'] = p['b_gla_gate'].reshape(1, hk)
    out['gla_ng'] = p['gla_norm_g'].reshape(1, MIX_W)

    def blockdiag_in(bm):
        g2 = S5_GROUPS // 2
        bm = bm.reshape(2, g2, S5_STATE, S5_GROUP)
        eye = jnp.eye(g2, dtype=bm.dtype)
        w = jnp.einsum('hgnj,gk->hgjkn', bm, eye)
        return w.reshape(2, g2 * S5_GROUP, g2 * S5_STATE).astype(BF16)

    def blockdiag_out(cm):
        g2 = S5_GROUPS // 2
        cm = cm.reshape(2, g2, S5_GROUP, S5_STATE)
        eye = jnp.eye(g2, dtype=cm.dtype)
        w = jnp.einsum('hgjn,gk->hgnkj', cm, eye)
        return w.reshape(2, g2 * S5_STATE, g2 * S5_GROUP).astype(BF16)

    out['s5_wbr'], out['s5_wbi'] = blockdiag_in(p['s5_b_re']), blockdiag_in(p['s5_b_im'])
    out['s5_wcr'], out['s5_wci'] = blockdiag_out(p['s5_c_re']), blockdiag_out(p['s5_c_im'])
    out['s5_d'] = p['s5_d'].reshape(1, MIX_W)
    out['s5_wglu'] = p['w_s5_glu'].astype(BF16)
    out['s5_bglu'] = p['b_s5_glu'].reshape(1, MIX_W)
    out['pool_w'] = p['w_pool'].astype(BF16)
    out['pool_sc'] = p['pool_scale'].reshape(1, MIX_W)
    wq = p['w_q_up'].reshape(MLA_Q_RANK, MLA_HEADS, MLA_NOPE + MLA_ROPE)
    wq_nope = wq[..., :MLA_NOPE].reshape(MLA_Q_RANK, -1)
    wq_rope = wq[..., MLA_NOPE:]
    out['mla_wq'] = jnp.concatenate(
        [wq_nope, wq_rope.reshape(MLA_Q_RANK, -1), _rot_half_cols(wq_rope).reshape(MLA_Q_RANK, -1)], 1).astype(BF16)
    wkv = p['w_kv_up'].reshape(MLA_KV_RANK, MLA_HEADS, MLA_NOPE + MLA_V)
    out['mla_wuk'] = wkv[..., :MLA_NOPE].transpose(1, 2, 0).astype(BF16)
    out['mla_wuv'] = wkv[..., MLA_NOPE:].transpose(1, 0, 2).astype(BF16)
    out['mla_qg'] = p['mla_q_norm'].reshape(1, MLA_Q_RANK)
    out['mla_kg'] = p['mla_kv_norm'].reshape(1, MLA_KV_RANK)
    out['w_branch'] = p['w_branch'].astype(BF16)
    out['w_out'] = p['w_out'].astype(BF16)
    out['w_up'] = p['w_ffn_up'].astype(BF16)
    out['w_down'] = p['w_ffn_down'].astype(BF16)
    out['conv_w'] = p['ffn_conv_w']
    out['conv_b'] = p['ffn_conv_b'].reshape(1, -1)
    for nm in ('ln1_g', 'ln1_b', 'ln2_g', 'ln2_b'):
        out[nm] = p[nm].reshape(1, d_model)
    return out


def _layer(x, xb, lw, tab_l, st, nb, t, pos0, layer, cache, alpha, chained):
    n, d = x.shape
    z = matmul(xb, lw['w_z'], _tile(n, 1024), 256)
    if chained:
        ya, s_gla = gla(z, lw['gla_wg'], lw['gla_bg'], lw['gla_ng'], st['gla'], nb, t, _tile(t, 64), 1)
        s5_rows = _tile(t, 256)
        seq_rows = _tile(t, 256)
    else:
        ya, s_gla = gla(z, lw['gla_wg'], lw['gla_bg'], lw['gla_ng'], st['gla'], nb, t, t, _tile(nb, 8))
        s5_rows = _tile(n, 128)
        seq_rows = _tile(n, 128)
    yb, h_re, h_im = s5(z, tab_l, lw['s5_wbr'], lw['s5_wbi'], lw['s5_wcr'], lw['s5_wci'], lw['s5_d'],
                        lw['s5_wglu'], lw['s5_bglu'], st['s5_re'].reshape(nb, S5_W), st['s5_im'].reshape(nb, S5_W),
                        nb, t, chained, s5_rows)
    yc, pbuf = pool(z, lw['pool_w'], lw['pool_sc'], st['pool'], nb, t, pos0, chained, seq_rows)
    cos4, sin4 = _rope_tables(pos0, t)
    if chained:
        qcat, kcat, c_new, kr_new = mla_prep(z, cos4, sin4, lw['mla_qg'], lw['mla_kg'], lw['mla_wq'],
                                             lw['mla_wuk'], _tile(t, 256), BF16)
        yd = mla_causal(qcat, kcat, lw['mla_wuv'], nb, t, _tile(t, 512))
    else:
        prep_rows = _tile(n, 128)
        tile_rows = lambda a: jnp.tile(a, (prep_rows // t, 1))
        qcat, kcat, c_new, kr_new = mla_prep(z, tile_rows(cos4), tile_rows(sin4), lw['mla_qg'], lw['mla_kg'],
                                             lw['mla_wq'], lw['mla_wuk'], prep_rows, F32)
        yd = mla_paged(qcat, kcat, cache[0], cache[1], cache[2], lw['mla_wuv'], layer)
    merged = merge(xb, (ya, yb, yc, yd), lw['w_gate'], lw['b_gate'], lw['w_branch'], _tile(n, 512), _tile(d, 512))
    x1, _ = out_ln(merged, lw['w_out'], x, lw['ln1_g'], lw['ln1_b'], alpha, _tile(n, 256))
    ff = lw['w_down'].shape[0]
    tm = _tile(t, 512) if chained else _tile(n, 512)
    x2, x2b, cbuf = conv_ffn(x1, lw['w_up'], lw['w_down'], lw['conv_w'], lw['conv_b'], st['conv'],
                             lw['ln2_g'], lw['ln2_b'], alpha, nb, t, chained, tm, _tile(ff, 512))
    new = (c_new.reshape(nb, t, MLA_KV_RANK), kr_new.reshape(nb, t, MLA_ROPE), s_gla, h_re, h_im, pbuf, cbuf)
    return x2, x2b, new


def _trunk(x3, lws, tabs, states, pos0, cache, alpha, chained):
    nb, t, d = x3.shape
    x = x3.reshape(nb * t, d)
    xb = x.astype(BF16)
    outs = [[] for _ in range(7)]
    for l, lw in enumerate(lws):
        st = {k: v[l] for k, v in states.items()}
        x, xb, new = _layer(x, xb, lw, tabs[l], st, nb, t, pos0, l, cache, alpha, chained)
        for lst, val in zip(outs, new):
            lst.append(val)
    return (x.reshape(nb, t, d),) + tuple(jnp.stack(o, 0) for o in outs)


def kernel(x_prompt, x_sample, cache_kv_latent, cache_k_rope, page_table, state_gla, state_s5_re, state_s5_im,
           state_pool, state_ffn_conv, ln1_g, ln1_b, w_in, b_gates, w_gla_gate, b_gla_gate, gla_norm_g,
           s5_a_re, s5_a_im, s5_log_dt, s5_b_re, s5_b_im, s5_c_re, s5_c_im, s5_d, w_s5_glu, b_s5_glu,
           w_pool, pool_scale, mla_q_norm, w_q_up, mla_kv_norm, w_kv_up, w_branch, w_out, ln2_g, ln2_b,
           w_ffn_up, ffn_conv_w, ffn_conv_b, w_ffn_down):
    params = {'ln1_g': ln1_g, 'ln1_b': ln1_b, 'w_in': w_in, 'b_gates': b_gates, 'w_gla_gate': w_gla_gate,
              'b_gla_gate': b_gla_gate, 'gla_norm_g': gla_norm_g, 's5_b_re': s5_b_re, 's5_b_im': s5_b_im,
              's5_c_re': s5_c_re, 's5_c_im': s5_c_im, 's5_d': s5_d, 'w_s5_glu': w_s5_glu, 'b_s5_glu': b_s5_glu,
              'w_pool': w_pool, 'pool_scale': pool_scale, 'mla_q_norm': mla_q_norm, 'w_q_up': w_q_up,
              'mla_kv_norm': mla_kv_norm, 'w_kv_up': w_kv_up, 'w_branch': w_branch, 'w_out': w_out,
              'ln2_g': ln2_g, 'ln2_b': ln2_b, 'w_ffn_up': w_ffn_up, 'ffn_conv_w': ffn_conv_w,
              'ffn_conv_b': ffn_conv_b, 'w_ffn_down': w_ffn_down}
    depth, d_model = w_in.shape[0], w_in.shape[1]
    d_ff = ffn_conv_w.shape[-1]
    alpha = (2 * depth) ** 0.25
    lws = [_prep_layer({k: v[l] for k, v in params.items()}, d_model) for l in range(depth)]
    tabs = s5_tables(s5_a_re, s5_a_im, s5_log_dt)

    nb_p, t_p, _ = x_prompt.shape
    zeros = lambda *shape: jnp.zeros((depth, nb_p) + shape, F32)
    st_p = dict(gla=zeros(GLA_HEADS, GLA_DK, GLA_DV), s5_re=zeros(S5_GROUPS, S5_STATE),
                s5_im=zeros(S5_GROUPS, S5_STATE), pool=zeros(POOL_BUF, MIX_W), conv=zeros(CONV_W - 1, d_ff))
    res_p = _trunk(x_prompt, lws, tabs, st_p, 0, None, alpha, True)

    past_len = page_table.shape[1] * PAGE_SIZE
    st_s = dict(gla=state_gla, s5_re=state_s5_re, s5_im=state_s5_im, pool=state_pool, conv=state_ffn_conv)
    cache_krt = jnp.swapaxes(cache_k_rope, 2, 3)
    res_s = _trunk(x_sample, lws, tabs, st_s, past_len, (cache_kv_latent, cache_krt, page_table), alpha, False)
    return (res_p[0], res_s[0]) + res_p[1:] + res_s[1:]
```

```python
import functools
import math

import jax
import jax.numpy as jnp
from jax import lax
from jax.experimental import pallas as pl
from jax.experimental.pallas import tpu as pltpu

F32 = jnp.float32
BF16 = jnp.bfloat16
HIGHEST = lax.Precision.HIGHEST

MIX_W = 512
N_BRANCH = 4
GLA_HEADS, GLA_DK, GLA_DV, GLA_GATE_RANK, GLA_TAU = 4, 64, 128, 16, 16.0
S5_GROUP, S5_GROUPS, S5_STATE = 16, 32, 64
S5_W = S5_GROUPS * S5_STATE
POOL_WINDOWS = (2, 4, 8, 16)
POOL_GROUP = 128
POOL_BUF = 15
MLA_HEADS, MLA_NOPE, MLA_ROPE, MLA_V, MLA_Q_RANK, MLA_KV_RANK = 4, 128, 64, 128, 384, 128
MLA_SCALE = (MLA_NOPE + MLA_ROPE) ** -0.5
MLA_QK = 256
ROPE_THETA = 10000.0
PAGE_SIZE = 128
CONV_W = 3
LN_EPS = 1e-5
RMS_EPS = 1e-6

Z_Q, Z_K, Z_V, Z_OG, Z_SU, Z_PU, Z_MQ, Z_MKV, Z_MKR, Z_MKR_ROT, Z_GA = (
    0, 256, 512, 1024, 1536, 2048, 2560, 2944, 3072, 3136, 3200)
Z_COLS = 3328

V7X_VMEM_BYTES = 64 * 1024 * 1024
VMEM_LIMIT = 48 * 1024 * 1024
SUBLANES = 8


def _cp(sem, vmem=VMEM_LIMIT):
    return pltpu.CompilerParams(dimension_semantics=sem, vmem_limit_bytes=vmem)


def _tile(dim, pref):
    t = min(dim, pref)
    while dim % t:
        t //= 2
    return t


def _sigmoid(x):
    return 1.0 / (1.0 + jnp.exp(-x))


def _gelu_tanh(x):
    c = math.sqrt(2.0 / math.pi)
    return 0.5 * x * (1.0 + jnp.tanh(c * (x + 0.044715 * (x * x * x))))


def _log_sigmoid(x):
    return jnp.minimum(x, 0.0) - jnp.log(1.0 + jnp.exp(-jnp.abs(x)))


def _layer_norm(y, g, b):
    mu = jnp.mean(y, -1, keepdims=True)
    d = y - mu
    var = jnp.mean(d * d, -1, keepdims=True)
    return d * lax.rsqrt(var + LN_EPS) * g + b


def _rms(x, g):
    return x * lax.rsqrt(jnp.mean(x * x, -1, keepdims=True) + RMS_EPS) * g


def _dot(a, b):
    return jnp.dot(a, b, preferred_element_type=F32)


def _dot_t(a, b):
    return lax.dot_general(a, b, (((1,), (1,)), ((), ())), preferred_element_type=F32)


def _dot_ta(a, b, **kw):
    return lax.dot_general(a, b, (((0,), (0,)), ((), ())), preferred_element_type=F32, **kw)


def _cmul(ar, ai, br, bi):
    return ar * br - ai * bi, ar * bi + ai * br


def _mm_kernel(x_ref, w_ref, o_ref):
    o_ref[...] = _dot(x_ref[...], w_ref[...]).astype(o_ref.dtype)


def matmul(x, w, l, tm, tn, out_dtype=F32):
    m, k = x.shape
    n = w.shape[2]
    return pl.pallas_call(
        _mm_kernel,
        grid=(m // tm, n // tn),
        in_specs=[pl.BlockSpec((tm, k), lambda i, j: (i, 0)),
                  pl.BlockSpec((None, k, tn), lambda i, j: (l, 0, j))],
        out_specs=pl.BlockSpec((tm, tn), lambda i, j: (i, j)),
        out_shape=jax.ShapeDtypeStruct((m, n), out_dtype),
        compiler_params=_cp(("parallel", "parallel")),
        name="in_proj",
    )(x, w)


def _gla_kernel(q_ref, k_ref, v_ref, og_ref, ga_ref, wg_ref, bg_ref, ng_ref, s0_ref,
                ya_ref, st_ref, *, L, nsub, chain):
    R = nsub * L

    @pl.when(pl.program_id(1) == 0)
    def _():
        st_ref[...] = s0_ref[...]

    pre = jnp.dot(ga_ref[...], wg_ref[...], precision=HIGHEST, preferred_element_type=F32) + bg_ref[...]
    log_a = _log_sigmoid(pre) * (1.0 / GLA_TAU)
    row = lax.broadcasted_iota(jnp.int32, (R, R), 0)
    col = lax.broadcasted_iota(jnp.int32, (R, R), 1)
    tri = jnp.where(row >= col, 1.0, 0.0)
    if nsub > 1:
        blk_of = lambda x: sum((x >= j * L).astype(jnp.int32) for j in range(1, nsub))
        same = jnp.where(blk_of(row) == blk_of(col), 1.0, 0.0)
        tri = tri * same
        tot = jnp.dot(same, log_a, precision=HIGHEST, preferred_element_type=F32)
    causal = tri > 0.5
    bc = jnp.dot(tri, log_a, precision=HIGHEST, preferred_element_type=F32)
    if nsub == 1:
        tot = bc[L - 1:L, :]
    q = q_ref[...] * (GLA_DK ** -0.5)
    k = k_ref[...]
    q_in = q * jnp.exp(bc)
    k_in = k * jnp.exp(-bc)
    k_out = k * jnp.exp(tot - bc)
    ones = jnp.ones((L, GLA_DV), F32)
    for h in range(GLA_HEADS):
        ks = slice(h * GLA_DK, (h + 1) * GLA_DK)
        vs = slice(h * GLA_DV, (h + 1) * GLA_DV)
        v = v_ref[:, vs]
        scores = jnp.where(causal, _dot_t(q_in[:, ks].astype(BF16), k_in[:, ks].astype(BF16)), 0.0)
        o = _dot(scores.astype(BF16), v.astype(BF16))
        o_inter = []
        state = st_ref[0, h]
        for j in range(nsub):
            rs = slice(j * L, (j + 1) * L)
            if not chain:
                state = st_ref[j, h]
            o_inter.append(_dot(q_in[rs, ks].astype(BF16), state.astype(BF16)))
            a_tot = jnp.exp(_dot_ta(log_a[rs, ks], ones, precision=HIGHEST))
            state = a_tot * state + _dot_ta(k_out[rs, ks].astype(BF16), v[rs].astype(BF16))
            if not chain:
                st_ref[j, h] = state
        if chain:
            st_ref[0, h] = state
        o = o + (o_inter[0] if nsub == 1 else jnp.concatenate(o_inter, 0))
        o = o * lax.rsqrt(jnp.mean(o * o, -1, keepdims=True) + RMS_EPS) * ng_ref[:, vs]
        g = og_ref[:, vs]
        ya_ref[:, vs] = (o * (g * _sigmoid(g))).astype(ya_ref.dtype)


def gla(z, wg_pad, bg, ng, l, s0, ls, nb, t, L, nsub, chain):
    R = nsub * L
    if chain:
        nc = t // R
        grid = (nb, nc)
        nst = 1
    else:
        assert L == t
        nc = 1
        grid = (nb // nsub, 1)
        nst = nsub
    rb = lambda b, c: b * nc + c
    hk = GLA_HEADS * GLA_DK
    st_blk = (nst, GLA_HEADS, GLA_DK, GLA_DV)
    return pl.pallas_call(
        functools.partial(_gla_kernel, L=L, nsub=nsub, chain=chain),
        grid=grid,
        in_specs=[
            pl.BlockSpec((R, hk), lambda b, c: (rb(b, c), Z_Q // hk)),
            pl.BlockSpec((R, hk), lambda b, c: (rb(b, c), Z_K // hk)),
            pl.BlockSpec((R, MIX_W), lambda b, c: (rb(b, c), Z_V // MIX_W)),
            pl.BlockSpec((R, MIX_W), lambda b, c: (rb(b, c), Z_OG // MIX_W)),
            pl.BlockSpec((R, 128), lambda b, c: (rb(b, c), Z_GA // 128)),
            pl.BlockSpec((None, 128, hk), lambda b, c: (l, 0, 0)),
            pl.BlockSpec((None, 1, hk), lambda b, c: (l, 0, 0)),
            pl.BlockSpec((None, 1, MIX_W), lambda b, c: (l, 0, 0)),
            pl.BlockSpec((None,) + st_blk, lambda b, c: (ls, b, 0, 0, 0)),
        ],
        out_specs=[pl.BlockSpec((R, MIX_W), lambda b, c: (rb(b, c), 0)),
                   pl.BlockSpec(st_blk, lambda b, c: (b, 0, 0, 0))],
        out_shape=[jax.ShapeDtypeStruct((nb * t, MIX_W), BF16),
                   jax.ShapeDtypeStruct((nb, GLA_HEADS, GLA_DK, GLA_DV), F32)],
        compiler_params=_cp(("parallel", "arbitrary")),
        name="gla",
    )(z, z, z, z, z, wg_pad, bg, ng, s0)


S5_NJ_CHAIN = 32
TAB_A, TAB_T1, TAB_T2, TAB_T4, TAB_F = 0, 2, 4, 6, 8
N_TAB = 10
S5_LANES = 512


def _s5_tab_kernel(are_ref, aim_ref, ldt_ref, tab_ref, pj_ref, *, nj):
    lr = are_ref[0]
    li = aim_ref[0]
    dt = jnp.exp(ldt_ref[0])
    mag = jnp.exp(lr * dt)
    a_re = mag * jnp.cos(li * dt)
    a_im = mag * jnp.sin(li * dt)
    shp = (SUBLANES, S5_W)
    r = lax.broadcasted_iota(jnp.int32, shp, 0)
    tab_ref[0, TAB_A] = jnp.broadcast_to(a_re, shp)
    tab_ref[0, TAB_A + 1] = jnp.broadcast_to(a_im, shp)
    pr, pi = a_re, a_im
    for j in range(nj):
        pj_ref[0, 0, j * SUBLANES:(j + 1) * SUBLANES, :] = jnp.broadcast_to(pr, shp)
        pj_ref[0, 1, j * SUBLANES:(j + 1) * SUBLANES, :] = jnp.broadcast_to(pi, shp)
        if j + 1 < nj:
            pr, pi = _cmul(pr, pi, a_re, a_im)
    br, bi = pr, pi
    for s, idx in ((1, TAB_T1), (2, TAB_T2), (4, TAB_T4)):
        tab_ref[0, idx] = jnp.where(r >= s, br, 0.0)
        tab_ref[0, idx + 1] = jnp.where(r >= s, bi, 0.0)
        br, bi = _cmul(br, bi, br, bi)
    den = lr * lr + li * li
    nr, ni = a_re - 1.0, a_im
    tab_ref[0, TAB_F] = jnp.broadcast_to((nr * lr + ni * li) / den, shp)
    tab_ref[0, TAB_F + 1] = jnp.broadcast_to((ni * lr - nr * li) / den, shp)


def s5_tables(a_re, a_im, log_dt, nj):
    depth = a_re.shape[0]
    flat = lambda a: a.reshape(depth, 1, S5_W)
    ldt = jnp.broadcast_to(log_dt[:, :, None], (depth, S5_GROUPS, S5_STATE))
    spec = pl.BlockSpec((1, 1, S5_W), lambda l: (l, 0, 0))
    return pl.pallas_call(
        functools.partial(_s5_tab_kernel, nj=nj),
        grid=(depth,),
        in_specs=[spec, spec, spec],
        out_specs=[pl.BlockSpec((1, N_TAB, SUBLANES, S5_W), lambda l: (l, 0, 0, 0)),
                   pl.BlockSpec((1, 2, nj * SUBLANES, S5_W), lambda l: (l, 0, 0, 0))],
        out_shape=[jax.ShapeDtypeStruct((depth, N_TAB, SUBLANES, S5_W), F32),
                   jax.ShapeDtypeStruct((depth, 2, nj * SUBLANES, S5_W), F32)],
        compiler_params=_cp(("parallel",)),
        name="s5_tables",
    )(flat(a_re), flat(a_im), flat(ldt))


def _s5_kernel(u0_ref, u1_ref, u2_ref, u3_ref, tab_ref, pj_ref, wbr_ref, wbi_ref, wcr_ref, wci_ref, d_ref,
               wglu_ref, bglu_ref, h0r_ref, h0i_ref, yb_ref, hr_out_ref, hi_out_ref,
               up_ref, xr_ref, xi_ref, cr_ref, ci_ref, *, R, nj, chained):
    half_u = MIX_W // 2
    half_s = S5_W // 2
    grp = SUBLANES * nj
    ngrp = R // grp
    ncol = MIX_W // 128
    for c, u_ref in enumerate((u0_ref, u1_ref, u2_ref, u3_ref)):
        for g in range(ngrp):
            for j in range(nj):
                up_ref[c, g * grp + j * SUBLANES:g * grp + (j + 1) * SUBLANES, :] = (
                    u_ref[pl.ds(g * grp + j, SUBLANES, stride=nj), :])
    u = jnp.concatenate([up_ref[c] for c in range(ncol)], 1)
    ub = u.astype(BF16)
    fr = tab_ref[TAB_F][0:1]
    fi = tab_ref[TAB_F + 1][0:1]
    for hf in range(2):
        us = ub[:, hf * half_u:(hf + 1) * half_u]
        ss = slice(hf * half_s, (hf + 1) * half_s)
        bur = _dot(us, wbr_ref[hf])
        bui = _dot(us, wbi_ref[hf])
        xr_ref[:, ss] = fr[:, ss] * bur - fi[:, ss] * bui
        xi_ref[:, ss] = fr[:, ss] * bui + fi[:, ss] * bur

    if chained:
        @pl.when(pl.program_id(1) == 0)
        def _():
            cr_ref[...] = h0r_ref[0]
            ci_ref[...] = h0i_ref[0]

    row8 = lax.broadcasted_iota(jnp.int32, (SUBLANES, S5_LANES), 0)
    for lc in range(S5_W // S5_LANES):
        ls = slice(lc * S5_LANES, (lc + 1) * S5_LANES)
        ar = tab_ref[TAB_A, :, ls]
        ai = tab_ref[TAB_A + 1, :, ls]
        for g in range(ngrp):
            rows = lambda j: slice(g * grp + j * SUBLANES, g * grp + (j + 1) * SUBLANES)
            if chained:
                hr = jnp.zeros((SUBLANES, S5_LANES), F32)
                hi = jnp.zeros((SUBLANES, S5_LANES), F32)
            else:
                hr = h0r_ref[g * SUBLANES:(g + 1) * SUBLANES, ls]
                hi = h0i_ref[g * SUBLANES:(g + 1) * SUBLANES, ls]
            for j in range(nj):
                pr, pi = _cmul(ar, ai, hr, hi)
                hr = pr + xr_ref[rows(j), ls]
                hi = pi + xi_ref[rows(j), ls]
                xr_ref[rows(j), ls] = hr
                xi_ref[rows(j), ls] = hi
            if not chained:
                hr_out_ref[g * SUBLANES:(g + 1) * SUBLANES, ls] = hr
                hi_out_ref[g * SUBLANES:(g + 1) * SUBLANES, ls] = hi
                continue
            yr = jnp.where(row8 == 0, cr_ref[:, ls], pltpu.roll(hr, 1, 0))
            yi = jnp.where(row8 == 0, ci_ref[:, ls], pltpu.roll(hi, 1, 0))
            for s, idx in ((1, TAB_T1), (2, TAB_T2), (4, TAB_T4)):
                pr, pi = _cmul(tab_ref[idx, :, ls], tab_ref[idx + 1, :, ls],
                               pltpu.roll(yr, s, 0), pltpu.roll(yi, s, 0))
                yr, yi = yr + pr, yi + pi
            for j in range(nj):
                pr, pi = _cmul(pj_ref[0, rows(j), ls], pj_ref[1, rows(j), ls], yr, yi)
                hr = xr_ref[rows(j), ls] + pr
                hi = xi_ref[rows(j), ls] + pi
                xr_ref[rows(j), ls] = hr
                xi_ref[rows(j), ls] = hi
            cr_ref[:, ls] = hr[SUBLANES - 1:SUBLANES]
            ci_ref[:, ls] = hi[SUBLANES - 1:SUBLANES]

    if chained:
        @pl.when(pl.program_id(1) == pl.num_programs(1) - 1)
        def _():
            hr_out_ref[0] = cr_ref[...]
            hi_out_ref[0] = ci_ref[...]

    ys = []
    for hf in range(2):
        ss = slice(hf * half_s, (hf + 1) * half_s)
        y = _dot(xr_ref[:, ss].astype(BF16), wcr_ref[hf]) - _dot(xi_ref[:, ss].astype(BF16), wci_ref[hf])
        us = slice(hf * half_u, (hf + 1) * half_u)
        ys.append(_gelu_tanh(y + d_ref[:, us] * u[:, us]))
    y = jnp.concatenate(ys, 1)
    y = y * _sigmoid(_dot(y.astype(BF16), wglu_ref[...]) + bglu_ref[...])
    for c in range(ncol):
        up_ref[c] = y[:, c * 128:(c + 1) * 128]
        for g in range(ngrp):
            for r in range(SUBLANES):
                yb_ref[g * grp + r * nj:g * grp + (r + 1) * nj, c * 128:(c + 1) * 128] = (
                    up_ref[c, pl.ds(g * grp + r, nj, stride=SUBLANES), :].astype(yb_ref.dtype))


def s5(z, tab, pj, wbr, wbi, wcr, wci, d, wglu, bglu, l, h0r, h0i, ls, nb, t, chained, R):
    depth_s = h0r.shape[0]
    if chained:
        nj = S5_NJ_CHAIN
        assert R == SUBLANES * nj
        nc = t // R
        grid = (nb, nc)
        rb = lambda b, c: b * nc + c
        h0 = (h0r.reshape(depth_s, nb, 1, S5_W), h0i.reshape(depth_s, nb, 1, S5_W))
        st_in = pl.BlockSpec((None, 1, 1, S5_W), lambda b, c: (ls, b, 0, 0))
        st_out = pl.BlockSpec((1, 1, S5_W), lambda b, c: (b, 0, 0))
        st_shape = jax.ShapeDtypeStruct((nb, 1, S5_W), F32)
        out_dtype = BF16
    else:
        assert t == SUBLANES
        nj = SUBLANES
        nseq = R // SUBLANES
        grid = (nb // nseq, 1)
        rb = lambda b, c: b
        h0 = (h0r, h0i)
        st_in = pl.BlockSpec((None, nseq, S5_W), lambda b, c: (ls, b, 0))
        st_out = pl.BlockSpec((nseq, S5_W), lambda b, c: (b, 0))
        st_shape = jax.ShapeDtypeStruct((nb, S5_W), F32)
        out_dtype = F32
    lay = lambda shape: pl.BlockSpec((None,) + shape, lambda b, c: (l,) + (0,) * len(shape))
    yb, hr, hi = pl.pallas_call(
        functools.partial(_s5_kernel, R=R, nj=nj, chained=chained),
        grid=grid,
        in_specs=[pl.BlockSpec((R, 128), lambda b, c, k=k: (rb(b, c), Z_SU // 128 + k))
                  for k in range(MIX_W // 128)] + [
            lay((N_TAB, SUBLANES, S5_W)),
            lay((2, pj.shape[2], S5_W)),
            lay((2, MIX_W // 2, S5_W // 2)), lay((2, MIX_W // 2, S5_W // 2)),
            lay((2, S5_W // 2, MIX_W // 2)), lay((2, S5_W // 2, MIX_W // 2)),
            lay((1, MIX_W)), lay((MIX_W, MIX_W)), lay((1, MIX_W)),
            st_in, st_in,
        ],
        out_specs=[pl.BlockSpec((R, MIX_W), lambda b, c: (rb(b, c), 0)), st_out, st_out],
        out_shape=[jax.ShapeDtypeStruct((nb * t, MIX_W), out_dtype), st_shape, st_shape],
        scratch_shapes=[pltpu.VMEM((MIX_W // 128, R, 128), F32), pltpu.VMEM((R, S5_W), F32),
                        pltpu.VMEM((R, S5_W), F32), pltpu.VMEM((1, S5_W), F32), pltpu.VMEM((1, S5_W), F32)],
        compiler_params=_cp(("parallel", "arbitrary")),
        name="s5",
    )(z, z, z, z, tab, pj, wbr, wbi, wcr, wci, d, wglu, bglu, *h0)
    return yb, hr.reshape(nb, S5_GROUPS, S5_STATE), hi.reshape(nb, S5_GROUPS, S5_STATE)


HALO = 16


def _pool_windows(ext_ref, base, n, pos, wp_ref, sc_ref, out_ref, out_rows):
    for g, w in enumerate(POOL_WINDOWS):
        cs = slice(g * POOL_GROUP, (g + 1) * POOL_GROUP)
        u = ext_ref[pl.ds(base, n), cs]
        s = u
        for j in range(1, w):
            s = s + ext_ref[pl.ds(base - j, n), cs]
        cnt = jnp.minimum(pos + 1, w).astype(F32)
        dlt = s / cnt - u
        y = _dot(dlt.astype(BF16), wp_ref[g]) * sc_ref[:, cs]
        out_ref[out_rows, cs] = y.astype(out_ref.dtype)


def _pool_chain_kernel(u_ref, wp_ref, sc_ref, pre_ref, yc_ref, buf_ref, ext_ref, *, L, pos0):
    c = pl.program_id(1)

    @pl.when(c == 0)
    def _():
        ext_ref[0:HALO, :] = pre_ref[0]

    ext_ref[HALO:HALO + L, :] = u_ref[...]
    pos = pos0 + c * L + lax.broadcasted_iota(jnp.int32, (L, 1), 0)
    _pool_windows(ext_ref, HALO, L, pos, wp_ref, sc_ref, yc_ref, slice(None))
    tail = ext_ref[L:L + HALO, :]
    ext_ref[0:HALO, :] = tail

    @pl.when(c == pl.num_programs(1) - 1)
    def _():
        buf_ref[0] = tail


def _pool_seq_kernel(u_ref, wp_ref, sc_ref, pre_ref, yc_ref, buf_ref, ext_ref, *, nseq, pos0):
    t = SUBLANES
    pos = pos0 + lax.broadcasted_iota(jnp.int32, (t, 1), 0)
    for j in range(nseq):
        ext_ref[0:HALO, :] = pre_ref[j]
        ext_ref[HALO:HALO + t, :] = u_ref[j * t:(j + 1) * t, :]
        _pool_windows(ext_ref, HALO, t, pos, wp_ref, sc_ref, yc_ref, slice(j * t, (j + 1) * t))
        buf_ref[j] = ext_ref[t:t + HALO, :]


def pool(z, wp, sc, l, pre, ls, nb, t, pos0, chained, L):
    lay = lambda shape: pl.BlockSpec((None,) + shape, lambda b, c: (l,) + (0,) * len(shape))
    if chained:
        nc = t // L
        grid = (nb, nc)
        kern = functools.partial(_pool_chain_kernel, L=L, pos0=pos0)
        u_spec = pl.BlockSpec((L, MIX_W), lambda b, c: (b * nc + c, Z_PU // MIX_W))
        y_spec = pl.BlockSpec((L, MIX_W), lambda b, c: (b * nc + c, 0))
        nst = 1
        ext_rows = HALO + L
    else:
        assert t == SUBLANES
        nst = L // t
        grid = (nb // nst, 1)
        kern = functools.partial(_pool_seq_kernel, nseq=nst, pos0=pos0)
        u_spec = pl.BlockSpec((L, MIX_W), lambda b, c: (b, Z_PU // MIX_W))
        y_spec = pl.BlockSpec((L, MIX_W), lambda b, c: (b, 0))
        ext_rows = HALO + t
    yc, buf = pl.pallas_call(
        kern,
        grid=grid,
        in_specs=[u_spec, lay((len(POOL_WINDOWS), POOL_GROUP, POOL_GROUP)), lay((1, MIX_W)),
                  pl.BlockSpec((None, nst, HALO, MIX_W), lambda b, c: (ls, b, 0, 0))],
        out_specs=[y_spec, pl.BlockSpec((nst, HALO, MIX_W), lambda b, c: (b, 0, 0))],
        out_shape=[jax.ShapeDtypeStruct((nb * t, MIX_W), BF16),
                   jax.ShapeDtypeStruct((nb, HALO, MIX_W), F32)],
        scratch_shapes=[pltpu.VMEM((ext_rows, MIX_W), F32)],
        compiler_params=_cp(("parallel", "arbitrary")),
        name="pool",
    )(z, wp, sc, pre)
    return yc, buf[:, 1:, :]


def _mla_prep_kernel(zq_ref, zr_ref, cq_ref, sq_ref, qg_ref, kg_ref, wq_ref, wuk_ref,
                     qcat_ref, kcat_ref, c_ref, kr_ref):
    nq = MLA_HEADS * MLA_NOPE
    nr = MLA_HEADS * MLA_ROPE
    mq = zq_ref[:, 0:MLA_Q_RANK]
    mkv = zq_ref[:, MLA_Q_RANK:MLA_Q_RANK + MLA_KV_RANK]
    qd = _dot(_rms(mq, qg_ref[...]).astype(BF16), wq_ref[...])
    cos4 = cq_ref[...]
    sin4 = sq_ref[...]
    q_rope = (qd[:, nq:nq + nr] * cos4 + qd[:, nq + nr:nq + 2 * nr] * sin4) * MLA_SCALE
    zpad = jnp.zeros((qd.shape[0], MLA_QK - MLA_KV_RANK - MLA_ROPE), qcat_ref.dtype)
    for h in range(MLA_HEADS):
        q_lat = _dot(qd[:, h * MLA_NOPE:(h + 1) * MLA_NOPE].astype(BF16), wuk_ref[h]) * MLA_SCALE
        qcat_ref[h, :, 0:MLA_KV_RANK] = q_lat.astype(qcat_ref.dtype)
        qcat_ref[h, :, MLA_KV_RANK:MLA_KV_RANK + MLA_ROPE] = (
            q_rope[:, h * MLA_ROPE:(h + 1) * MLA_ROPE].astype(qcat_ref.dtype))
        qcat_ref[h, :, MLA_KV_RANK + MLA_ROPE:MLA_QK] = zpad
    c_new = _rms(mkv, kg_ref[...])
    kr_new = (zr_ref[:, 0:MLA_ROPE] * cos4[:, 0:MLA_ROPE]
              + zr_ref[:, MLA_ROPE:2 * MLA_ROPE] * sin4[:, 0:MLA_ROPE])
    c_ref[...] = c_new
    kr_ref[...] = kr_new
    kcat_ref[:, 0:MLA_KV_RANK] = c_new.astype(kcat_ref.dtype)
    kcat_ref[:, MLA_KV_RANK:MLA_KV_RANK + MLA_ROPE] = kr_new.astype(kcat_ref.dtype)
    kcat_ref[:, MLA_KV_RANK + MLA_ROPE:MLA_QK] = zpad


def mla_prep(z, cos4, sin4, qg, kg, wq, wuk, l, L, cat_dtype):
    n = z.shape[0]
    nt = cos4.shape[0] // L
    lay = lambda shape: pl.BlockSpec((None,) + shape, lambda i: (l,) + (0,) * len(shape))
    tab_spec = pl.BlockSpec((L, MLA_HEADS * MLA_ROPE), lambda i: (i % nt, 0))
    return pl.pallas_call(
        _mla_prep_kernel,
        grid=(n // L,),
        in_specs=[
            pl.BlockSpec((L, MIX_W), lambda i: (i, Z_MQ // MIX_W)),
            pl.BlockSpec((L, 128), lambda i: (i, Z_MKR // 128)),
            tab_spec, tab_spec,
            lay((1, MLA_Q_RANK)), lay((1, MLA_KV_RANK)),
            lay((MLA_Q_RANK, MLA_HEADS * (MLA_NOPE + 2 * MLA_ROPE))),
            lay((MLA_HEADS, MLA_NOPE, MLA_KV_RANK)),
        ],
        out_specs=[
            pl.BlockSpec((MLA_HEADS, L, MLA_QK), lambda i: (0, i, 0)),
            pl.BlockSpec((L, MLA_QK), lambda i: (i, 0)),
            pl.BlockSpec((L, MLA_KV_RANK), lambda i: (i, 0)),
            pl.BlockSpec((L, MLA_ROPE), lambda i: (i, 0)),
        ],
        out_shape=[
            jax.ShapeDtypeStruct((MLA_HEADS, n, MLA_QK), cat_dtype),
            jax.ShapeDtypeStruct((n, MLA_QK), cat_dtype),
            jax.ShapeDtypeStruct((n, MLA_KV_RANK), F32),
            jax.ShapeDtypeStruct((n, MLA_ROPE), F32),
        ],
        compiler_params=_cp(("parallel",)),
        name="mla_prep",
    )(z, z, cos4, sin4, qg, kg, wq, wuk)


NEG_BIG = -1e30


def _causal_mask(tq, tk):
    row = lax.broadcasted_iota(jnp.int32, (MLA_HEADS * tq, tk), 0)
    key = lax.broadcasted_iota(jnp.int32, (MLA_HEADS * tq, tk), 1)
    head = sum((row >= h * tq).astype(jnp.int32) for h in range(1, MLA_HEADS))
    return key <= row - head * tq


def _mla_causal_kernel(q_ref, k_ref, wuv_ref, yd_ref, m_ref, l_ref, acc_ref, *, tb):
    qi = pl.program_id(1)
    rows = MLA_HEADS * tb
    m_ref[...] = jnp.full(m_ref.shape, NEG_BIG, F32)
    l_ref[...] = jnp.zeros(l_ref.shape, F32)
    acc_ref[...] = jnp.zeros(acc_ref.shape, F32)
    q = q_ref[...].reshape(rows, MLA_QK)

    def block(kj, masked):
        k = k_ref[pl.ds(pl.multiple_of(kj * tb, tb), tb), :]
        s = _dot_t(q, k)
        if masked:
            s = jnp.where(_causal_mask(tb, tb), s, NEG_BIG)
        m_old = m_ref[...]
        m_new = jnp.maximum(m_old, jnp.max(s, -1, keepdims=True))
        alpha = jnp.exp(m_old - m_new)
        p = jnp.exp(s - jnp.tile(m_new, (1, tb // 128)))
        l_ref[...] = alpha * l_ref[...] + jnp.sum(p, -1, keepdims=True)
        acc_ref[...] = alpha * acc_ref[...] + _dot(p.astype(BF16), k[:, 0:MLA_KV_RANK])
        m_ref[...] = m_new

    def body(kj, carry):
        block(kj, False)
        return carry

    lax.fori_loop(0, qi, body, 0)
    block(qi, True)
    o = acc_ref[...] / l_ref[...]
    for h in range(MLA_HEADS):
        oh = o[h * tb:(h + 1) * tb, :].astype(BF16)
        yd_ref[:, h * MLA_V:(h + 1) * MLA_V] = _dot(oh, wuv_ref[h]).astype(yd_ref.dtype)


def mla_causal(qcat, kcat, wuv, l, nb, t, tb):
    nq = t // tb
    rows = MLA_HEADS * tb
    return pl.pallas_call(
        functools.partial(_mla_causal_kernel, tb=tb),
        grid=(nb, nq),
        in_specs=[
            pl.BlockSpec((MLA_HEADS, tb, MLA_QK), lambda b, i: (0, b * nq + i, 0)),
            pl.BlockSpec((t, MLA_QK), lambda b, i: (b, 0)),
            pl.BlockSpec((None, MLA_HEADS, MLA_KV_RANK, MLA_V), lambda b, i: (l, 0, 0, 0)),
        ],
        out_specs=pl.BlockSpec((tb, MIX_W), lambda b, i: (b * nq + i, 0)),
        out_shape=jax.ShapeDtypeStruct((nb * t, MIX_W), BF16),
        scratch_shapes=[pltpu.VMEM((rows, 128), F32), pltpu.VMEM((rows, 128), F32),
                        pltpu.VMEM((rows, MLA_KV_RANK), F32)],
        compiler_params=_cp(("parallel", "arbitrary")),
        name="mla_causal",
    )(qcat, kcat, wuv)


PAGED_GROUP = 4


def _mla_paged_kernel(pt_ref, q_ref, kn_ref, cache_c_ref, cache_krt_ref, wuv_ref, yd_ref,
                      cbuf_ref, krbuf_ref, sem_ref, cb16_ref, s_ref, *, layer, n_pages, kchunk):
    g = pl.program_id(0)
    ng = pl.num_programs(0)
    t = SUBLANES
    hr = MLA_HEADS * t
    past = n_pages * PAGE_SIZE
    nchunks = past // kchunk
    ppc = kchunk // PAGE_SIZE

    def page_copies(base, slot, p, page_of):
        out = []
        for ab in range(2):
            page = page_of(base + ab, p)
            out.append(pltpu.make_async_copy(
                cache_c_ref.at[layer, page],
                cbuf_ref.at[slot, pl.ds(p * PAGE_SIZE, PAGE_SIZE), pl.ds(ab * MLA_KV_RANK, MLA_KV_RANK)],
                sem_ref.at[0, slot]))
            out.append(pltpu.make_async_copy(
                cache_krt_ref.at[layer, page],
                krbuf_ref.at[slot, pl.ds(ab * MLA_ROPE, MLA_ROPE), pl.ds(p * PAGE_SIZE, PAGE_SIZE)],
                sem_ref.at[1, slot]))
        return out

    table_page = lambda seq, p: pt_ref[seq * n_pages + p]

    def issue(base, slot, chunk):
        for p in range(chunk * ppc, (chunk + 1) * ppc):
            for cp in page_copies(base, slot, p, table_page):
                cp.start()

    def wait_all(slot):
        for p in range(n_pages):
            for cp in page_copies(0, slot, p, lambda seq, p: 0):
                cp.wait()

    @pl.when(g == 0)
    def _():
        for chunk in range(nchunks):
            issue(0, 0, chunk)

    qf = q_ref[...]
    knf = kn_ref[...]
    mask_new = _causal_mask(t, t)
    next_base = jnp.minimum(PAGED_GROUP * (g + 1), PAGED_GROUP * (ng - 1))

    for half in range(2):
        slot = half
        wait_all(slot)
        fill_base, fill_slot = (PAGED_GROUP * g + 2, 1) if half == 0 else (next_base, 0)
        qs = [qf[:, (2 * half + ab) * t:(2 * half + ab + 1) * t, :].reshape(hr, MLA_QK) for ab in range(2)]
        kns = [knf[(2 * half + ab) * t:(2 * half + ab + 1) * t, :].astype(BF16) for ab in range(2)]
        z_lat = jnp.zeros((hr, MLA_KV_RANK), F32)
        z_rope = jnp.zeros((hr, MLA_ROPE), F32)
        rope = slice(MLA_KV_RANK, MLA_KV_RANK + MLA_ROPE)
        q_lat = jnp.concatenate([jnp.concatenate([qs[0][:, 0:MLA_KV_RANK], z_lat], 1),
                                 jnp.concatenate([z_lat, qs[1][:, 0:MLA_KV_RANK]], 1)], 0).astype(BF16)
        q_rope = jnp.concatenate([jnp.concatenate([qs[0][:, rope], z_rope], 1),
                                  jnp.concatenate([z_rope, qs[1][:, rope]], 1)], 0).astype(BF16)
        for chunk in range(nchunks):
            ks = slice(chunk * kchunk, (chunk + 1) * kchunk)
            cb = cbuf_ref[slot, ks, :].astype(BF16)
            cb16_ref[ks, :] = cb
            krb = krbuf_ref[slot, :, ks].astype(BF16)
            s_ref[:, ks] = _dot_t(q_lat, cb) + _dot(q_rope, krb)
            issue(fill_base, fill_slot, chunk)
        s_new = jnp.concatenate(
            [jnp.where(mask_new, _dot_t(qs[ab].astype(BF16), kns[ab]), NEG_BIG) for ab in range(2)], 0)
        s_past = s_ref[...]
        m = jnp.maximum(jnp.max(s_past, -1, keepdims=True), jnp.max(s_new, -1, keepdims=True))
        p_past = jnp.exp(s_past - m)
        p_new = jnp.exp(s_new - m)
        l = jnp.sum(p_past, -1, keepdims=True) + jnp.sum(p_new, -1, keepdims=True)
        o_pair = _dot(p_past.astype(BF16), cb16_ref[...])
        o = []
        for ab in range(2):
            rs = slice(ab * hr, (ab + 1) * hr)
            o_ab = (o_pair[rs, ab * MLA_KV_RANK:(ab + 1) * MLA_KV_RANK]
                    + _dot(p_new[rs].astype(BF16), kns[ab][:, 0:MLA_KV_RANK]))
            o.append(o_ab / l[rs])
        for h in range(MLA_HEADS):
            oh = jnp.concatenate([o[0][h * t:(h + 1) * t], o[1][h * t:(h + 1) * t]], 0).astype(BF16)
            yd_ref[2 * half * t:(2 * half + 2) * t, h * MLA_V:(h + 1) * MLA_V] = (
                _dot(oh, wuv_ref[h]).astype(yd_ref.dtype))

    @pl.when(g == ng - 1)
    def _():
        wait_all(0)


def mla_paged(qcat, kcat, cache_c, cache_krt, page_table, wuv, layer):
    nb, n_pages = page_table.shape
    assert nb % PAGED_GROUP == 0
    t = SUBLANES
    past = n_pages * PAGE_SIZE
    kchunk = _tile(past, 2048)
    rows = PAGED_GROUP * t
    grid_spec = pltpu.PrefetchScalarGridSpec(
        num_scalar_prefetch=1,
        grid=(nb // PAGED_GROUP,),
        in_specs=[
            pl.BlockSpec((MLA_HEADS, rows, MLA_QK), lambda g, pt: (0, g, 0)),
            pl.BlockSpec((rows, MLA_QK), lambda g, pt: (g, 0)),
            pl.BlockSpec(memory_space=pl.ANY),
            pl.BlockSpec(memory_space=pl.ANY),
            pl.BlockSpec((None, MLA_HEADS, MLA_KV_RANK, MLA_V), lambda g, pt: (layer, 0, 0, 0)),
        ],
        out_specs=pl.BlockSpec((rows, MIX_W), lambda g, pt: (g, 0)),
        scratch_shapes=[
            pltpu.VMEM((2, past, 2 * MLA_KV_RANK), F32),
            pltpu.VMEM((2, 2 * MLA_ROPE, past), F32),
            pltpu.SemaphoreType.DMA((2, 2)),
            pltpu.VMEM((past, 2 * MLA_KV_RANK), BF16),
            pltpu.VMEM((2 * MLA_HEADS * t, past), F32),
        ],
    )
    return pl.pallas_call(
        functools.partial(_mla_paged_kernel, layer=layer, n_pages=n_pages, kchunk=kchunk),
        grid_spec=grid_spec,
        out_shape=jax.ShapeDtypeStruct((nb * t, MIX_W), F32),
        compiler_params=_cp(("arbitrary",)),
        name="mla_paged",
    )(page_table.reshape(-1), qcat, kcat, cache_c, cache_krt, wuv)


def _merge_kernel(x_ref, *refs):
    brs, wgs, bgs = refs[0:N_BRANCH], refs[N_BRANCH:2 * N_BRANCH], refs[2 * N_BRANCH:3 * N_BRANCH]
    wb_ref, o_ref = refs[3 * N_BRANCH:]
    x = x_ref[...]
    acc = None
    for g in range(N_BRANCH):
        gate = _sigmoid(_dot(x, wgs[g][...]) + bgs[g][...])
        term = _dot(brs[g][...].astype(BF16), wb_ref[g]) * gate
        acc = term if acc is None else acc + term
    o_ref[...] = acc.astype(o_ref.dtype)


def merge(xb, ys, wg, bg, wb, l, tm, tn):
    n, d = xb.shape
    nj = d // tn
    row = lambda shape: pl.BlockSpec(shape, lambda j, i: (i, 0))
    gate_cols = lambda rows: [pl.BlockSpec((None, rows, tn), lambda j, i, g=g: (l, 0, g * nj + j))
                              for g in range(N_BRANCH)]
    return pl.pallas_call(
        _merge_kernel,
        grid=(nj, n // tm),
        in_specs=[row((tm, d))] + [row((tm, MIX_W))] * N_BRANCH + gate_cols(d) + gate_cols(1)
        + [pl.BlockSpec((None, N_BRANCH, MIX_W, tn), lambda j, i: (l, 0, 0, j))],
        out_specs=pl.BlockSpec((tm, tn), lambda j, i: (i, j)),
        out_shape=jax.ShapeDtypeStruct((n, d), BF16),
        compiler_params=_cp(("parallel", "parallel")),
        name="merge",
    )(xb, *ys, *([wg] * N_BRANCH), *([bg] * N_BRANCH), wb)


def _out_ln_kernel(m_ref, w_ref, x_ref, g_ref, b_ref, o_ref, *, alpha):
    y = alpha * x_ref[...] + _dot(m_ref[...], w_ref[...])
    o_ref[...] = _layer_norm(y, g_ref[...], b_ref[...])


def out_ln(merged, w_out, x, g, b, l, alpha, tm):
    n, d = x.shape
    row = pl.BlockSpec((tm, d), lambda i: (i, 0))
    lay = lambda shape: pl.BlockSpec((None,) + shape, lambda i: (l, 0, 0))
    return pl.pallas_call(
        functools.partial(_out_ln_kernel, alpha=alpha),
        grid=(n // tm,),
        in_specs=[row, lay((d, d)), row, lay((1, d)), lay((1, d))],
        out_specs=row,
        out_shape=jax.ShapeDtypeStruct((n, d), F32),
        compiler_params=_cp(("parallel",)),
        name="out_ln",
    )(merged, w_out, x, g, b)


FFN_HALO = 16


def _ffn_kernel(x_ref, halo_ref, wh_ref, wg_ref, wd_ref, cw_ref, cb_ref, st_ref, g_ref, b_ref,
                o_ref, ob_ref, hl_ref, xcat_ref, hs_ref, acc_ref, *, tm, alpha, chained, tiles_per_seq):
    i = pl.program_id(0)
    f = pl.program_id(1)

    @pl.when(f == 0)
    def _():
        acc_ref[...] = jnp.zeros(acc_ref.shape, F32)
        xcat_ref[FFN_HALO:, :] = x_ref[...].astype(BF16)
        if chained:
            seq_start = (i % tiles_per_seq) == 0
            xcat_ref[0:FFN_HALO, :] = jnp.where(seq_start, 0.0, halo_ref[...]).astype(BF16)

    w0 = cw_ref[0:1, :]
    w1 = cw_ref[1:2, :]
    w2 = cw_ref[2:3, :]
    if chained:
        hs_ref[...] = _dot(xcat_ref[...], wh_ref[...])
        h = hs_ref[FFN_HALO:, :]
        hm1 = hs_ref[pl.ds(FFN_HALO - 1, tm), :]
        hm2 = hs_ref[pl.ds(FFN_HALO - 2, tm), :]
        hl_ref[0] = hs_ref[tm + FFN_HALO - SUBLANES:, :]
        conv = cb_ref[...] + w0 * hm2 + w1 * hm1 + w2 * h
    else:
        t = SUBLANES
        nseq = tm // t
        tf = wh_ref.shape[1]
        h = _dot(xcat_ref[FFN_HALO:, :], wh_ref[...]).reshape(nseq, t, tf)
        r = lax.broadcasted_iota(jnp.int32, (nseq, t, tf), 1)
        b0 = st_ref[:, 0:1, :]
        b1 = st_ref[:, 1:2, :]
        hm1 = jnp.where(r >= 1, pltpu.roll(h, 1, 1), b1)
        hm2 = jnp.where(r >= 2, pltpu.roll(h, 2, 1), jnp.where(r == 1, b1, b0))
        hl_ref[...] = h[:, t - (CONV_W - 1):, :]
        conv = (cb_ref[...] + w0 * hm2 + w1 * hm1 + w2 * h).reshape(tm, tf)
    gate = _dot(xcat_ref[FFN_HALO:, :], wg_ref[...])
    a = (_gelu_tanh(conv) * gate).astype(BF16)
    acc_ref[...] += _dot(a, wd_ref[...])

    @pl.when(f == pl.num_programs(1) - 1)
    def _():
        y = _layer_norm(alpha * x_ref[...] + acc_ref[...], g_ref[...], b_ref[...])
        o_ref[...] = y
        ob_ref[...] = y.astype(ob_ref.dtype)


def conv_ffn(x, w_up, w_down, cw, cb, g, b, l, state, ls, alpha, nb, t, chained, tm, tf):
    n, d = x.shape
    ff = w_down.shape[1]
    nf = ff // tf
    lay = lambda shape: pl.BlockSpec((None,) + shape, lambda i, f: (l, 0, 0))
    hb = tm // FFN_HALO
    if chained:
        tiles_per_seq = t // tm
        st = jnp.zeros((1, 1, CONV_W - 1, tf), F32)
        st_spec = pl.BlockSpec((None, 1, CONV_W - 1, tf), lambda i, f: (0, 0, 0, 0))
        hl_shape = jax.ShapeDtypeStruct((n // tm, SUBLANES, ff), F32)
        hl_spec = pl.BlockSpec((1, SUBLANES, tf), lambda i, f: (i, 0, f))
    else:
        assert t == SUBLANES
        tiles_per_seq = 1
        st = state
        st_spec = pl.BlockSpec((None, tm // t, CONV_W - 1, tf), lambda i, f: (ls, i, 0, f))
        hl_shape = jax.ShapeDtypeStruct((nb, CONV_W - 1, ff), F32)
        hl_spec = pl.BlockSpec((tm // t, CONV_W - 1, tf), lambda i, f: (i, 0, f))
    o, ob, hl = pl.pallas_call(
        functools.partial(_ffn_kernel, tm=tm, alpha=alpha, chained=chained, tiles_per_seq=tiles_per_seq),
        grid=(n // tm, nf),
        in_specs=[
            pl.BlockSpec((tm, d), lambda i, f: (i, 0)),
            pl.BlockSpec((FFN_HALO, d), lambda i, f: (jnp.maximum(i * hb - 1, 0), 0)),
            pl.BlockSpec((None, d, tf), lambda i, f: (l, 0, f)),
            pl.BlockSpec((None, d, tf), lambda i, f: (l, 0, nf + f)),
            pl.BlockSpec((None, tf, d), lambda i, f: (l, f, 0)),
            pl.BlockSpec((None, CONV_W, tf), lambda i, f: (l, 0, f)),
            pl.BlockSpec((None, 1, tf), lambda i, f: (l, 0, f)),
            st_spec, lay((1, d)), lay((1, d)),
        ],
        out_specs=[pl.BlockSpec((tm, d), lambda i, f: (i, 0)), pl.BlockSpec((tm, d), lambda i, f: (i, 0)), hl_spec],
        out_shape=[jax.ShapeDtypeStruct((n, d), F32), jax.ShapeDtypeStruct((n, d), BF16), hl_shape],
        scratch_shapes=[pltpu.VMEM((tm + FFN_HALO, d), BF16), pltpu.VMEM((tm + FFN_HALO, tf), F32),
                        pltpu.VMEM((tm, d), F32)],
        compiler_params=_cp(("parallel", "arbitrary")),
        name="conv_ffn",
    )(x, x, w_up, w_up, w_down, cw, cb, st, g, b)
    if chained:
        hl = hl[tiles_per_seq - 1::tiles_per_seq, SUBLANES - (CONV_W - 1):, :]
    return o, ob, hl


def _rope_tables(pos0, t):
    half = MLA_ROPE // 2
    inv = ROPE_THETA ** (-jnp.arange(half, dtype=F32) / half)
    ang = (pos0 + jnp.arange(t)).astype(F32)[:, None] * inv
    cos, sin = jnp.cos(ang), jnp.sin(ang)
    cosf = jnp.concatenate([cos, cos], -1)
    sinf = jnp.concatenate([-sin, sin], -1)
    return jnp.tile(cosf, (1, MLA_HEADS)), jnp.tile(sinf, (1, MLA_HEADS))


def _rot_half_cols(w):
    half = w.shape[-1] // 2
    return jnp.concatenate([w[..., half:], w[..., :half]], -1)


def _prep_weights(p):
    depth, d_model, _ = p['w_in'].shape
    hk = GLA_HEADS * GLA_DK
    sizes = (hk, hk, GLA_HEADS * GLA_DV, MIX_W, GLA_GATE_RANK, MIX_W, MIX_W, MLA_Q_RANK, MLA_KV_RANK, MLA_ROPE)
    offs = [0]
    for s in sizes:
        offs.append(offs[-1] + s)
    w_in = p['w_in']
    seg = lambda i: w_in[:, :, offs[i]:offs[i + 1]].astype(BF16)
    gq, gk, gv, gog, ga, su, pu, mq, mkv, mkr = [seg(i) for i in range(10)]
    pad = jnp.zeros((depth, d_model, Z_COLS - Z_GA - GLA_GATE_RANK), BF16)
    out = dict(
        w_z=jnp.concatenate([gq, gk, gv, gog, su, pu, mq, mkv, mkr, _rot_half_cols(mkr), ga, pad], 2),
        w_gate=w_in[:, :, offs[10]:].astype(BF16),
        b_gate=p['b_gates'].reshape(depth, 1, N_BRANCH * d_model))
    out['gla_wg'] = jnp.zeros((depth, 128, hk), F32).at[:, :GLA_GATE_RANK].set(p['w_gla_gate'])
    out['gla_bg'] = p['b_gla_gate'].reshape(depth, 1, hk)
    out['gla_ng'] = p['gla_norm_g'].reshape(depth, 1, MIX_W)

    g2 = S5_GROUPS // 2
    eye = jnp.eye(g2, dtype=F32)

    def blockdiag_in(bm):
        bm = bm.reshape(depth, 2, g2, S5_STATE, S5_GROUP)
        w = jnp.einsum('lhgnj,gk->lhgjkn', bm, eye)
        return w.reshape(depth, 2, g2 * S5_GROUP, g2 * S5_STATE).astype(BF16)

    def blockdiag_out(cm):
        cm = cm.reshape(depth, 2, g2, S5_GROUP, S5_STATE)
        w = jnp.einsum('lhgjn,gk->lhgnkj', cm, eye)
        return w.reshape(depth, 2, g2 * S5_STATE, g2 * S5_GROUP).astype(BF16)

    out['s5_wbr'], out['s5_wbi'] = blockdiag_in(p['s5_b_re']), blockdiag_in(p['s5_b_im'])
    out['s5_wcr'], out['s5_wci'] = blockdiag_out(p['s5_c_re']), blockdiag_out(p['s5_c_im'])
    out['s5_d'] = p['s5_d'].reshape(depth, 1, MIX_W)
    out['s5_wglu'] = p['w_s5_glu'].astype(BF16)
    out['s5_bglu'] = p['b_s5_glu'].reshape(depth, 1, MIX_W)
    out['pool_w'] = p['w_pool'].astype(BF16)
    out['pool_sc'] = p['pool_scale'].reshape(depth, 1, MIX_W)
    wq = p['w_q_up'].reshape(depth, MLA_Q_RANK, MLA_HEADS, MLA_NOPE + MLA_ROPE)
    flat = lambda w: w.reshape(depth, MLA_Q_RANK, -1)
    wq_rope = wq[..., MLA_NOPE:]
    out['mla_wq'] = jnp.concatenate(
        [flat(wq[..., :MLA_NOPE]), flat(wq_rope), flat(_rot_half_cols(wq_rope))], 2).astype(BF16)
    wkv = p['w_kv_up'].reshape(depth, MLA_KV_RANK, MLA_HEADS, MLA_NOPE + MLA_V)
    out['mla_wuk'] = wkv[..., :MLA_NOPE].transpose(0, 2, 3, 1).astype(BF16)
    out['mla_wuv'] = wkv[..., MLA_NOPE:].transpose(0, 2, 1, 3).astype(BF16)
    out['mla_qg'] = p['mla_q_norm'].reshape(depth, 1, MLA_Q_RANK)
    out['mla_kg'] = p['mla_kv_norm'].reshape(depth, 1, MLA_KV_RANK)
    out['w_branch'] = p['w_branch'].astype(BF16)
    out['w_out'] = p['w_out'].astype(BF16)
    out['w_up'] = p['w_ffn_up'].astype(BF16)
    out['w_down'] = p['w_ffn_down'].astype(BF16)
    out['conv_w'] = p['ffn_conv_w']
    out['conv_b'] = p['ffn_conv_b'].reshape(depth, 1, -1)
    for nm in ('ln1_g', 'ln1_b', 'ln2_g', 'ln2_b'):
        out[nm] = p[nm].reshape(depth, 1, d_model)
    return out


def _layer(x, xb, w, tabs, st, nb, t, pos0, l, ls, cache, alpha, chained):
    n, d = x.shape
    z = matmul(xb, w['w_z'], l, _tile(n, 1024), 256)
    gla_w = (w['gla_wg'], w['gla_bg'], w['gla_ng'], l)
    s5_w = (tabs[0], tabs[1], w['s5_wbr'], w['s5_wbi'], w['s5_wcr'], w['s5_wci'], w['s5_d'], w['s5_wglu'],
            w['s5_bglu'], l)
    cos4, sin4 = _rope_tables(pos0, t)
    mla_w = (w['mla_qg'], w['mla_kg'], w['mla_wq'], w['mla_wuk'], l)
    if chained:
        gl = _tile(t, 64)
        ya, s_gla = gla(z, *gla_w, st['gla'], ls, nb, t, gl, _tile(t // gl, 4), True)
        yb, h_re, h_im = s5(z, *s5_w, st['s5_re'], st['s5_im'], ls, nb, t, True, SUBLANES * S5_NJ_CHAIN)
        yc, pbuf = pool(z, w['pool_w'], w['pool_sc'], l, st['pool'], ls, nb, t, pos0, True, _tile(t, 256))
        qcat, kcat, c_new, kr_new = mla_prep(z, cos4, sin4, *mla_w, _tile(t, 256), BF16)
        yd = mla_causal(qcat, kcat, w['mla_wuv'], l, nb, t, _tile(t, 512))
    else:
        ya, s_gla = gla(z, *gla_w, st['gla'], ls, nb, t, t, _tile(nb, 8), False)
        yb, h_re, h_im = s5(z, *s5_w, st['s5_re'], st['s5_im'], ls, nb, t, False, _tile(n, 256))
        yc, pbuf = pool(z, w['pool_w'], w['pool_sc'], l, st['pool'], ls, nb, t, pos0, False, _tile(n, 128))
        prep_rows = _tile(n, 128)
        tile_rows = lambda a: jnp.tile(a, (prep_rows // t, 1))
        qcat, kcat, c_new, kr_new = mla_prep(z, tile_rows(cos4), tile_rows(sin4), *mla_w, prep_rows, F32)
        yd = mla_paged(qcat, kcat, cache[0], cache[1], cache[2], w['mla_wuv'], l)
    merged = merge(xb, (ya, yb, yc, yd), w['w_gate'], w['b_gate'], w['w_branch'], l, _tile(n, 512), _tile(d, 512))
    x1 = out_ln(merged, w['w_out'], x, w['ln1_g'], w['ln1_b'], l, alpha, _tile(n, 256))
    ff = w['w_down'].shape[1]
    tm = _tile(t, 512) if chained else _tile(n, 512)
    x2, x2b, cbuf = conv_ffn(x1, w['w_up'], w['w_down'], w['conv_w'], w['conv_b'], w['ln2_g'], w['ln2_b'], l,
                             st['conv'], ls, alpha, nb, t, chained, tm, _tile(ff, 512))
    new = (c_new.reshape(nb, t, MLA_KV_RANK), kr_new.reshape(nb, t, MLA_ROPE), s_gla, h_re, h_im, pbuf, cbuf)
    return x2, x2b, new


def _trunk(x3, w, tabs, states, shared_state, pos0, cache, alpha, chained):
    nb, t, d = x3.shape
    depth = w['w_z'].shape[0]
    x = x3.reshape(nb * t, d)
    xb = x.astype(BF16)
    outs = [[] for _ in range(7)]
    for l in range(depth):
        x, xb, new = _layer(x, xb, w, tabs, states, nb, t, pos0, l, 0 if shared_state else l, cache, alpha,
                            chained)
        for lst, val in zip(outs, new):
            lst.append(val)
    return (x.reshape(nb, t, d),) + tuple(jnp.stack(o, 0) for o in outs)


def kernel(x_prompt, x_sample, cache_kv_latent, cache_k_rope, page_table, state_gla, state_s5_re, state_s5_im,
           state_pool, state_ffn_conv, ln1_g, ln1_b, w_in, b_gates, w_gla_gate, b_gla_gate, gla_norm_g,
           s5_a_re, s5_a_im, s5_log_dt, s5_b_re, s5_b_im, s5_c_re, s5_c_im, s5_d, w_s5_glu, b_s5_glu,
           w_pool, pool_scale, mla_q_norm, w_q_up, mla_kv_norm, w_kv_up, w_branch, w_out, ln2_g, ln2_b,
           w_ffn_up, ffn_conv_w, ffn_conv_b, w_ffn_down):
    params = {'ln1_g': ln1_g, 'ln1_b': ln1_b, 'w_in': w_in, 'b_gates': b_gates, 'w_gla_gate': w_gla_gate,
              'b_gla_gate': b_gla_gate, 'gla_norm_g': gla_norm_g, 's5_b_re': s5_b_re, 's5_b_im': s5_b_im,
              's5_c_re': s5_c_re, 's5_c_im': s5_c_im, 's5_d': s5_d, 'w_s5_glu': w_s5_glu, 'b_s5_glu': b_s5_glu,
              'w_pool': w_pool, 'pool_scale': pool_scale, 'mla_q_norm': mla_q_norm, 'w_q_up': w_q_up,
              'mla_kv_norm': mla_kv_norm, 'w_kv_up': w_kv_up, 'w_branch': w_branch, 'w_out': w_out,
              'ln2_g': ln2_g, 'ln2_b': ln2_b, 'w_ffn_up': w_ffn_up, 'ffn_conv_w': ffn_conv_w,
              'ffn_conv_b': ffn_conv_b, 'w_ffn_down': w_ffn_down}
    depth = w_in.shape[0]
    d_ff = ffn_conv_w.shape[-1]
    alpha = (2 * depth) ** 0.25
    w = _prep_weights(params)
    tabs = s5_tables(s5_a_re, s5_a_im, s5_log_dt, S5_NJ_CHAIN)
    pad_pool = lambda a: jnp.pad(a, ((0, 0), (0, 0), (1, 0), (0, 0)))

    nb_p = x_prompt.shape[0]
    zeros = lambda *shape: jnp.zeros((1, nb_p) + shape, F32)
    st_p = dict(gla=zeros(GLA_HEADS, GLA_DK, GLA_DV), s5_re=zeros(S5_W), s5_im=zeros(S5_W),
                pool=zeros(HALO, MIX_W), conv=None)
    res_p = _trunk(x_prompt, w, tabs, st_p, True, 0, None, alpha, True)

    nb_s = x_sample.shape[0]
    past_len = page_table.shape[1] * PAGE_SIZE
    st_s = dict(gla=state_gla, s5_re=state_s5_re.reshape(depth, nb_s, S5_W),
                s5_im=state_s5_im.reshape(depth, nb_s, S5_W), pool=pad_pool(state_pool), conv=state_ffn_conv)
    cache_krt = jnp.swapaxes(cache_k_rope, 2, 3)
    res_s = _trunk(x_sample, w, tabs, st_s, False, past_len, (cache_kv_latent, cache_krt, page_table), alpha, False)
    return (res_p[0], res_s[0]) + res_p[1:] + res_s[1:]
```

```python
import functools
import math

import jax
import jax.numpy as jnp
from jax import lax
from jax.experimental import pallas as pl
from jax.experimental.pallas import tpu as pltpu

F32 = jnp.float32
BF16 = jnp.bfloat16
HIGHEST = lax.Precision.HIGHEST

MIX_W = 512
N_BRANCH = 4
GLA_HEADS, GLA_DK, GLA_DV, GLA_GATE_RANK, GLA_TAU = 4, 64, 128, 16, 16.0
S5_GROUP, S5_GROUPS, S5_STATE = 16, 32, 64
S5_W = S5_GROUPS * S5_STATE
POOL_WINDOWS = (2, 4, 8, 16)
POOL_GROUP = 128
POOL_BUF = 15
MLA_HEADS, MLA_NOPE, MLA_ROPE, MLA_V, MLA_Q_RANK, MLA_KV_RANK = 4, 128, 64, 128, 384, 128
MLA_SCALE = (MLA_NOPE + MLA_ROPE) ** -0.5
MLA_QK = 256
ROPE_THETA = 10000.0
PAGE_SIZE = 128
CONV_W = 3
LN_EPS = 1e-5
RMS_EPS = 1e-6

Z_Q, Z_K, Z_V, Z_OG, Z_SU, Z_PU, Z_MQ, Z_MKV, Z_MKR, Z_MKR_ROT, Z_GA = (
    0, 256, 512, 1024, 1536, 2048, 2560, 2944, 3072, 3136, 3200)
Z_COLS = 3328

V7X_VMEM_BYTES = 64 * 1024 * 1024
VMEM_LIMIT = 48 * 1024 * 1024
SUBLANES = 8


def _cp(sem, vmem=VMEM_LIMIT):
    return pltpu.CompilerParams(dimension_semantics=sem, vmem_limit_bytes=vmem)


def _tile(dim, pref):
    t = min(dim, pref)
    while dim % t:
        t //= 2
    return t


def _sigmoid(x):
    return 1.0 / (1.0 + jnp.exp(-x))


def _gelu_tanh(x):
    c = math.sqrt(2.0 / math.pi)
    return 0.5 * x * (1.0 + jnp.tanh(c * (x + 0.044715 * (x * x * x))))


def _log_sigmoid(x):
    return jnp.minimum(x, 0.0) - jnp.log(1.0 + jnp.exp(-jnp.abs(x)))


def _layer_norm(y, g, b):
    mu = jnp.mean(y, -1, keepdims=True)
    d = y - mu
    var = jnp.mean(d * d, -1, keepdims=True)
    return d * lax.rsqrt(var + LN_EPS) * g + b


def _rms(x, g):
    return x * lax.rsqrt(jnp.mean(x * x, -1, keepdims=True) + RMS_EPS) * g


def _dot(a, b):
    return jnp.dot(a, b, preferred_element_type=F32)


def _dot_t(a, b):
    return lax.dot_general(a, b, (((1,), (1,)), ((), ())), preferred_element_type=F32)


def _dot_ta(a, b, **kw):
    return lax.dot_general(a, b, (((0,), (0,)), ((), ())), preferred_element_type=F32, **kw)


def _cmul(ar, ai, br, bi):
    return ar * br - ai * bi, ar * bi + ai * br


def _mm_kernel(x_ref, w_ref, o_ref):
    o_ref[...] = _dot(x_ref[...], w_ref[...]).astype(o_ref.dtype)


def matmul(x, w, l, tm, tn, out_dtype=F32):
    m, k = x.shape
    n = w.shape[2]
    return pl.pallas_call(
        _mm_kernel,
        grid=(m // tm, n // tn),
        in_specs=[pl.BlockSpec((tm, k), lambda i, j: (i, 0)),
                  pl.BlockSpec((None, k, tn), lambda i, j: (l, 0, j))],
        out_specs=pl.BlockSpec((tm, tn), lambda i, j: (i, j)),
        out_shape=jax.ShapeDtypeStruct((m, n), out_dtype),
        compiler_params=_cp(("parallel", "parallel")),
        name="in_proj",
    )(x, w)


def _gla_kernel(q_ref, k_ref, v_ref, og_ref, ga_ref, wg_ref, bg_ref, ng_ref, s0_ref,
                ya_ref, st_ref, *, L, nsub, chain):
    R = nsub * L

    @pl.when(pl.program_id(1) == 0)
    def _():
        st_ref[...] = s0_ref[...]

    pre = jnp.dot(ga_ref[...], wg_ref[...], precision=HIGHEST, preferred_element_type=F32) + bg_ref[...]
    log_a = _log_sigmoid(pre) * (1.0 / GLA_TAU)
    hk = GLA_HEADS * GLA_DK
    blk_of = lambda x: sum((x >= j * L).astype(jnp.int32) for j in range(1, nsub))
    row = lax.broadcasted_iota(jnp.int32, (R, R), 0)
    col = lax.broadcasted_iota(jnp.int32, (R, R), 1)
    tri = jnp.where(row >= col, 1.0, 0.0)
    if nsub > 1:
        tri = jnp.where(blk_of(row) == blk_of(col), tri, 0.0)
    causal = tri > 0.5
    a_hi = log_a.astype(BF16)
    rem = log_a - a_hi.astype(F32)
    a_mid = rem.astype(BF16)
    a_lo = (rem - a_mid.astype(F32)).astype(BF16)
    a3 = jnp.concatenate([a_hi, a_mid, a_lo], 1)
    sum3 = lambda m, ax: (lax.slice_in_dim(m, 0, hk, axis=ax) + lax.slice_in_dim(m, hk, 2 * hk, axis=ax)
                          + lax.slice_in_dim(m, 2 * hk, 3 * hk, axis=ax))
    bc = sum3(_dot(tri.astype(BF16), a3), 1)
    if nsub > 1:
        rowb = lax.broadcasted_iota(jnp.int32, (R, nsub * GLA_DV), 0)
        colb = lax.broadcasted_iota(jnp.int32, (R, nsub * GLA_DV), 1)
        blk_col = sum((colb >= j * GLA_DV).astype(jnp.int32) for j in range(1, nsub))
        member = jnp.where(blk_of(rowb) == blk_col, 1.0, 0.0).astype(BF16)
    else:
        member = jnp.ones((R, GLA_DV), BF16)
    a_tot_all = jnp.exp(sum3(_dot_ta(a3, member), 0))
    tot = jnp.concatenate([jnp.broadcast_to(bc[(j + 1) * L - 1:(j + 1) * L, :], (L, hk)) for j in range(nsub)], 0)
    q = q_ref[...] * (GLA_DK ** -0.5)
    k = k_ref[...]
    q_in = q * jnp.exp(bc)
    k_in = k * jnp.exp(-bc)
    k_out = k * jnp.exp(tot - bc)
    for h in range(GLA_HEADS):
        ks = slice(h * GLA_DK, (h + 1) * GLA_DK)
        vs = slice(h * GLA_DV, (h + 1) * GLA_DV)
        v = v_ref[:, vs]
        scores = jnp.where(causal, _dot_t(q_in[:, ks].astype(BF16), k_in[:, ks].astype(BF16)), 0.0)
        o = _dot(scores.astype(BF16), v.astype(BF16))
        o_inter = []
        state = st_ref[0, h]
        for j in range(nsub):
            rs = slice(j * L, (j + 1) * L)
            if not chain:
                state = st_ref[j, h]
            o_inter.append(_dot(q_in[rs, ks].astype(BF16), state.astype(BF16)))
            a_tot = a_tot_all[ks, j * GLA_DV:(j + 1) * GLA_DV]
            state = a_tot * state + _dot_ta(k_out[rs, ks].astype(BF16), v[rs].astype(BF16))
            if not chain:
                st_ref[j, h] = state
        if chain:
            st_ref[0, h] = state
        o = o + (o_inter[0] if nsub == 1 else jnp.concatenate(o_inter, 0))
        o = o * lax.rsqrt(jnp.mean(o * o, -1, keepdims=True) + RMS_EPS) * ng_ref[:, vs]
        g = og_ref[:, vs]
        ya_ref[:, vs] = (o * (g * _sigmoid(g))).astype(ya_ref.dtype)


def gla(z, wg_pad, bg, ng, l, s0, ls, nb, t, L, nsub, chain):
    R = nsub * L
    if chain:
        nc = t // R
        grid = (nb, nc)
        nst = 1
    else:
        assert L == t
        nc = 1
        grid = (nb // nsub, 1)
        nst = nsub
    rb = lambda b, c: b * nc + c
    hk = GLA_HEADS * GLA_DK
    st_blk = (nst, GLA_HEADS, GLA_DK, GLA_DV)
    return pl.pallas_call(
        functools.partial(_gla_kernel, L=L, nsub=nsub, chain=chain),
        grid=grid,
        in_specs=[
            pl.BlockSpec((R, hk), lambda b, c: (rb(b, c), Z_Q // hk)),
            pl.BlockSpec((R, hk), lambda b, c: (rb(b, c), Z_K // hk)),
            pl.BlockSpec((R, MIX_W), lambda b, c: (rb(b, c), Z_V // MIX_W)),
            pl.BlockSpec((R, MIX_W), lambda b, c: (rb(b, c), Z_OG // MIX_W)),
            pl.BlockSpec((R, 128), lambda b, c: (rb(b, c), Z_GA // 128)),
            pl.BlockSpec((None, 128, hk), lambda b, c: (l, 0, 0)),
            pl.BlockSpec((None, 1, hk), lambda b, c: (l, 0, 0)),
            pl.BlockSpec((None, 1, MIX_W), lambda b, c: (l, 0, 0)),
            pl.BlockSpec((None,) + st_blk, lambda b, c: (ls, b, 0, 0, 0)),
        ],
        out_specs=[pl.BlockSpec((R, MIX_W), lambda b, c: (rb(b, c), 0)),
                   pl.BlockSpec(st_blk, lambda b, c: (b, 0, 0, 0))],
        out_shape=[jax.ShapeDtypeStruct((nb * t, MIX_W), BF16),
                   jax.ShapeDtypeStruct((nb, GLA_HEADS, GLA_DK, GLA_DV), F32)],
        compiler_params=_cp(("parallel", "arbitrary")),
        name="gla",
    )(z, z, z, z, z, wg_pad, bg, ng, s0)


S5_NJ_CHAIN = 32
TAB_A, TAB_T1, TAB_T2, TAB_T4, TAB_F = 0, 2, 4, 6, 8
N_TAB = 10
S5_LANES = 512


def _s5_tab_kernel(are_ref, aim_ref, ldt_ref, tab_ref, pj_ref, *, nj):
    lr = are_ref[0]
    li = aim_ref[0]
    dt = jnp.exp(ldt_ref[0])
    mag = jnp.exp(lr * dt)
    a_re = mag * jnp.cos(li * dt)
    a_im = mag * jnp.sin(li * dt)
    shp = (SUBLANES, S5_W)
    r = lax.broadcasted_iota(jnp.int32, shp, 0)
    tab_ref[0, TAB_A] = jnp.broadcast_to(a_re, shp)
    tab_ref[0, TAB_A + 1] = jnp.broadcast_to(a_im, shp)
    pr, pi = a_re, a_im
    for j in range(nj):
        pj_ref[0, 0, j * SUBLANES:(j + 1) * SUBLANES, :] = jnp.broadcast_to(pr, shp)
        pj_ref[0, 1, j * SUBLANES:(j + 1) * SUBLANES, :] = jnp.broadcast_to(pi, shp)
        if j + 1 < nj:
            pr, pi = _cmul(pr, pi, a_re, a_im)
    br, bi = pr, pi
    for s, idx in ((1, TAB_T1), (2, TAB_T2), (4, TAB_T4)):
        tab_ref[0, idx] = jnp.where(r >= s, br, 0.0)
        tab_ref[0, idx + 1] = jnp.where(r >= s, bi, 0.0)
        br, bi = _cmul(br, bi, br, bi)
    den = lr * lr + li * li
    nr, ni = a_re - 1.0, a_im
    tab_ref[0, TAB_F] = jnp.broadcast_to((nr * lr + ni * li) / den, shp)
    tab_ref[0, TAB_F + 1] = jnp.broadcast_to((ni * lr - nr * li) / den, shp)


def s5_tables(a_re, a_im, log_dt, nj):
    depth = a_re.shape[0]
    flat = lambda a: a.reshape(depth, 1, S5_W)
    ldt = jnp.broadcast_to(log_dt[:, :, None], (depth, S5_GROUPS, S5_STATE))
    spec = pl.BlockSpec((1, 1, S5_W), lambda l: (l, 0, 0))
    return pl.pallas_call(
        functools.partial(_s5_tab_kernel, nj=nj),
        grid=(depth,),
        in_specs=[spec, spec, spec],
        out_specs=[pl.BlockSpec((1, N_TAB, SUBLANES, S5_W), lambda l: (l, 0, 0, 0)),
                   pl.BlockSpec((1, 2, nj * SUBLANES, S5_W), lambda l: (l, 0, 0, 0))],
        out_shape=[jax.ShapeDtypeStruct((depth, N_TAB, SUBLANES, S5_W), F32),
                   jax.ShapeDtypeStruct((depth, 2, nj * SUBLANES, S5_W), F32)],
        compiler_params=_cp(("parallel",)),
        name="s5_tables",
    )(flat(a_re), flat(a_im), flat(ldt))


def _s5_kernel(u0_ref, u1_ref, u2_ref, u3_ref, tab_ref, pj_ref, wbr_ref, wbi_ref, wcr_ref, wci_ref, d_ref,
               wglu_ref, bglu_ref, h0r_ref, h0i_ref, yb_ref, hr_out_ref, hi_out_ref,
               up_ref, xr_ref, xi_ref, cr_ref, ci_ref, *, R, nj, chained):
    half_u = MIX_W // 2
    half_s = S5_W // 2
    grp = SUBLANES * nj
    ngrp = R // grp
    ncol = MIX_W // 128
    for c, u_ref in enumerate((u0_ref, u1_ref, u2_ref, u3_ref)):
        for g in range(ngrp):
            for j in range(nj):
                up_ref[c, g * grp + j * SUBLANES:g * grp + (j + 1) * SUBLANES, :] = (
                    u_ref[pl.ds(g * grp + j, SUBLANES, stride=nj), :])
    u = jnp.concatenate([up_ref[c] for c in range(ncol)], 1)
    ub = u.astype(BF16)
    fr = tab_ref[TAB_F][0:1]
    fi = tab_ref[TAB_F + 1][0:1]
    for hf in range(2):
        us = ub[:, hf * half_u:(hf + 1) * half_u]
        ss = slice(hf * half_s, (hf + 1) * half_s)
        bur = _dot(us, wbr_ref[hf])
        bui = _dot(us, wbi_ref[hf])
        xr_ref[:, ss] = fr[:, ss] * bur - fi[:, ss] * bui
        xi_ref[:, ss] = fr[:, ss] * bui + fi[:, ss] * bur

    if chained:
        @pl.when(pl.program_id(1) == 0)
        def _():
            cr_ref[...] = h0r_ref[0]
            ci_ref[...] = h0i_ref[0]

    row8 = lax.broadcasted_iota(jnp.int32, (SUBLANES, S5_LANES), 0)
    for lc in range(S5_W // S5_LANES):
        ls = slice(lc * S5_LANES, (lc + 1) * S5_LANES)
        ar = tab_ref[TAB_A, :, ls]
        ai = tab_ref[TAB_A + 1, :, ls]
        for g in range(ngrp):
            rows = lambda j: slice(g * grp + j * SUBLANES, g * grp + (j + 1) * SUBLANES)
            if chained:
                hr = jnp.zeros((SUBLANES, S5_LANES), F32)
                hi = jnp.zeros((SUBLANES, S5_LANES), F32)
            else:
                hr = h0r_ref[g * SUBLANES:(g + 1) * SUBLANES, ls]
                hi = h0i_ref[g * SUBLANES:(g + 1) * SUBLANES, ls]
            for j in range(nj):
                pr, pi = _cmul(ar, ai, hr, hi)
                hr = pr + xr_ref[rows(j), ls]
                hi = pi + xi_ref[rows(j), ls]
                xr_ref[rows(j), ls] = hr
                xi_ref[rows(j), ls] = hi
            if not chained:
                hr_out_ref[g * SUBLANES:(g + 1) * SUBLANES, ls] = hr
                hi_out_ref[g * SUBLANES:(g + 1) * SUBLANES, ls] = hi
                continue
            yr = jnp.where(row8 == 0, cr_ref[:, ls], pltpu.roll(hr, 1, 0))
            yi = jnp.where(row8 == 0, ci_ref[:, ls], pltpu.roll(hi, 1, 0))
            for s, idx in ((1, TAB_T1), (2, TAB_T2), (4, TAB_T4)):
                pr, pi = _cmul(tab_ref[idx, :, ls], tab_ref[idx + 1, :, ls],
                               pltpu.roll(yr, s, 0), pltpu.roll(yi, s, 0))
                yr, yi = yr + pr, yi + pi
            for j in range(nj):
                pr, pi = _cmul(pj_ref[0, rows(j), ls], pj_ref[1, rows(j), ls], yr, yi)
                hr = xr_ref[rows(j), ls] + pr
                hi = xi_ref[rows(j), ls] + pi
                xr_ref[rows(j), ls] = hr
                xi_ref[rows(j), ls] = hi
            cr_ref[:, ls] = hr[SUBLANES - 1:SUBLANES]
            ci_ref[:, ls] = hi[SUBLANES - 1:SUBLANES]

    if chained:
        @pl.when(pl.program_id(1) == pl.num_programs(1) - 1)
        def _():
            hr_out_ref[0] = cr_ref[...]
            hi_out_ref[0] = ci_ref[...]

    ys = []
    for hf in range(2):
        ss = slice(hf * half_s, (hf + 1) * half_s)
        y = _dot(xr_ref[:, ss].astype(BF16), wcr_ref[hf]) - _dot(xi_ref[:, ss].astype(BF16), wci_ref[hf])
        us = slice(hf * half_u, (hf + 1) * half_u)
        ys.append(_gelu_tanh(y + d_ref[:, us] * u[:, us]))
    y = jnp.concatenate(ys, 1)
    y = y * _sigmoid(_dot(y.astype(BF16), wglu_ref[...]) + bglu_ref[...])
    for c in range(ncol):
        up_ref[c] = y[:, c * 128:(c + 1) * 128]
        for g in range(ngrp):
            for r in range(SUBLANES):
                yb_ref[g * grp + r * nj:g * grp + (r + 1) * nj, c * 128:(c + 1) * 128] = (
                    up_ref[c, pl.ds(g * grp + r, nj, stride=SUBLANES), :].astype(yb_ref.dtype))


def s5(z, tab, pj, wbr, wbi, wcr, wci, d, wglu, bglu, l, h0r, h0i, ls, nb, t, chained, R):
    depth_s = h0r.shape[0]
    if chained:
        nj = S5_NJ_CHAIN
        assert R == SUBLANES * nj
        nc = t // R
        grid = (nb, nc)
        rb = lambda b, c: b * nc + c
        h0 = (h0r.reshape(depth_s, nb, 1, S5_W), h0i.reshape(depth_s, nb, 1, S5_W))
        st_in = pl.BlockSpec((None, 1, 1, S5_W), lambda b, c: (ls, b, 0, 0))
        st_out = pl.BlockSpec((1, 1, S5_W), lambda b, c: (b, 0, 0))
        st_shape = jax.ShapeDtypeStruct((nb, 1, S5_W), F32)
        out_dtype = BF16
    else:
        assert t == SUBLANES
        nj = SUBLANES
        nseq = R // SUBLANES
        grid = (nb // nseq, 1)
        rb = lambda b, c: b
        h0 = (h0r, h0i)
        st_in = pl.BlockSpec((None, nseq, S5_W), lambda b, c: (ls, b, 0))
        st_out = pl.BlockSpec((nseq, S5_W), lambda b, c: (b, 0))
        st_shape = jax.ShapeDtypeStruct((nb, S5_W), F32)
        out_dtype = F32
    lay = lambda shape: pl.BlockSpec((None,) + shape, lambda b, c: (l,) + (0,) * len(shape))
    yb, hr, hi = pl.pallas_call(
        functools.partial(_s5_kernel, R=R, nj=nj, chained=chained),
        grid=grid,
        in_specs=[pl.BlockSpec((R, 128), lambda b, c, k=k: (rb(b, c), Z_SU // 128 + k))
                  for k in range(MIX_W // 128)] + [
            lay((N_TAB, SUBLANES, S5_W)),
            lay((2, pj.shape[2], S5_W)),
            lay((2, MIX_W // 2, S5_W // 2)), lay((2, MIX_W // 2, S5_W // 2)),
            lay((2, S5_W // 2, MIX_W // 2)), lay((2, S5_W // 2, MIX_W // 2)),
            lay((1, MIX_W)), lay((MIX_W, MIX_W)), lay((1, MIX_W)),
            st_in, st_in,
        ],
        out_specs=[pl.BlockSpec((R, MIX_W), lambda b, c: (rb(b, c), 0)), st_out, st_out],
        out_shape=[jax.ShapeDtypeStruct((nb * t, MIX_W), out_dtype), st_shape, st_shape],
        scratch_shapes=[pltpu.VMEM((MIX_W // 128, R, 128), F32), pltpu.VMEM((R, S5_W), F32),
                        pltpu.VMEM((R, S5_W), F32), pltpu.VMEM((1, S5_W), F32), pltpu.VMEM((1, S5_W), F32)],
        compiler_params=_cp(("parallel", "arbitrary")),
        name="s5",
    )(z, z, z, z, tab, pj, wbr, wbi, wcr, wci, d, wglu, bglu, *h0)
    return yb, hr.reshape(nb, S5_GROUPS, S5_STATE), hi.reshape(nb, S5_GROUPS, S5_STATE)


HALO = 16


def _pool_windows(ext_ref, base, n, pos, wp_ref, sc_ref, out_ref, out_rows):
    for g, w in enumerate(POOL_WINDOWS):
        cs = slice(g * POOL_GROUP, (g + 1) * POOL_GROUP)
        u = ext_ref[pl.ds(base, n), cs]
        s = u
        for j in range(1, w):
            s = s + ext_ref[pl.ds(base - j, n), cs]
        cnt = jnp.minimum(pos + 1, w).astype(F32)
        dlt = s / cnt - u
        y = _dot(dlt.astype(BF16), wp_ref[g]) * sc_ref[:, cs]
        out_ref[out_rows, cs] = y.astype(out_ref.dtype)


def _pool_chain_kernel(u_ref, wp_ref, sc_ref, pre_ref, yc_ref, buf_ref, ext_ref, *, L, pos0):
    c = pl.program_id(1)

    @pl.when(c == 0)
    def _():
        ext_ref[0:HALO, :] = pre_ref[0]

    ext_ref[HALO:HALO + L, :] = u_ref[...]
    pos = pos0 + c * L + lax.broadcasted_iota(jnp.int32, (L, 1), 0)
    _pool_windows(ext_ref, HALO, L, pos, wp_ref, sc_ref, yc_ref, slice(None))
    tail = ext_ref[L:L + HALO, :]
    ext_ref[0:HALO, :] = tail

    @pl.when(c == pl.num_programs(1) - 1)
    def _():
        buf_ref[0] = tail


def _pool_seq_kernel(u_ref, wp_ref, sc_ref, pre_ref, yc_ref, buf_ref, ext_ref, *, nseq, pos0):
    t = SUBLANES
    pos = pos0 + lax.broadcasted_iota(jnp.int32, (t, 1), 0)
    for j in range(nseq):
        ext_ref[0:HALO, :] = pre_ref[j]
        ext_ref[HALO:HALO + t, :] = u_ref[j * t:(j + 1) * t, :]
        _pool_windows(ext_ref, HALO, t, pos, wp_ref, sc_ref, yc_ref, slice(j * t, (j + 1) * t))
        buf_ref[j] = ext_ref[t:t + HALO, :]


def pool(z, wp, sc, l, pre, ls, nb, t, pos0, chained, L):
    lay = lambda shape: pl.BlockSpec((None,) + shape, lambda b, c: (l,) + (0,) * len(shape))
    if chained:
        nc = t // L
        grid = (nb, nc)
        kern = functools.partial(_pool_chain_kernel, L=L, pos0=pos0)
        u_spec = pl.BlockSpec((L, MIX_W), lambda b, c: (b * nc + c, Z_PU // MIX_W))
        y_spec = pl.BlockSpec((L, MIX_W), lambda b, c: (b * nc + c, 0))
        nst = 1
        ext_rows = HALO + L
    else:
        assert t == SUBLANES
        nst = L // t
        grid = (nb // nst, 1)
        kern = functools.partial(_pool_seq_kernel, nseq=nst, pos0=pos0)
        u_spec = pl.BlockSpec((L, MIX_W), lambda b, c: (b, Z_PU // MIX_W))
        y_spec = pl.BlockSpec((L, MIX_W), lambda b, c: (b, 0))
        ext_rows = HALO + t
    yc, buf = pl.pallas_call(
        kern,
        grid=grid,
        in_specs=[u_spec, lay((len(POOL_WINDOWS), POOL_GROUP, POOL_GROUP)), lay((1, MIX_W)),
                  pl.BlockSpec((None, nst, HALO, MIX_W), lambda b, c: (ls, b, 0, 0))],
        out_specs=[y_spec, pl.BlockSpec((nst, HALO, MIX_W), lambda b, c: (b, 0, 0))],
        out_shape=[jax.ShapeDtypeStruct((nb * t, MIX_W), BF16),
                   jax.ShapeDtypeStruct((nb, HALO, MIX_W), F32)],
        scratch_shapes=[pltpu.VMEM((ext_rows, MIX_W), F32)],
        compiler_params=_cp(("parallel", "arbitrary")),
        name="pool",
    )(z, wp, sc, pre)
    return yc, buf[:, 1:, :]


def _mla_prep_kernel(zq_ref, zr_ref, cq_ref, sq_ref, qg_ref, kg_ref, wq_ref, wuk_ref,
                     qcat_ref, kcat_ref, c_ref, kr_ref):
    nq = MLA_HEADS * MLA_NOPE
    nr = MLA_HEADS * MLA_ROPE
    mq = zq_ref[:, 0:MLA_Q_RANK]
    mkv = zq_ref[:, MLA_Q_RANK:MLA_Q_RANK + MLA_KV_RANK]
    qd = _dot(_rms(mq, qg_ref[...]).astype(BF16), wq_ref[...])
    cos4 = cq_ref[...]
    sin4 = sq_ref[...]
    q_rope = (qd[:, nq:nq + nr] * cos4 + qd[:, nq + nr:nq + 2 * nr] * sin4) * MLA_SCALE
    zpad = jnp.zeros((qd.shape[0], MLA_QK - MLA_KV_RANK - MLA_ROPE), qcat_ref.dtype)
    for h in range(MLA_HEADS):
        q_lat = _dot(qd[:, h * MLA_NOPE:(h + 1) * MLA_NOPE].astype(BF16), wuk_ref[h]) * MLA_SCALE
        qcat_ref[h, :, 0:MLA_KV_RANK] = q_lat.astype(qcat_ref.dtype)
        qcat_ref[h, :, MLA_KV_RANK:MLA_KV_RANK + MLA_ROPE] = (
            q_rope[:, h * MLA_ROPE:(h + 1) * MLA_ROPE].astype(qcat_ref.dtype))
        qcat_ref[h, :, MLA_KV_RANK + MLA_ROPE:MLA_QK] = zpad
    c_new = _rms(mkv, kg_ref[...])
    kr_new = (zr_ref[:, 0:MLA_ROPE] * cos4[:, 0:MLA_ROPE]
              + zr_ref[:, MLA_ROPE:2 * MLA_ROPE] * sin4[:, 0:MLA_ROPE])
    c_ref[...] = c_new
    kr_ref[...] = kr_new
    kcat_ref[:, 0:MLA_KV_RANK] = c_new.astype(kcat_ref.dtype)
    kcat_ref[:, MLA_KV_RANK:MLA_KV_RANK + MLA_ROPE] = kr_new.astype(kcat_ref.dtype)
    kcat_ref[:, MLA_KV_RANK + MLA_ROPE:MLA_QK] = zpad


def mla_prep(z, cos4, sin4, qg, kg, wq, wuk, l, L, cat_dtype):
    n = z.shape[0]
    nt = cos4.shape[0] // L
    lay = lambda shape: pl.BlockSpec((None,) + shape, lambda i: (l,) + (0,) * len(shape))
    tab_spec = pl.BlockSpec((L, MLA_HEADS * MLA_ROPE), lambda i: (i % nt, 0))
    return pl.pallas_call(
        _mla_prep_kernel,
        grid=(n // L,),
        in_specs=[
            pl.BlockSpec((L, MIX_W), lambda i: (i, Z_MQ // MIX_W)),
            pl.BlockSpec((L, 128), lambda i: (i, Z_MKR // 128)),
            tab_spec, tab_spec,
            lay((1, MLA_Q_RANK)), lay((1, MLA_KV_RANK)),
            lay((MLA_Q_RANK, MLA_HEADS * (MLA_NOPE + 2 * MLA_ROPE))),
            lay((MLA_HEADS, MLA_NOPE, MLA_KV_RANK)),
        ],
        out_specs=[
            pl.BlockSpec((MLA_HEADS, L, MLA_QK), lambda i: (0, i, 0)),
            pl.BlockSpec((L, MLA_QK), lambda i: (i, 0)),
            pl.BlockSpec((L, MLA_KV_RANK), lambda i: (i, 0)),
            pl.BlockSpec((L, MLA_ROPE), lambda i: (i, 0)),
        ],
        out_shape=[
            jax.ShapeDtypeStruct((MLA_HEADS, n, MLA_QK), cat_dtype),
            jax.ShapeDtypeStruct((n, MLA_QK), cat_dtype),
            jax.ShapeDtypeStruct((n, MLA_KV_RANK), F32),
            jax.ShapeDtypeStruct((n, MLA_ROPE), F32),
        ],
        compiler_params=_cp(("parallel",)),
        name="mla_prep",
    )(z, z, cos4, sin4, qg, kg, wq, wuk)


NEG_BIG = -1e30


def _causal_mask(tq, tk):
    row = lax.broadcasted_iota(jnp.int32, (MLA_HEADS * tq, tk), 0)
    key = lax.broadcasted_iota(jnp.int32, (MLA_HEADS * tq, tk), 1)
    head = sum((row >= h * tq).astype(jnp.int32) for h in range(1, MLA_HEADS))
    return key <= row - head * tq


def _mla_causal_kernel(q_ref, k_ref, wuv_ref, yd_ref, m_ref, l_ref, acc_ref, *, tb):
    qi = pl.program_id(1)
    rows = MLA_HEADS * tb
    m_ref[...] = jnp.full(m_ref.shape, NEG_BIG, F32)
    l_ref[...] = jnp.zeros(l_ref.shape, F32)
    acc_ref[...] = jnp.zeros(acc_ref.shape, F32)
    q = q_ref[...].reshape(rows, MLA_QK)

    def block(kj, masked):
        k = k_ref[pl.ds(pl.multiple_of(kj * tb, tb), tb), :]
        s = _dot_t(q, k)
        if masked:
            s = jnp.where(_causal_mask(tb, tb), s, NEG_BIG)
        m_old = m_ref[...]
        m_new = jnp.maximum(m_old, jnp.max(s, -1, keepdims=True))
        alpha = jnp.exp(m_old - m_new)
        p = jnp.exp(s - jnp.tile(m_new, (1, tb // 128)))
        l_ref[...] = alpha * l_ref[...] + jnp.sum(p, -1, keepdims=True)
        acc_ref[...] = alpha * acc_ref[...] + _dot(p.astype(BF16), k[:, 0:MLA_KV_RANK])
        m_ref[...] = m_new

    def body(kj, carry):
        block(kj, False)
        return carry

    lax.fori_loop(0, qi, body, 0)
    block(qi, True)
    o = acc_ref[...] / l_ref[...]
    for h in range(MLA_HEADS):
        oh = o[h * tb:(h + 1) * tb, :].astype(BF16)
        yd_ref[:, h * MLA_V:(h + 1) * MLA_V] = _dot(oh, wuv_ref[h]).astype(yd_ref.dtype)


def mla_causal(qcat, kcat, wuv, l, nb, t, tb):
    nq = t // tb
    rows = MLA_HEADS * tb
    return pl.pallas_call(
        functools.partial(_mla_causal_kernel, tb=tb),
        grid=(nb, nq),
        in_specs=[
            pl.BlockSpec((MLA_HEADS, tb, MLA_QK), lambda b, i: (0, b * nq + i, 0)),
            pl.BlockSpec((t, MLA_QK), lambda b, i: (b, 0)),
            pl.BlockSpec((None, MLA_HEADS, MLA_KV_RANK, MLA_V), lambda b, i: (l, 0, 0, 0)),
        ],
        out_specs=pl.BlockSpec((tb, MIX_W), lambda b, i: (b * nq + i, 0)),
        out_shape=jax.ShapeDtypeStruct((nb * t, MIX_W), BF16),
        scratch_shapes=[pltpu.VMEM((rows, 128), F32), pltpu.VMEM((rows, 128), F32),
                        pltpu.VMEM((rows, MLA_KV_RANK), F32)],
        compiler_params=_cp(("parallel", "arbitrary")),
        name="mla_causal",
    )(qcat, kcat, wuv)


PAGED_GROUP = 4


def _mla_paged_kernel(pt_ref, q_ref, kn_ref, cache_c_ref, cache_krt_ref, wuv_ref, yd_ref,
                      cbuf_ref, krbuf_ref, sem_ref, cb16_ref, s_ref, *, layer, n_pages, kchunk):
    g = pl.program_id(0)
    ng = pl.num_programs(0)
    t = SUBLANES
    hr = MLA_HEADS * t
    past = n_pages * PAGE_SIZE
    nchunks = past // kchunk
    ppc = kchunk // PAGE_SIZE

    def page_copies(base, slot, p, page_of):
        out = []
        for ab in range(2):
            page = page_of(base + ab, p)
            out.append(pltpu.make_async_copy(
                cache_c_ref.at[layer, page],
                cbuf_ref.at[slot, pl.ds(p * PAGE_SIZE, PAGE_SIZE), pl.ds(ab * MLA_KV_RANK, MLA_KV_RANK)],
                sem_ref.at[0, slot]))
            out.append(pltpu.make_async_copy(
                cache_krt_ref.at[layer, page],
                krbuf_ref.at[slot, pl.ds(ab * MLA_ROPE, MLA_ROPE), pl.ds(p * PAGE_SIZE, PAGE_SIZE)],
                sem_ref.at[1, slot]))
        return out

    table_page = lambda seq, p: pt_ref[seq * n_pages + p]

    def issue(base, slot, chunk):
        for p in range(chunk * ppc, (chunk + 1) * ppc):
            for cp in page_copies(base, slot, p, table_page):
                cp.start()

    def wait_all(slot):
        for p in range(n_pages):
            for cp in page_copies(0, slot, p, lambda seq, p: 0):
                cp.wait()

    @pl.when(g == 0)
    def _():
        for chunk in range(nchunks):
            issue(0, 0, chunk)

    qf = q_ref[...]
    knf = kn_ref[...]
    mask_new = _causal_mask(t, t)
    next_base = jnp.minimum(PAGED_GROUP * (g + 1), PAGED_GROUP * (ng - 1))

    for half in range(2):
        slot = half
        wait_all(slot)
        fill_base, fill_slot = (PAGED_GROUP * g + 2, 1) if half == 0 else (next_base, 0)
        qs = [qf[:, (2 * half + ab) * t:(2 * half + ab + 1) * t, :].reshape(hr, MLA_QK) for ab in range(2)]
        kns = [knf[(2 * half + ab) * t:(2 * half + ab + 1) * t, :].astype(BF16) for ab in range(2)]
        z_lat = jnp.zeros((hr, MLA_KV_RANK), F32)
        z_rope = jnp.zeros((hr, MLA_ROPE), F32)
        rope = slice(MLA_KV_RANK, MLA_KV_RANK + MLA_ROPE)
        q_lat = jnp.concatenate([jnp.concatenate([qs[0][:, 0:MLA_KV_RANK], z_lat], 1),
                                 jnp.concatenate([z_lat, qs[1][:, 0:MLA_KV_RANK]], 1)], 0).astype(BF16)
        q_rope = jnp.concatenate([jnp.concatenate([qs[0][:, rope], z_rope], 1),
                                  jnp.concatenate([z_rope, qs[1][:, rope]], 1)], 0).astype(BF16)
        for chunk in range(nchunks):
            ks = slice(chunk * kchunk, (chunk + 1) * kchunk)
            cb = cbuf_ref[slot, ks, :].astype(BF16)
            cb16_ref[ks, :] = cb
            krb = krbuf_ref[slot, :, ks].astype(BF16)
            s_ref[:, ks] = _dot_t(q_lat, cb) + _dot(q_rope, krb)
            issue(fill_base, fill_slot, chunk)
        s_new = jnp.concatenate(
            [jnp.where(mask_new, _dot_t(qs[ab].astype(BF16), kns[ab]), NEG_BIG) for ab in range(2)], 0)
        s_past = s_ref[...]
        m = jnp.maximum(jnp.max(s_past, -1, keepdims=True), jnp.max(s_new, -1, keepdims=True))
        p_past = jnp.exp(s_past - m)
        p_new = jnp.exp(s_new - m)
        l = jnp.sum(p_past, -1, keepdims=True) + jnp.sum(p_new, -1, keepdims=True)
        o_pair = _dot(p_past.astype(BF16), cb16_ref[...])
        o = []
        for ab in range(2):
            rs = slice(ab * hr, (ab + 1) * hr)
            o_ab = (o_pair[rs, ab * MLA_KV_RANK:(ab + 1) * MLA_KV_RANK]
                    + _dot(p_new[rs].astype(BF16), kns[ab][:, 0:MLA_KV_RANK]))
            o.append(o_ab / l[rs])
        for h in range(MLA_HEADS):
            oh = jnp.concatenate([o[0][h * t:(h + 1) * t], o[1][h * t:(h + 1) * t]], 0).astype(BF16)
            yd_ref[2 * half * t:(2 * half + 2) * t, h * MLA_V:(h + 1) * MLA_V] = (
                _dot(oh, wuv_ref[h]).astype(yd_ref.dtype))

    @pl.when(g == ng - 1)
    def _():
        wait_all(0)


def mla_paged(qcat, kcat, cache_c, cache_krt, page_table, wuv, layer):
    nb, n_pages = page_table.shape
    assert nb % PAGED_GROUP == 0
    t = SUBLANES
    past = n_pages * PAGE_SIZE
    kchunk = _tile(past, 2048)
    rows = PAGED_GROUP * t
    grid_spec = pltpu.PrefetchScalarGridSpec(
        num_scalar_prefetch=1,
        grid=(nb // PAGED_GROUP,),
        in_specs=[
            pl.BlockSpec((MLA_HEADS, rows, MLA_QK), lambda g, pt: (0, g, 0)),
            pl.BlockSpec((rows, MLA_QK), lambda g, pt: (g, 0)),
            pl.BlockSpec(memory_space=pl.ANY),
            pl.BlockSpec(memory_space=pl.ANY),
            pl.BlockSpec((None, MLA_HEADS, MLA_KV_RANK, MLA_V), lambda g, pt: (layer, 0, 0, 0)),
        ],
        out_specs=pl.BlockSpec((rows, MIX_W), lambda g, pt: (g, 0)),
        scratch_shapes=[
            pltpu.VMEM((2, past, 2 * MLA_KV_RANK), F32),
            pltpu.VMEM((2, 2 * MLA_ROPE, past), F32),
            pltpu.SemaphoreType.DMA((2, 2)),
            pltpu.VMEM((past, 2 * MLA_KV_RANK), BF16),
            pltpu.VMEM((2 * MLA_HEADS * t, past), F32),
        ],
    )
    return pl.pallas_call(
        functools.partial(_mla_paged_kernel, layer=layer, n_pages=n_pages, kchunk=kchunk),
        grid_spec=grid_spec,
        out_shape=jax.ShapeDtypeStruct((nb * t, MIX_W), F32),
        compiler_params=_cp(("arbitrary",)),
        name="mla_paged",
    )(page_table.reshape(-1), qcat, kcat, cache_c, cache_krt, wuv)


def _merge_kernel(x_ref, *refs):
    brs, wgs, bgs = refs[0:N_BRANCH], refs[N_BRANCH:2 * N_BRANCH], refs[2 * N_BRANCH:3 * N_BRANCH]
    wb_ref, o_ref = refs[3 * N_BRANCH:]
    x = x_ref[...]
    acc = None
    for g in range(N_BRANCH):
        gate = _sigmoid(_dot(x, wgs[g][...]) + bgs[g][...])
        term = _dot(brs[g][...].astype(BF16), wb_ref[g]) * gate
        acc = term if acc is None else acc + term
    o_ref[...] = acc.astype(o_ref.dtype)


def merge(xb, ys, wg, bg, wb, l, tm, tn):
    n, d = xb.shape
    nj = d // tn
    row = lambda shape: pl.BlockSpec(shape, lambda j, i: (i, 0))
    gate_cols = lambda rows: [pl.BlockSpec((None, rows, tn), lambda j, i, g=g: (l, 0, g * nj + j))
                              for g in range(N_BRANCH)]
    return pl.pallas_call(
        _merge_kernel,
        grid=(nj, n // tm),
        in_specs=[row((tm, d))] + [row((tm, MIX_W))] * N_BRANCH + gate_cols(d) + gate_cols(1)
        + [pl.BlockSpec((None, N_BRANCH, MIX_W, tn), lambda j, i: (l, 0, 0, j))],
        out_specs=pl.BlockSpec((tm, tn), lambda j, i: (i, j)),
        out_shape=jax.ShapeDtypeStruct((n, d), BF16),
        compiler_params=_cp(("parallel", "parallel")),
        name="merge",
    )(xb, *ys, *([wg] * N_BRANCH), *([bg] * N_BRANCH), wb)


def _out_ln_kernel(m_ref, w_ref, x_ref, g_ref, b_ref, o_ref, *, alpha):
    y = alpha * x_ref[...] + _dot(m_ref[...], w_ref[...])
    o_ref[...] = _layer_norm(y, g_ref[...], b_ref[...])


def out_ln(merged, w_out, x, g, b, l, alpha, tm):
    n, d = x.shape
    row = pl.BlockSpec((tm, d), lambda i: (i, 0))
    lay = lambda shape: pl.BlockSpec((None,) + shape, lambda i: (l, 0, 0))
    return pl.pallas_call(
        functools.partial(_out_ln_kernel, alpha=alpha),
        grid=(n // tm,),
        in_specs=[row, lay((d, d)), row, lay((1, d)), lay((1, d))],
        out_specs=row,
        out_shape=jax.ShapeDtypeStruct((n, d), F32),
        compiler_params=_cp(("parallel",)),
        name="out_ln",
    )(merged, w_out, x, g, b)


FFN_HALO = 16


def _ffn_kernel(x_ref, halo_ref, wh_ref, wg_ref, wd_ref, cw_ref, cb_ref, st_ref, g_ref, b_ref,
                o_ref, ob_ref, hl_ref, xcat_ref, hs_ref, acc_ref, *, tm, alpha, chained, tiles_per_seq):
    i = pl.program_id(0)
    f = pl.program_id(1)

    @pl.when(f == 0)
    def _():
        acc_ref[...] = jnp.zeros(acc_ref.shape, F32)
        xcat_ref[FFN_HALO:, :] = x_ref[...].astype(BF16)
        if chained:
            seq_start = (i % tiles_per_seq) == 0
            xcat_ref[0:FFN_HALO, :] = jnp.where(seq_start, 0.0, halo_ref[...]).astype(BF16)

    w0 = cw_ref[0:1, :]
    w1 = cw_ref[1:2, :]
    w2 = cw_ref[2:3, :]
    if chained:
        hs_ref[...] = _dot(xcat_ref[...], wh_ref[...])
        h = hs_ref[FFN_HALO:, :]
        hm1 = hs_ref[pl.ds(FFN_HALO - 1, tm), :]
        hm2 = hs_ref[pl.ds(FFN_HALO - 2, tm), :]
        hl_ref[0] = hs_ref[tm + FFN_HALO - SUBLANES:, :]
        conv = cb_ref[...] + w0 * hm2 + w1 * hm1 + w2 * h
    else:
        t = SUBLANES
        nseq = tm // t
        tf = wh_ref.shape[1]
        h = _dot(xcat_ref[FFN_HALO:, :], wh_ref[...]).reshape(nseq, t, tf)
        r = lax.broadcasted_iota(jnp.int32, (nseq, t, tf), 1)
        b0 = st_ref[:, 0:1, :]
        b1 = st_ref[:, 1:2, :]
        hm1 = jnp.where(r >= 1, pltpu.roll(h, 1, 1), b1)
        hm2 = jnp.where(r >= 2, pltpu.roll(h, 2, 1), jnp.where(r == 1, b1, b0))
        hl_ref[...] = h[:, t - (CONV_W - 1):, :]
        conv = (cb_ref[...] + w0 * hm2 + w1 * hm1 + w2 * h).reshape(tm, tf)
    gate = _dot(xcat_ref[FFN_HALO:, :], wg_ref[...])
    a = (_gelu_tanh(conv) * gate).astype(BF16)
    acc_ref[...] += _dot(a, wd_ref[...])

    @pl.when(f == pl.num_programs(1) - 1)
    def _():
        y = _layer_norm(alpha * x_ref[...] + acc_ref[...], g_ref[...], b_ref[...])
        o_ref[...] = y
        ob_ref[...] = y.astype(ob_ref.dtype)


def conv_ffn(x, w_up, w_down, cw, cb, g, b, l, state, ls, alpha, nb, t, chained, tm, tf):
    n, d = x.shape
    ff = w_down.shape[1]
    nf = ff // tf
    lay = lambda shape: pl.BlockSpec((None,) + shape, lambda i, f: (l, 0, 0))
    hb = tm // FFN_HALO
    if chained:
        tiles_per_seq = t // tm
        st = jnp.zeros((1, 1, CONV_W - 1, tf), F32)
        st_spec = pl.BlockSpec((None, 1, CONV_W - 1, tf), lambda i, f: (0, 0, 0, 0))
        hl_shape = jax.ShapeDtypeStruct((n // tm, SUBLANES, ff), F32)
        hl_spec = pl.BlockSpec((1, SUBLANES, tf), lambda i, f: (i, 0, f))
    else:
        assert t == SUBLANES
        tiles_per_seq = 1
        st = state
        st_spec = pl.BlockSpec((None, tm // t, CONV_W - 1, tf), lambda i, f: (ls, i, 0, f))
        hl_shape = jax.ShapeDtypeStruct((nb, CONV_W - 1, ff), F32)
        hl_spec = pl.BlockSpec((tm // t, CONV_W - 1, tf), lambda i, f: (i, 0, f))
    o, ob, hl = pl.pallas_call(
        functools.partial(_ffn_kernel, tm=tm, alpha=alpha, chained=chained, tiles_per_seq=tiles_per_seq),
        grid=(n // tm, nf),
        in_specs=[
            pl.BlockSpec((tm, d), lambda i, f: (i, 0)),
            pl.BlockSpec((FFN_HALO, d), lambda i, f: (jnp.maximum(i * hb - 1, 0), 0)),
            pl.BlockSpec((None, d, tf), lambda i, f: (l, 0, f)),
            pl.BlockSpec((None, d, tf), lambda i, f: (l, 0, nf + f)),
            pl.BlockSpec((None, tf, d), lambda i, f: (l, f, 0)),
            pl.BlockSpec((None, CONV_W, tf), lambda i, f: (l, 0, f)),
            pl.BlockSpec((None, 1, tf), lambda i, f: (l, 0, f)),
            st_spec, lay((1, d)), lay((1, d)),
        ],
        out_specs=[pl.BlockSpec((tm, d), lambda i, f: (i, 0)), pl.BlockSpec((tm, d), lambda i, f: (i, 0)), hl_spec],
        out_shape=[jax.ShapeDtypeStruct((n, d), F32), jax.ShapeDtypeStruct((n, d), BF16), hl_shape],
        scratch_shapes=[pltpu.VMEM((tm + FFN_HALO, d), BF16), pltpu.VMEM((tm + FFN_HALO, tf), F32),
                        pltpu.VMEM((tm, d), F32)],
        compiler_params=_cp(("parallel", "arbitrary")),
        name="conv_ffn",
    )(x, x, w_up, w_up, w_down, cw, cb, st, g, b)
    if chained:
        hl = hl[tiles_per_seq - 1::tiles_per_seq, SUBLANES - (CONV_W - 1):, :]
    return o, ob, hl


def _rope_tables(pos0, t):
    half = MLA_ROPE // 2
    inv = ROPE_THETA ** (-jnp.arange(half, dtype=F32) / half)
    ang = (pos0 + jnp.arange(t)).astype(F32)[:, None] * inv
    cos, sin = jnp.cos(ang), jnp.sin(ang)
    cosf = jnp.concatenate([cos, cos], -1)
    sinf = jnp.concatenate([-sin, sin], -1)
    return jnp.tile(cosf, (1, MLA_HEADS)), jnp.tile(sinf, (1, MLA_HEADS))


def _rot_half_cols(w):
    half = w.shape[-1] // 2
    return jnp.concatenate([w[..., half:], w[..., :half]], -1)


IN_SIZES = (GLA_HEADS * GLA_DK, GLA_HEADS * GLA_DK, GLA_HEADS * GLA_DV, MIX_W, GLA_GATE_RANK,
            MIX_W, MIX_W, MLA_Q_RANK, MLA_KV_RANK, MLA_ROPE)
IN_GATE_OFF = sum(IN_SIZES)
IN_DEST = (Z_Q, Z_K, Z_V, Z_OG, Z_GA, Z_SU, Z_PU, Z_MQ, Z_MKV, Z_MKR)


def _win_prep_kernel(w_ref, wz_ref, wg_ref):
    off = 0
    for size, dst in zip(IN_SIZES, IN_DEST):
        wz_ref[:, dst:dst + size] = w_ref[:, off:off + size].astype(BF16)
        off += size
    half = MLA_ROPE // 2
    kr_off = IN_GATE_OFF - MLA_ROPE
    wz_ref[:, Z_MKR_ROT:Z_MKR_ROT + half] = w_ref[:, kr_off + half:kr_off + MLA_ROPE].astype(BF16)
    wz_ref[:, Z_MKR_ROT + half:Z_MKR_ROT + MLA_ROPE] = w_ref[:, kr_off:kr_off + half].astype(BF16)
    pad0 = Z_GA + GLA_GATE_RANK
    wz_ref[:, pad0:Z_COLS] = jnp.zeros((wz_ref.shape[0], Z_COLS - pad0), BF16)
    wg_ref[...] = w_ref[:, IN_GATE_OFF:].astype(BF16)


def win_prep(w_in, tr):
    depth, d_model, cols = w_in.shape
    ngate = cols - IN_GATE_OFF
    return pl.pallas_call(
        _win_prep_kernel,
        grid=(depth, d_model // tr),
        in_specs=[pl.BlockSpec((None, tr, cols), lambda l, i: (l, i, 0))],
        out_specs=[pl.BlockSpec((None, tr, Z_COLS), lambda l, i: (l, i, 0)),
                   pl.BlockSpec((None, tr, ngate), lambda l, i: (l, i, 0))],
        out_shape=[jax.ShapeDtypeStruct((depth, d_model, Z_COLS), BF16),
                   jax.ShapeDtypeStruct((depth, d_model, ngate), BF16)],
        compiler_params=_cp(("parallel", "parallel")),
        name="win_prep",
    )(w_in)


def _prep_weights(p):
    depth, d_model, _ = p['w_in'].shape
    hk = GLA_HEADS * GLA_DK
    w_z, w_gate = win_prep(p['w_in'], _tile(d_model, 128))
    out = dict(w_z=w_z, w_gate=w_gate, b_gate=p['b_gates'].reshape(depth, 1, N_BRANCH * d_model))
    out['gla_wg'] = jnp.zeros((depth, 128, hk), F32).at[:, :GLA_GATE_RANK].set(p['w_gla_gate'])
    out['gla_bg'] = p['b_gla_gate'].reshape(depth, 1, hk)
    out['gla_ng'] = p['gla_norm_g'].reshape(depth, 1, MIX_W)

    g2 = S5_GROUPS // 2
    eye = jnp.eye(g2, dtype=F32)

    def blockdiag_in(bm):
        bm = bm.reshape(depth, 2, g2, S5_STATE, S5_GROUP)
        w = jnp.einsum('lhgnj,gk->lhgjkn', bm, eye)
        return w.reshape(depth, 2, g2 * S5_GROUP, g2 * S5_STATE).astype(BF16)

    def blockdiag_out(cm):
        cm = cm.reshape(depth, 2, g2, S5_GROUP, S5_STATE)
        w = jnp.einsum('lhgjn,gk->lhgnkj', cm, eye)
        return w.reshape(depth, 2, g2 * S5_STATE, g2 * S5_GROUP).astype(BF16)

    out['s5_wbr'], out['s5_wbi'] = blockdiag_in(p['s5_b_re']), blockdiag_in(p['s5_b_im'])
    out['s5_wcr'], out['s5_wci'] = blockdiag_out(p['s5_c_re']), blockdiag_out(p['s5_c_im'])
    out['s5_d'] = p['s5_d'].reshape(depth, 1, MIX_W)
    out['s5_wglu'] = p['w_s5_glu'].astype(BF16)
    out['s5_bglu'] = p['b_s5_glu'].reshape(depth, 1, MIX_W)
    out['pool_w'] = p['w_pool'].astype(BF16)
    out['pool_sc'] = p['pool_scale'].reshape(depth, 1, MIX_W)
    wq = p['w_q_up'].reshape(depth, MLA_Q_RANK, MLA_HEADS, MLA_NOPE + MLA_ROPE)
    flat = lambda w: w.reshape(depth, MLA_Q_RANK, -1)
    wq_rope = wq[..., MLA_NOPE:]
    out['mla_wq'] = jnp.concatenate(
        [flat(wq[..., :MLA_NOPE]), flat(wq_rope), flat(_rot_half_cols(wq_rope))], 2).astype(BF16)
    wkv = p['w_kv_up'].reshape(depth, MLA_KV_RANK, MLA_HEADS, MLA_NOPE + MLA_V)
    out['mla_wuk'] = wkv[..., :MLA_NOPE].transpose(0, 2, 3, 1).astype(BF16)
    out['mla_wuv'] = wkv[..., MLA_NOPE:].transpose(0, 2, 1, 3).astype(BF16)
    out['mla_qg'] = p['mla_q_norm'].reshape(depth, 1, MLA_Q_RANK)
    out['mla_kg'] = p['mla_kv_norm'].reshape(depth, 1, MLA_KV_RANK)
    out['w_branch'] = p['w_branch'].astype(BF16)
    out['w_out'] = p['w_out'].astype(BF16)
    out['w_up'] = p['w_ffn_up'].astype(BF16)
    out['w_down'] = p['w_ffn_down'].astype(BF16)
    out['conv_w'] = p['ffn_conv_w']
    out['conv_b'] = p['ffn_conv_b'].reshape(depth, 1, -1)
    for nm in ('ln1_g', 'ln1_b', 'ln2_g', 'ln2_b'):
        out[nm] = p[nm].reshape(depth, 1, d_model)
    return out


def _layer(x, xb, w, tabs, st, nb, t, pos0, l, ls, cache, alpha, chained):
    n, d = x.shape
    z = matmul(xb, w['w_z'], l, _tile(n, 1024), Z_COLS // 2)
    gla_w = (w['gla_wg'], w['gla_bg'], w['gla_ng'], l)
    s5_w = (tabs[0], tabs[1], w['s5_wbr'], w['s5_wbi'], w['s5_wcr'], w['s5_wci'], w['s5_d'], w['s5_wglu'],
            w['s5_bglu'], l)
    cos4, sin4 = _rope_tables(pos0, t)
    mla_w = (w['mla_qg'], w['mla_kg'], w['mla_wq'], w['mla_wuk'], l)
    if chained:
        gl = _tile(t, 64)
        ya, s_gla = gla(z, *gla_w, st['gla'], ls, nb, t, gl, _tile(t // gl, 4), True)
        yb, h_re, h_im = s5(z, *s5_w, st['s5_re'], st['s5_im'], ls, nb, t, True, SUBLANES * S5_NJ_CHAIN)
        yc, pbuf = pool(z, w['pool_w'], w['pool_sc'], l, st['pool'], ls, nb, t, pos0, True, _tile(t, 256))
        qcat, kcat, c_new, kr_new = mla_prep(z, cos4, sin4, *mla_w, _tile(t, 256), BF16)
        yd = mla_causal(qcat, kcat, w['mla_wuv'], l, nb, t, _tile(t, 512))
    else:
        ya, s_gla = gla(z, *gla_w, st['gla'], ls, nb, t, t, _tile(nb, 8), False)
        yb, h_re, h_im = s5(z, *s5_w, st['s5_re'], st['s5_im'], ls, nb, t, False, _tile(n, 256))
        yc, pbuf = pool(z, w['pool_w'], w['pool_sc'], l, st['pool'], ls, nb, t, pos0, False, _tile(n, 128))
        prep_rows = _tile(n, 128)
        tile_rows = lambda a: jnp.tile(a, (prep_rows // t, 1))
        qcat, kcat, c_new, kr_new = mla_prep(z, tile_rows(cos4), tile_rows(sin4), *mla_w, prep_rows, F32)
        yd = mla_paged(qcat, kcat, cache[0], cache[1], cache[2], w['mla_wuv'], l)
    merged = merge(xb, (ya, yb, yc, yd), w['w_gate'], w['b_gate'], w['w_branch'], l, _tile(n, 512), _tile(d, 512))
    x1 = out_ln(merged, w['w_out'], x, w['ln1_g'], w['ln1_b'], l, alpha, _tile(n, 512))
    ff = w['w_down'].shape[1]
    tm = _tile(t, 512) if chained else _tile(n, 512)
    x2, x2b, cbuf = conv_ffn(x1, w['w_up'], w['w_down'], w['conv_w'], w['conv_b'], w['ln2_g'], w['ln2_b'], l,
                             st['conv'], ls, alpha, nb, t, chained, tm, _tile(ff, 512))
    new = (c_new.reshape(nb, t, MLA_KV_RANK), kr_new.reshape(nb, t, MLA_ROPE), s_gla, h_re, h_im, pbuf, cbuf)
    return x2, x2b, new


def _trunk(x3, w, tabs, states, shared_state, pos0, cache, alpha, chained):
    nb, t, d = x3.shape
    depth = w['w_z'].shape[0]
    x = x3.reshape(nb * t, d)
    xb = x.astype(BF16)
    outs = [[] for _ in range(7)]
    for l in range(depth):
        x, xb, new = _layer(x, xb, w, tabs, states, nb, t, pos0, l, 0 if shared_state else l, cache, alpha,
                            chained)
        for lst, val in zip(outs, new):
            lst.append(val)
    return (x.reshape(nb, t, d),) + tuple(jnp.stack(o, 0) for o in outs)


def kernel(x_prompt, x_sample, cache_kv_latent, cache_k_rope, page_table, state_gla, state_s5_re, state_s5_im,
           state_pool, state_ffn_conv, ln1_g, ln1_b, w_in, b_gates, w_gla_gate, b_gla_gate, gla_norm_g,
           s5_a_re, s5_a_im, s5_log_dt, s5_b_re, s5_b_im, s5_c_re, s5_c_im, s5_d, w_s5_glu, b_s5_glu,
           w_pool, pool_scale, mla_q_norm, w_q_up, mla_kv_norm, w_kv_up, w_branch, w_out, ln2_g, ln2_b,
           w_ffn_up, ffn_conv_w, ffn_conv_b, w_ffn_down):
    params = {'ln1_g': ln1_g, 'ln1_b': ln1_b, 'w_in': w_in, 'b_gates': b_gates, 'w_gla_gate': w_gla_gate,
              'b_gla_gate': b_gla_gate, 'gla_norm_g': gla_norm_g, 's5_b_re': s5_b_re, 's5_b_im': s5_b_im,
              's5_c_re': s5_c_re, 's5_c_im': s5_c_im, 's5_d': s5_d, 'w_s5_glu': w_s5_glu, 'b_s5_glu': b_s5_glu,
              'w_pool': w_pool, 'pool_scale': pool_scale, 'mla_q_norm': mla_q_norm, 'w_q_up': w_q_up,
              'mla_kv_norm': mla_kv_norm, 'w_kv_up': w_kv_up, 'w_branch': w_branch, 'w_out': w_out,
              'ln2_g': ln2_g, 'ln2_b': ln2_b, 'w_ffn_up': w_ffn_up, 'ffn_conv_w': ffn_conv_w,
              'ffn_conv_b': ffn_conv_b, 'w_ffn_down': w_ffn_down}
    depth = w_in.shape[0]
    d_ff = ffn_conv_w.shape[-1]
    alpha = (2 * depth) ** 0.25
    w = _prep_weights(params)
    tabs = s5_tables(s5_a_re, s5_a_im, s5_log_dt, S5_NJ_CHAIN)
    pad_pool = lambda a: jnp.pad(a, ((0, 0), (0, 0), (1, 0), (0, 0)))

    nb_p = x_prompt.shape[0]
    zeros = lambda *shape: jnp.zeros((1, nb_p) + shape, F32)
    st_p = dict(gla=zeros(GLA_HEADS, GLA_DK, GLA_DV), s5_re=zeros(S5_W), s5_im=zeros(S5_W),
                pool=zeros(HALO, MIX_W), conv=None)
    res_p = _trunk(x_prompt, w, tabs, st_p, True, 0, None, alpha, True)

    nb_s = x_sample.shape[0]
    past_len = page_table.shape[1] * PAGE_SIZE
    st_s = dict(gla=state_gla, s5_re=state_s5_re.reshape(depth, nb_s, S5_W),
                s5_im=state_s5_im.reshape(depth, nb_s, S5_W), pool=pad_pool(state_pool), conv=state_ffn_conv)
    cache_krt = jnp.swapaxes(cache_k_rope, 2, 3)
    res_s = _trunk(x_sample, w, tabs, st_s, False, past_len, (cache_kv_latent, cache_krt, page_table), alpha, False)
    return (res_p[0], res_s[0]) + res_p[1:] + res_s[1:]
```

```python
import functools
import math

import jax
import jax.numpy as jnp
from jax import lax
from jax.experimental import pallas as pl
from jax.experimental.pallas import tpu as pltpu

F32 = jnp.float32
BF16 = jnp.bfloat16
HIGHEST = lax.Precision.HIGHEST

MIX_W = 512
N_BRANCH = 4
GLA_HEADS, GLA_DK, GLA_DV, GLA_GATE_RANK, GLA_TAU = 4, 64, 128, 16, 16.0
S5_GROUP, S5_GROUPS, S5_STATE = 16, 32, 64
S5_W = S5_GROUPS * S5_STATE
POOL_WINDOWS = (2, 4, 8, 16)
POOL_GROUP = 128
POOL_BUF = 15
MLA_HEADS, MLA_NOPE, MLA_ROPE, MLA_V, MLA_Q_RANK, MLA_KV_RANK = 4, 128, 64, 128, 384, 128
MLA_SCALE = (MLA_NOPE + MLA_ROPE) ** -0.5
MLA_QK = 256
ROPE_THETA = 10000.0
PAGE_SIZE = 128
CONV_W = 3
LN_EPS = 1e-5
RMS_EPS = 1e-6

Z_Q, Z_K, Z_V, Z_OG, Z_SU, Z_PU, Z_MQ, Z_MKV, Z_MKR, Z_MKR_ROT, Z_GA = (
    0, 256, 512, 1024, 1536, 2048, 2560, 2944, 3072, 3136, 3200)
Z_COLS = 3328

V7X_VMEM_BYTES = 64 * 1024 * 1024
VMEM_LIMIT = 48 * 1024 * 1024
SUBLANES = 8


def _cp(sem, vmem=VMEM_LIMIT):
    return pltpu.CompilerParams(dimension_semantics=sem, vmem_limit_bytes=vmem)


def _tile(dim, pref):
    t = min(dim, pref)
    while dim % t:
        t //= 2
    return t


def _sigmoid(x):
    return 1.0 / (1.0 + jnp.exp(-x))


def _gelu_tanh(x):
    c = math.sqrt(2.0 / math.pi)
    return 0.5 * x * (1.0 + jnp.tanh(c * (x + 0.044715 * (x * x * x))))


def _log_sigmoid(x):
    return jnp.minimum(x, 0.0) - jnp.log(1.0 + jnp.exp(-jnp.abs(x)))


def _layer_norm(y, g, b):
    mu = jnp.mean(y, -1, keepdims=True)
    d = y - mu
    var = jnp.mean(d * d, -1, keepdims=True)
    return d * lax.rsqrt(var + LN_EPS) * g + b


def _rms(x, g):
    return x * lax.rsqrt(jnp.mean(x * x, -1, keepdims=True) + RMS_EPS) * g


def _dot(a, b):
    return jnp.dot(a, b, preferred_element_type=F32)


def _dot_t(a, b):
    return lax.dot_general(a, b, (((1,), (1,)), ((), ())), preferred_element_type=F32)


def _dot_ta(a, b, **kw):
    return lax.dot_general(a, b, (((0,), (0,)), ((), ())), preferred_element_type=F32, **kw)


def _cmul(ar, ai, br, bi):
    return ar * br - ai * bi, ar * bi + ai * br


def _mm_kernel(x_ref, w_ref, o_ref):
    o_ref[...] = _dot_t(x_ref[...], w_ref[...]).astype(o_ref.dtype)


def matmul(x, w, l, tm, tn, out_dtype=F32):
    m, k = x.shape
    n = w.shape[1]
    return pl.pallas_call(
        _mm_kernel,
        grid=(m // tm, n // tn),
        in_specs=[pl.BlockSpec((tm, k), lambda i, j: (i, 0)),
                  pl.BlockSpec((None, tn, k), lambda i, j: (l, j, 0))],
        out_specs=pl.BlockSpec((tm, tn), lambda i, j: (i, j)),
        out_shape=jax.ShapeDtypeStruct((m, n), out_dtype),
        compiler_params=_cp(("parallel", "parallel")),
        name="in_proj",
    )(x, w)


GLA_SAFE_LOGDECAY = 60.0


def _gla_kernel(q_ref, k_ref, v_ref, og_ref, ga_ref, wg_ref, bg_ref, ng_ref, s0_ref,
                ya_ref, st_ref, la_ref, o_ref, *, L, nsub, chain):
    R = nsub * L
    assert chain or L == SUBLANES

    @pl.when(pl.program_id(1) == 0)
    def _():
        st_ref[...] = s0_ref[...]

    pre = jnp.dot(ga_ref[...], wg_ref[...], precision=HIGHEST, preferred_element_type=F32) + bg_ref[...]
    log_a = _log_sigmoid(pre) * (1.0 / GLA_TAU)
    hk = GLA_HEADS * GLA_DK
    blk_of = lambda x: sum((x >= j * L).astype(jnp.int32) for j in range(1, nsub))
    row = lax.broadcasted_iota(jnp.int32, (R, R), 0)
    col = lax.broadcasted_iota(jnp.int32, (R, R), 1)
    tri = jnp.where(row >= col, 1.0, 0.0)
    if nsub > 1:
        tri = jnp.where(blk_of(row) == blk_of(col), tri, 0.0)
    causal = tri > 0.5
    a_hi = log_a.astype(BF16)
    rem = log_a - a_hi.astype(F32)
    a_mid = rem.astype(BF16)
    a_lo = (rem - a_mid.astype(F32)).astype(BF16)
    a3 = jnp.concatenate([a_hi, a_mid, a_lo], 1)
    sum3 = lambda m, ax: (lax.slice_in_dim(m, 0, hk, axis=ax) + lax.slice_in_dim(m, hk, 2 * hk, axis=ax)
                          + lax.slice_in_dim(m, 2 * hk, 3 * hk, axis=ax))
    bc = sum3(_dot(tri.astype(BF16), a3), 1)
    if nsub > 1:
        rowb = lax.broadcasted_iota(jnp.int32, (R, nsub * GLA_DV), 0)
        colb = lax.broadcasted_iota(jnp.int32, (R, nsub * GLA_DV), 1)
        blk_col = sum((colb >= j * GLA_DV).astype(jnp.int32) for j in range(1, nsub))
        member = jnp.where(blk_of(rowb) == blk_col, 1.0, 0.0).astype(BF16)
    else:
        member = jnp.ones((R, GLA_DV), BF16)
    a_tot_all = jnp.exp(sum3(_dot_ta(a3, member), 0))
    tot = jnp.concatenate([jnp.broadcast_to(bc[(j + 1) * L - 1:(j + 1) * L, :], (L, hk)) for j in range(nsub)], 0)
    scale = GLA_DK ** -0.5
    safe = jnp.max(jnp.abs(bc)) < GLA_SAFE_LOGDECAY
    la_ref[...] = log_a

    @pl.when(safe)
    def _():
        q = q_ref[...] * scale
        k = k_ref[...]
        q_in = q * jnp.exp(bc)
        k_in = k * jnp.exp(-bc)
        k_out = k * jnp.exp(tot - bc)
        for h in range(GLA_HEADS):
            ks = slice(h * GLA_DK, (h + 1) * GLA_DK)
            vs = slice(h * GLA_DV, (h + 1) * GLA_DV)
            v = v_ref[:, vs]
            scores = jnp.where(causal, _dot_t(q_in[:, ks].astype(BF16), k_in[:, ks].astype(BF16)), 0.0)
            o = _dot(scores.astype(BF16), v.astype(BF16))
            o_inter = []
            state = st_ref[0, h]
            for j in range(nsub):
                rs = slice(j * L, (j + 1) * L)
                if not chain:
                    state = st_ref[j, h]
                o_inter.append(_dot(q_in[rs, ks].astype(BF16), state.astype(BF16)))
                a_tot = a_tot_all[ks, j * GLA_DV:(j + 1) * GLA_DV]
                state = a_tot * state + _dot_ta(k_out[rs, ks].astype(BF16), v[rs].astype(BF16))
                if not chain:
                    st_ref[j, h] = state
            if chain:
                st_ref[0, h] = state
            o_ref[:, vs] = o + (o_inter[0] if nsub == 1 else jnp.concatenate(o_inter, 0))

    @pl.when(jnp.logical_not(safe))
    def _():
        rr = lax.broadcasted_iota(jnp.int32, (SUBLANES, SUBLANES * GLA_DV), 0)
        cc = lax.broadcasted_iota(jnp.int32, (SUBLANES, SUBLANES * GLA_DV), 1)
        spread = jnp.where(sum((cc >= r * GLA_DV).astype(jnp.int32) for r in range(1, SUBLANES)) == rr, 1.0, 0.0)
        columns = lambda x8: _dot_ta(x8, spread, precision=HIGHEST)
        for h in range(GLA_HEADS):
            ks = slice(h * GLA_DK, (h + 1) * GLA_DK)
            vs = slice(h * GLA_DV, (h + 1) * GLA_DV)
            def eight_tokens(i8, state):
                rows = pl.ds(pl.multiple_of(i8 * SUBLANES, SUBLANES), SUBLANES)
                a_cols = jnp.exp(columns(la_ref[rows, ks]))
                k_cols = columns(k_ref[rows, ks])
                q_cols = columns(q_ref[rows, ks] * scale)
                v8 = v_ref[rows, vs]
                outs = []
                for r in range(SUBLANES):
                    lanes = slice(r * GLA_DV, (r + 1) * GLA_DV)
                    state = a_cols[:, lanes] * state + k_cols[:, lanes] * v8[r:r + 1, :]
                    outs.append(jnp.sum(q_cols[:, lanes] * state, 0, keepdims=True))
                o_ref[rows, vs] = jnp.concatenate(outs, 0)
                return state

            if chain:
                st_ref[0, h] = lax.fori_loop(0, R // SUBLANES, eight_tokens, st_ref[0, h])
            else:
                def one_sequence(j, carry):
                    st_ref[j, h] = eight_tokens(j, st_ref[j, h])
                    return carry

                lax.fori_loop(0, nsub, one_sequence, 0)

    for h in range(GLA_HEADS):
        vs = slice(h * GLA_DV, (h + 1) * GLA_DV)
        o = o_ref[:, vs]
        o = o * lax.rsqrt(jnp.mean(o * o, -1, keepdims=True) + RMS_EPS) * ng_ref[:, vs]
        g = og_ref[:, vs]
        ya_ref[:, vs] = (o * (g * _sigmoid(g))).astype(ya_ref.dtype)


def gla(z, wg_pad, bg, ng, l, s0, ls, nb, t, L, nsub, chain):
    R = nsub * L
    if chain:
        nc = t // R
        grid = (nb, nc)
        nst = 1
    else:
        assert L == t
        nc = 1
        grid = (nb // nsub, 1)
        nst = nsub
    rb = lambda b, c: b * nc + c
    hk = GLA_HEADS * GLA_DK
    st_blk = (nst, GLA_HEADS, GLA_DK, GLA_DV)
    return pl.pallas_call(
        functools.partial(_gla_kernel, L=L, nsub=nsub, chain=chain),
        grid=grid,
        in_specs=[
            pl.BlockSpec((R, hk), lambda b, c: (rb(b, c), Z_Q // hk)),
            pl.BlockSpec((R, hk), lambda b, c: (rb(b, c), Z_K // hk)),
            pl.BlockSpec((R, MIX_W), lambda b, c: (rb(b, c), Z_V // MIX_W)),
            pl.BlockSpec((R, MIX_W), lambda b, c: (rb(b, c), Z_OG // MIX_W)),
            pl.BlockSpec((R, 128), lambda b, c: (rb(b, c), Z_GA // 128)),
            pl.BlockSpec((None, 128, hk), lambda b, c: (l, 0, 0)),
            pl.BlockSpec((None, 1, hk), lambda b, c: (l, 0, 0)),
            pl.BlockSpec((None, 1, MIX_W), lambda b, c: (l, 0, 0)),
            pl.BlockSpec((None,) + st_blk, lambda b, c: (ls, b, 0, 0, 0)),
        ],
        out_specs=[pl.BlockSpec((R, MIX_W), lambda b, c: (rb(b, c), 0)),
                   pl.BlockSpec(st_blk, lambda b, c: (b, 0, 0, 0))],
        out_shape=[jax.ShapeDtypeStruct((nb * t, MIX_W), BF16),
                   jax.ShapeDtypeStruct((nb, GLA_HEADS, GLA_DK, GLA_DV), F32)],
        scratch_shapes=[pltpu.VMEM((R, hk), F32), pltpu.VMEM((R, MIX_W), F32)],
        compiler_params=_cp(("parallel", "arbitrary")),
        name="gla",
    )(z, z, z, z, z, wg_pad, bg, ng, s0)


S5_NJ_CHAIN = 32
TAB_A, TAB_T1, TAB_T2, TAB_T4, TAB_F = 0, 2, 4, 6, 8
N_TAB = 10
S5_LANES = 512


def _s5_tab_kernel(are_ref, aim_ref, ldt_ref, tab_ref, pj_ref, *, nj):
    lr = are_ref[0]
    li = aim_ref[0]
    dt = jnp.exp(ldt_ref[0])
    mag = jnp.exp(lr * dt)
    a_re = mag * jnp.cos(li * dt)
    a_im = mag * jnp.sin(li * dt)
    shp = (SUBLANES, S5_W)
    r = lax.broadcasted_iota(jnp.int32, shp, 0)
    tab_ref[0, TAB_A] = jnp.broadcast_to(a_re, shp)
    tab_ref[0, TAB_A + 1] = jnp.broadcast_to(a_im, shp)
    pr, pi = a_re, a_im
    for j in range(nj):
        pj_ref[0, 0, j * SUBLANES:(j + 1) * SUBLANES, :] = jnp.broadcast_to(pr, shp)
        pj_ref[0, 1, j * SUBLANES:(j + 1) * SUBLANES, :] = jnp.broadcast_to(pi, shp)
        if j + 1 < nj:
            pr, pi = _cmul(pr, pi, a_re, a_im)
    br, bi = pr, pi
    for s, idx in ((1, TAB_T1), (2, TAB_T2), (4, TAB_T4)):
        tab_ref[0, idx] = jnp.where(r >= s, br, 0.0)
        tab_ref[0, idx + 1] = jnp.where(r >= s, bi, 0.0)
        br, bi = _cmul(br, bi, br, bi)
    den = lr * lr + li * li
    nr, ni = a_re - 1.0, a_im
    tab_ref[0, TAB_F] = jnp.broadcast_to((nr * lr + ni * li) / den, shp)
    tab_ref[0, TAB_F + 1] = jnp.broadcast_to((ni * lr - nr * li) / den, shp)


def s5_tables(a_re, a_im, log_dt, nj):
    depth = a_re.shape[0]
    flat = lambda a: a.reshape(depth, 1, S5_W)
    ldt = jnp.broadcast_to(log_dt[:, :, None], (depth, S5_GROUPS, S5_STATE))
    spec = pl.BlockSpec((1, 1, S5_W), lambda l: (l, 0, 0))
    return pl.pallas_call(
        functools.partial(_s5_tab_kernel, nj=nj),
        grid=(depth,),
        in_specs=[spec, spec, spec],
        out_specs=[pl.BlockSpec((1, N_TAB, SUBLANES, S5_W), lambda l: (l, 0, 0, 0)),
                   pl.BlockSpec((1, 2, nj * SUBLANES, S5_W), lambda l: (l, 0, 0, 0))],
        out_shape=[jax.ShapeDtypeStruct((depth, N_TAB, SUBLANES, S5_W), F32),
                   jax.ShapeDtypeStruct((depth, 2, nj * SUBLANES, S5_W), F32)],
        compiler_params=_cp(("parallel",)),
        name="s5_tables",
    )(flat(a_re), flat(a_im), flat(ldt))


def _s5_kernel(u0_ref, u1_ref, u2_ref, u3_ref, tab_ref, pj_ref, wbr_ref, wbi_ref, wcr_ref, wci_ref, d_ref,
               wglu_ref, bglu_ref, h0r_ref, h0i_ref, yb_ref, hr_out_ref, hi_out_ref,
               up_ref, xr_ref, xi_ref, cr_ref, ci_ref, *, R, nj, chained):
    half_u = MIX_W // 2
    half_s = S5_W // 2
    grp = SUBLANES * nj
    ngrp = R // grp
    ncol = MIX_W // 128
    for c, u_ref in enumerate((u0_ref, u1_ref, u2_ref, u3_ref)):
        for g in range(ngrp):
            for j in range(nj):
                up_ref[c, g * grp + j * SUBLANES:g * grp + (j + 1) * SUBLANES, :] = (
                    u_ref[pl.ds(g * grp + j, SUBLANES, stride=nj), :])
    u = jnp.concatenate([up_ref[c] for c in range(ncol)], 1)
    ub = u.astype(BF16)
    fr = tab_ref[TAB_F][0:1]
    fi = tab_ref[TAB_F + 1][0:1]
    for hf in range(2):
        us = ub[:, hf * half_u:(hf + 1) * half_u]
        ss = slice(hf * half_s, (hf + 1) * half_s)
        bur = _dot(us, wbr_ref[hf])
        bui = _dot(us, wbi_ref[hf])
        xr_ref[:, ss] = fr[:, ss] * bur - fi[:, ss] * bui
        xi_ref[:, ss] = fr[:, ss] * bui + fi[:, ss] * bur

    if chained:
        @pl.when(pl.program_id(1) == 0)
        def _():
            cr_ref[...] = h0r_ref[0]
            ci_ref[...] = h0i_ref[0]

    row8 = lax.broadcasted_iota(jnp.int32, (SUBLANES, S5_LANES), 0)
    for lc in range(S5_W // S5_LANES):
        ls = slice(lc * S5_LANES, (lc + 1) * S5_LANES)
        ar = tab_ref[TAB_A, :, ls]
        ai = tab_ref[TAB_A + 1, :, ls]
        for g in range(ngrp):
            rows = lambda j: slice(g * grp + j * SUBLANES, g * grp + (j + 1) * SUBLANES)
            if chained:
                hr = jnp.zeros((SUBLANES, S5_LANES), F32)
                hi = jnp.zeros((SUBLANES, S5_LANES), F32)
            else:
                hr = h0r_ref[g * SUBLANES:(g + 1) * SUBLANES, ls]
                hi = h0i_ref[g * SUBLANES:(g + 1) * SUBLANES, ls]
            for j in range(nj):
                pr, pi = _cmul(ar, ai, hr, hi)
                hr = pr + xr_ref[rows(j), ls]
                hi = pi + xi_ref[rows(j), ls]
                xr_ref[rows(j), ls] = hr
                xi_ref[rows(j), ls] = hi
            if not chained:
                hr_out_ref[g * SUBLANES:(g + 1) * SUBLANES, ls] = hr
                hi_out_ref[g * SUBLANES:(g + 1) * SUBLANES, ls] = hi
                continue
            yr = jnp.where(row8 == 0, cr_ref[:, ls], pltpu.roll(hr, 1, 0))
            yi = jnp.where(row8 == 0, ci_ref[:, ls], pltpu.roll(hi, 1, 0))
            for s, idx in ((1, TAB_T1), (2, TAB_T2), (4, TAB_T4)):
                pr, pi = _cmul(tab_ref[idx, :, ls], tab_ref[idx + 1, :, ls],
                               pltpu.roll(yr, s, 0), pltpu.roll(yi, s, 0))
                yr, yi = yr + pr, yi + pi
            for j in range(nj):
                pr, pi = _cmul(pj_ref[0, rows(j), ls], pj_ref[1, rows(j), ls], yr, yi)
                hr = xr_ref[rows(j), ls] + pr
                hi = xi_ref[rows(j), ls] + pi
                xr_ref[rows(j), ls] = hr
                xi_ref[rows(j), ls] = hi
            cr_ref[:, ls] = hr[SUBLANES - 1:SUBLANES]
            ci_ref[:, ls] = hi[SUBLANES - 1:SUBLANES]

    if chained:
        @pl.when(pl.program_id(1) == pl.num_programs(1) - 1)
        def _():
            hr_out_ref[0] = cr_ref[...]
            hi_out_ref[0] = ci_ref[...]

    ys = []
    for hf in range(2):
        ss = slice(hf * half_s, (hf + 1) * half_s)
        y = _dot(xr_ref[:, ss].astype(BF16), wcr_ref[hf]) - _dot(xi_ref[:, ss].astype(BF16), wci_ref[hf])
        us = slice(hf * half_u, (hf + 1) * half_u)
        ys.append(_gelu_tanh(y + d_ref[:, us] * u[:, us]))
    y = jnp.concatenate(ys, 1)
    y = y * _sigmoid(_dot(y.astype(BF16), wglu_ref[...]) + bglu_ref[...])
    for c in range(ncol):
        up_ref[c] = y[:, c * 128:(c + 1) * 128]
        for g in range(ngrp):
            for r in range(SUBLANES):
                yb_ref[g * grp + r * nj:g * grp + (r + 1) * nj, c * 128:(c + 1) * 128] = (
                    up_ref[c, pl.ds(g * grp + r, nj, stride=SUBLANES), :].astype(yb_ref.dtype))


def s5(z, tab, pj, wbr, wbi, wcr, wci, d, wglu, bglu, l, h0r, h0i, ls, nb, t, chained, R):
    depth_s = h0r.shape[0]
    if chained:
        nj = S5_NJ_CHAIN
        assert R == SUBLANES * nj
        nc = t // R
        grid = (nb, nc)
        rb = lambda b, c: b * nc + c
        h0 = (h0r.reshape(depth_s, nb, 1, S5_W), h0i.reshape(depth_s, nb, 1, S5_W))
        st_in = pl.BlockSpec((None, 1, 1, S5_W), lambda b, c: (ls, b, 0, 0))
        st_out = pl.BlockSpec((1, 1, S5_W), lambda b, c: (b, 0, 0))
        st_shape = jax.ShapeDtypeStruct((nb, 1, S5_W), F32)
        out_dtype = BF16
    else:
        assert t == SUBLANES
        nj = SUBLANES
        nseq = R // SUBLANES
        grid = (nb // nseq, 1)
        rb = lambda b, c: b
        h0 = (h0r, h0i)
        st_in = pl.BlockSpec((None, nseq, S5_W), lambda b, c: (ls, b, 0))
        st_out = pl.BlockSpec((nseq, S5_W), lambda b, c: (b, 0))
        st_shape = jax.ShapeDtypeStruct((nb, S5_W), F32)
        out_dtype = F32
    lay = lambda shape: pl.BlockSpec((None,) + shape, lambda b, c: (l,) + (0,) * len(shape))
    yb, hr, hi = pl.pallas_call(
        functools.partial(_s5_kernel, R=R, nj=nj, chained=chained),
        grid=grid,
        in_specs=[pl.BlockSpec((R, 128), lambda b, c, k=k: (rb(b, c), Z_SU // 128 + k))
                  for k in range(MIX_W // 128)] + [
            lay((N_TAB, SUBLANES, S5_W)),
            lay((2, pj.shape[2], S5_W)),
            lay((2, MIX_W // 2, S5_W // 2)), lay((2, MIX_W // 2, S5_W // 2)),
            lay((2, S5_W // 2, MIX_W // 2)), lay((2, S5_W // 2, MIX_W // 2)),
            lay((1, MIX_W)), lay((MIX_W, MIX_W)), lay((1, MIX_W)),
            st_in, st_in,
        ],
        out_specs=[pl.BlockSpec((R, MIX_W), lambda b, c: (rb(b, c), 0)), st_out, st_out],
        out_shape=[jax.ShapeDtypeStruct((nb * t, MIX_W), out_dtype), st_shape, st_shape],
        scratch_shapes=[pltpu.VMEM((MIX_W // 128, R, 128), F32), pltpu.VMEM((R, S5_W), F32),
                        pltpu.VMEM((R, S5_W), F32), pltpu.VMEM((1, S5_W), F32), pltpu.VMEM((1, S5_W), F32)],
        compiler_params=_cp(("parallel", "arbitrary")),
        name="s5",
    )(z, z, z, z, tab, pj, wbr, wbi, wcr, wci, d, wglu, bglu, *h0)
    return yb, hr.reshape(nb, S5_GROUPS, S5_STATE), hi.reshape(nb, S5_GROUPS, S5_STATE)


HALO = 16


def _pool_windows(ext_ref, base, n, pos, wp_ref, sc_ref, out_ref, out_rows):
    for g, w in enumerate(POOL_WINDOWS):
        cs = slice(g * POOL_GROUP, (g + 1) * POOL_GROUP)
        u = ext_ref[pl.ds(base, n), cs]
        s = u
        for j in range(1, w):
            s = s + ext_ref[pl.ds(base - j, n), cs]
        cnt = jnp.minimum(pos + 1, w).astype(F32)
        dlt = s / cnt - u
        y = _dot(dlt.astype(BF16), wp_ref[g]) * sc_ref[:, cs]
        out_ref[out_rows, cs] = y.astype(out_ref.dtype)


def _pool_chain_kernel(u_ref, wp_ref, sc_ref, pre_ref, yc_ref, buf_ref, ext_ref, *, L, pos0):
    c = pl.program_id(1)

    @pl.when(c == 0)
    def _():
        ext_ref[0:HALO, :] = pre_ref[0]

    ext_ref[HALO:HALO + L, :] = u_ref[...]
    pos = pos0 + c * L + lax.broadcasted_iota(jnp.int32, (L, 1), 0)
    _pool_windows(ext_ref, HALO, L, pos, wp_ref, sc_ref, yc_ref, slice(None))
    tail = ext_ref[L:L + HALO, :]
    ext_ref[0:HALO, :] = tail

    @pl.when(c == pl.num_programs(1) - 1)
    def _():
        buf_ref[0] = tail


def _pool_seq_kernel(u_ref, wp_ref, sc_ref, pre_ref, yc_ref, buf_ref, ext_ref, *, nseq, pos0):
    t = SUBLANES
    pos = pos0 + lax.broadcasted_iota(jnp.int32, (t, 1), 0)
    for j in range(nseq):
        ext_ref[0:HALO, :] = pre_ref[j]
        ext_ref[HALO:HALO + t, :] = u_ref[j * t:(j + 1) * t, :]
        _pool_windows(ext_ref, HALO, t, pos, wp_ref, sc_ref, yc_ref, slice(j * t, (j + 1) * t))
        buf_ref[j] = ext_ref[t:t + HALO, :]


def pool(z, wp, sc, l, pre, ls, nb, t, pos0, chained, L):
    lay = lambda shape: pl.BlockSpec((None,) + shape, lambda b, c: (l,) + (0,) * len(shape))
    if chained:
        nc = t // L
        grid = (nb, nc)
        kern = functools.partial(_pool_chain_kernel, L=L, pos0=pos0)
        u_spec = pl.BlockSpec((L, MIX_W), lambda b, c: (b * nc + c, Z_PU // MIX_W))
        y_spec = pl.BlockSpec((L, MIX_W), lambda b, c: (b * nc + c, 0))
        nst = 1
        ext_rows = HALO + L
    else:
        assert t == SUBLANES
        nst = L // t
        grid = (nb // nst, 1)
        kern = functools.partial(_pool_seq_kernel, nseq=nst, pos0=pos0)
        u_spec = pl.BlockSpec((L, MIX_W), lambda b, c: (b, Z_PU // MIX_W))
        y_spec = pl.BlockSpec((L, MIX_W), lambda b, c: (b, 0))
        ext_rows = HALO + t
    yc, buf = pl.pallas_call(
        kern,
        grid=grid,
        in_specs=[u_spec, lay((len(POOL_WINDOWS), POOL_GROUP, POOL_GROUP)), lay((1, MIX_W)),
                  pl.BlockSpec((None, nst, HALO, MIX_W), lambda b, c: (ls, b, 0, 0))],
        out_specs=[y_spec, pl.BlockSpec((nst, HALO, MIX_W), lambda b, c: (b, 0, 0))],
        out_shape=[jax.ShapeDtypeStruct((nb * t, MIX_W), BF16),
                   jax.ShapeDtypeStruct((nb, HALO, MIX_W), F32)],
        scratch_shapes=[pltpu.VMEM((ext_rows, MIX_W), F32)],
        compiler_params=_cp(("parallel", "arbitrary")),
        name="pool",
    )(z, wp, sc, pre)
    return yc, buf[:, 1:, :]


def _mla_prep_kernel(zq_ref, zr_ref, cq_ref, sq_ref, qg_ref, kg_ref, wq_ref, wuk_ref,
                     qcat_ref, kcat_ref, c_ref, kr_ref):
    nq = MLA_HEADS * MLA_NOPE
    nr = MLA_HEADS * MLA_ROPE
    mq = zq_ref[:, 0:MLA_Q_RANK]
    mkv = zq_ref[:, MLA_Q_RANK:MLA_Q_RANK + MLA_KV_RANK]
    qd = _dot(_rms(mq, qg_ref[...]).astype(BF16), wq_ref[...])
    cos4 = cq_ref[...]
    sin4 = sq_ref[...]
    q_rope = (qd[:, nq:nq + nr] * cos4 + qd[:, nq + nr:nq + 2 * nr] * sin4) * MLA_SCALE
    zpad = jnp.zeros((qd.shape[0], MLA_QK - MLA_KV_RANK - MLA_ROPE), qcat_ref.dtype)
    for h in range(MLA_HEADS):
        q_lat = _dot(qd[:, h * MLA_NOPE:(h + 1) * MLA_NOPE].astype(BF16), wuk_ref[h]) * MLA_SCALE
        qcat_ref[h, :, 0:MLA_KV_RANK] = q_lat.astype(qcat_ref.dtype)
        qcat_ref[h, :, MLA_KV_RANK:MLA_KV_RANK + MLA_ROPE] = (
            q_rope[:, h * MLA_ROPE:(h + 1) * MLA_ROPE].astype(qcat_ref.dtype))
        qcat_ref[h, :, MLA_KV_RANK + MLA_ROPE:MLA_QK] = zpad
    c_new = _rms(mkv, kg_ref[...])
    kr_new = (zr_ref[:, 0:MLA_ROPE] * cos4[:, 0:MLA_ROPE]
              + zr_ref[:, MLA_ROPE:2 * MLA_ROPE] * sin4[:, 0:MLA_ROPE])
    c_ref[...] = c_new
    kr_ref[...] = kr_new
    kcat_ref[:, 0:MLA_KV_RANK] = c_new.astype(kcat_ref.dtype)
    kcat_ref[:, MLA_KV_RANK:MLA_KV_RANK + MLA_ROPE] = kr_new.astype(kcat_ref.dtype)
    kcat_ref[:, MLA_KV_RANK + MLA_ROPE:MLA_QK] = zpad


def mla_prep(z, cos4, sin4, qg, kg, wq, wuk, l, L, cat_dtype):
    n = z.shape[0]
    nt = cos4.shape[0] // L
    lay = lambda shape: pl.BlockSpec((None,) + shape, lambda i: (l,) + (0,) * len(shape))
    tab_spec = pl.BlockSpec((L, MLA_HEADS * MLA_ROPE), lambda i: (i % nt, 0))
    return pl.pallas_call(
        _mla_prep_kernel,
        grid=(n // L,),
        in_specs=[
            pl.BlockSpec((L, MIX_W), lambda i: (i, Z_MQ // MIX_W)),
            pl.BlockSpec((L, 128), lambda i: (i, Z_MKR // 128)),
            tab_spec, tab_spec,
            lay((1, MLA_Q_RANK)), lay((1, MLA_KV_RANK)),
            lay((MLA_Q_RANK, MLA_HEADS * (MLA_NOPE + 2 * MLA_ROPE))),
            lay((MLA_HEADS, MLA_NOPE, MLA_KV_RANK)),
        ],
        out_specs=[
            pl.BlockSpec((MLA_HEADS, L, MLA_QK), lambda i: (0, i, 0)),
            pl.BlockSpec((L, MLA_QK), lambda i: (i, 0)),
            pl.BlockSpec((L, MLA_KV_RANK), lambda i: (i, 0)),
            pl.BlockSpec((L, MLA_ROPE), lambda i: (i, 0)),
        ],
        out_shape=[
            jax.ShapeDtypeStruct((MLA_HEADS, n, MLA_QK), cat_dtype),
            jax.ShapeDtypeStruct((n, MLA_QK), cat_dtype),
            jax.ShapeDtypeStruct((n, MLA_KV_RANK), F32),
            jax.ShapeDtypeStruct((n, MLA_ROPE), F32),
        ],
        compiler_params=_cp(("parallel",)),
        name="mla_prep",
    )(z, z, cos4, sin4, qg, kg, wq, wuk)


NEG_BIG = -1e30


def _causal_mask(tq, tk):
    row = lax.broadcasted_iota(jnp.int32, (MLA_HEADS * tq, tk), 0)
    key = lax.broadcasted_iota(jnp.int32, (MLA_HEADS * tq, tk), 1)
    head = sum((row >= h * tq).astype(jnp.int32) for h in range(1, MLA_HEADS))
    return key <= row - head * tq


def _mla_causal_kernel(q_ref, k_ref, wuv_ref, yd_ref, m_ref, l_ref, acc_ref, *, tb):
    qi = pl.program_id(1)
    rows = MLA_HEADS * tb
    m_ref[...] = jnp.full(m_ref.shape, NEG_BIG, F32)
    l_ref[...] = jnp.zeros(l_ref.shape, F32)
    acc_ref[...] = jnp.zeros(acc_ref.shape, F32)
    q = q_ref[...].reshape(rows, MLA_QK)

    def block(kj, masked):
        k = k_ref[pl.ds(pl.multiple_of(kj * tb, tb), tb), :]
        s = _dot_t(q, k)
        if masked:
            s = jnp.where(_causal_mask(tb, tb), s, NEG_BIG)
        m_old = m_ref[...]
        m_new = jnp.maximum(m_old, jnp.max(s, -1, keepdims=True))
        alpha = jnp.exp(m_old - m_new)
        p = jnp.exp(s - jnp.tile(m_new, (1, tb // 128)))
        l_ref[...] = alpha * l_ref[...] + jnp.sum(p, -1, keepdims=True)
        acc_ref[...] = alpha * acc_ref[...] + _dot(p.astype(BF16), k[:, 0:MLA_KV_RANK])
        m_ref[...] = m_new

    def body(kj, carry):
        block(kj, False)
        return carry

    lax.fori_loop(0, qi, body, 0)
    block(qi, True)
    o = acc_ref[...] / l_ref[...]
    for h in range(MLA_HEADS):
        oh = o[h * tb:(h + 1) * tb, :].astype(BF16)
        yd_ref[:, h * MLA_V:(h + 1) * MLA_V] = _dot(oh, wuv_ref[h]).astype(yd_ref.dtype)


def mla_causal(qcat, kcat, wuv, l, nb, t, tb):
    nq = t // tb
    rows = MLA_HEADS * tb
    return pl.pallas_call(
        functools.partial(_mla_causal_kernel, tb=tb),
        grid=(nb, nq),
        in_specs=[
            pl.BlockSpec((MLA_HEADS, tb, MLA_QK), lambda b, i: (0, b * nq + i, 0)),
            pl.BlockSpec((t, MLA_QK), lambda b, i: (b, 0)),
            pl.BlockSpec((None, MLA_HEADS, MLA_KV_RANK, MLA_V), lambda b, i: (l, 0, 0, 0)),
        ],
        out_specs=pl.BlockSpec((tb, MIX_W), lambda b, i: (b * nq + i, 0)),
        out_shape=jax.ShapeDtypeStruct((nb * t, MIX_W), BF16),
        scratch_shapes=[pltpu.VMEM((rows, 128), F32), pltpu.VMEM((rows, 128), F32),
                        pltpu.VMEM((rows, MLA_KV_RANK), F32)],
        compiler_params=_cp(("parallel", "arbitrary")),
        name="mla_causal",
    )(qcat, kcat, wuv)


PAGED_GROUP = 4


def _mla_paged_kernel(pt_ref, q_ref, kn_ref, cache_c_ref, cache_krt_ref, wuv_ref, yd_ref,
                      cbuf_ref, krbuf_ref, sem_ref, cb16_ref, s_ref, *, layer, n_pages, kchunk):
    g = pl.program_id(0)
    ng = pl.num_programs(0)
    t = SUBLANES
    hr = MLA_HEADS * t
    past = n_pages * PAGE_SIZE
    nchunks = past // kchunk
    ppc = kchunk // PAGE_SIZE

    def page_copies(base, slot, p, page_of):
        out = []
        for ab in range(2):
            page = page_of(base + ab, p)
            out.append(pltpu.make_async_copy(
                cache_c_ref.at[layer, page],
                cbuf_ref.at[slot, pl.ds(p * PAGE_SIZE, PAGE_SIZE), pl.ds(ab * MLA_KV_RANK, MLA_KV_RANK)],
                sem_ref.at[0, slot]))
            out.append(pltpu.make_async_copy(
                cache_krt_ref.at[layer, page],
                krbuf_ref.at[slot, pl.ds(ab * MLA_ROPE, MLA_ROPE), pl.ds(p * PAGE_SIZE, PAGE_SIZE)],
                sem_ref.at[1, slot]))
        return out

    table_page = lambda seq, p: pt_ref[seq * n_pages + p]

    def issue(base, slot, chunk):
        for p in range(chunk * ppc, (chunk + 1) * ppc):
            for cp in page_copies(base, slot, p, table_page):
                cp.start()

    def wait_all(slot):
        for p in range(n_pages):
            for cp in page_copies(0, slot, p, lambda seq, p: 0):
                cp.wait()

    @pl.when(g == 0)
    def _():
        for chunk in range(nchunks):
            issue(0, 0, chunk)

    qf = q_ref[...]
    knf = kn_ref[...]
    mask_new = _causal_mask(t, t)
    next_base = jnp.minimum(PAGED_GROUP * (g + 1), PAGED_GROUP * (ng - 1))

    for half in range(2):
        slot = half
        wait_all(slot)
        fill_base, fill_slot = (PAGED_GROUP * g + 2, 1) if half == 0 else (next_base, 0)
        qs = [qf[:, (2 * half + ab) * t:(2 * half + ab + 1) * t, :].reshape(hr, MLA_QK) for ab in range(2)]
        kns = [knf[(2 * half + ab) * t:(2 * half + ab + 1) * t, :].astype(BF16) for ab in range(2)]
        z_lat = jnp.zeros((hr, MLA_KV_RANK), F32)
        z_rope = jnp.zeros((hr, MLA_ROPE), F32)
        rope = slice(MLA_KV_RANK, MLA_KV_RANK + MLA_ROPE)
        q_lat = jnp.concatenate([jnp.concatenate([qs[0][:, 0:MLA_KV_RANK], z_lat], 1),
                                 jnp.concatenate([z_lat, qs[1][:, 0:MLA_KV_RANK]], 1)], 0).astype(BF16)
        q_rope = jnp.concatenate([jnp.concatenate([qs[0][:, rope], z_rope], 1),
                                  jnp.concatenate([z_rope, qs[1][:, rope]], 1)], 0).astype(BF16)
        for chunk in range(nchunks):
            ks = slice(chunk * kchunk, (chunk + 1) * kchunk)
            cb = cbuf_ref[slot, ks, :].astype(BF16)
            cb16_ref[ks, :] = cb
            krb = krbuf_ref[slot, :, ks].astype(BF16)
            s_ref[:, ks] = _dot_t(q_lat, cb) + _dot(q_rope, krb)
            issue(fill_base, fill_slot, chunk)
        s_new = jnp.concatenate(
            [jnp.where(mask_new, _dot_t(qs[ab].astype(BF16), kns[ab]), NEG_BIG) for ab in range(2)], 0)
        s_past = s_ref[...]
        m = jnp.maximum(jnp.max(s_past, -1, keepdims=True), jnp.max(s_new, -1, keepdims=True))
        p_past = jnp.exp(s_past - m)
        p_new = jnp.exp(s_new - m)
        l = jnp.sum(p_past, -1, keepdims=True) + jnp.sum(p_new, -1, keepdims=True)
        o_pair = _dot(p_past.astype(BF16), cb16_ref[...])
        o = []
        for ab in range(2):
            rs = slice(ab * hr, (ab + 1) * hr)
            o_ab = (o_pair[rs, ab * MLA_KV_RANK:(ab + 1) * MLA_KV_RANK]
                    + _dot(p_new[rs].astype(BF16), kns[ab][:, 0:MLA_KV_RANK]))
            o.append(o_ab / l[rs])
        for h in range(MLA_HEADS):
            oh = jnp.concatenate([o[0][h * t:(h + 1) * t], o[1][h * t:(h + 1) * t]], 0).astype(BF16)
            yd_ref[2 * half * t:(2 * half + 2) * t, h * MLA_V:(h + 1) * MLA_V] = (
                _dot(oh, wuv_ref[h]).astype(yd_ref.dtype))

    @pl.when(g == ng - 1)
    def _():
        wait_all(0)


def mla_paged(qcat, kcat, cache_c, cache_krt, page_table, wuv, layer):
    nb, n_pages = page_table.shape
    assert nb % PAGED_GROUP == 0
    t = SUBLANES
    past = n_pages * PAGE_SIZE
    kchunk = _tile(past, 2048)
    rows = PAGED_GROUP * t
    grid_spec = pltpu.PrefetchScalarGridSpec(
        num_scalar_prefetch=1,
        grid=(nb // PAGED_GROUP,),
        in_specs=[
            pl.BlockSpec((MLA_HEADS, rows, MLA_QK), lambda g, pt: (0, g, 0)),
            pl.BlockSpec((rows, MLA_QK), lambda g, pt: (g, 0)),
            pl.BlockSpec(memory_space=pl.ANY),
            pl.BlockSpec(memory_space=pl.ANY),
            pl.BlockSpec((None, MLA_HEADS, MLA_KV_RANK, MLA_V), lambda g, pt: (layer, 0, 0, 0)),
        ],
        out_specs=pl.BlockSpec((rows, MIX_W), lambda g, pt: (g, 0)),
        scratch_shapes=[
            pltpu.VMEM((2, past, 2 * MLA_KV_RANK), F32),
            pltpu.VMEM((2, 2 * MLA_ROPE, past), F32),
            pltpu.SemaphoreType.DMA((2, 2)),
            pltpu.VMEM((past, 2 * MLA_KV_RANK), BF16),
            pltpu.VMEM((2 * MLA_HEADS * t, past), F32),
        ],
    )
    return pl.pallas_call(
        functools.partial(_mla_paged_kernel, layer=layer, n_pages=n_pages, kchunk=kchunk),
        grid_spec=grid_spec,
        out_shape=jax.ShapeDtypeStruct((nb * t, MIX_W), F32),
        compiler_params=_cp(("arbitrary",)),
        name="mla_paged",
    )(page_table.reshape(-1), qcat, kcat, cache_c, cache_krt, wuv)


def _merge_kernel(x_ref, *refs):
    brs, wgs, bgs = refs[0:N_BRANCH], refs[N_BRANCH:2 * N_BRANCH], refs[2 * N_BRANCH:3 * N_BRANCH]
    wb_ref, o_ref = refs[3 * N_BRANCH:]
    x = x_ref[...]
    acc = None
    for g in range(N_BRANCH):
        gate = _sigmoid(_dot_t(x, wgs[g][...]) + bgs[g][...])
        term = _dot(brs[g][...].astype(BF16), wb_ref[g]) * gate
        acc = term if acc is None else acc + term
    o_ref[...] = acc.astype(o_ref.dtype)


def merge(xb, ys, wg, bg, wb, l, tm, tn):
    n, d = xb.shape
    nj = d // tn
    row = lambda shape: pl.BlockSpec(shape, lambda j, i: (i, 0))
    gate_w = [pl.BlockSpec((None, tn, d), lambda j, i, g=g: (l, g * nj + j, 0)) for g in range(N_BRANCH)]
    gate_b = [pl.BlockSpec((None, 1, tn), lambda j, i, g=g: (l, 0, g * nj + j)) for g in range(N_BRANCH)]
    return pl.pallas_call(
        _merge_kernel,
        grid=(nj, n // tm),
        in_specs=[row((tm, d))] + [row((tm, MIX_W))] * N_BRANCH + gate_w + gate_b
        + [pl.BlockSpec((None, N_BRANCH, MIX_W, tn), lambda j, i: (l, 0, 0, j))],
        out_specs=pl.BlockSpec((tm, tn), lambda j, i: (i, j)),
        out_shape=jax.ShapeDtypeStruct((n, d), BF16),
        compiler_params=_cp(("parallel", "parallel")),
        name="merge",
    )(xb, *ys, *([wg] * N_BRANCH), *([bg] * N_BRANCH), wb)


def _out_ln_kernel(m_ref, w_ref, x_ref, g_ref, b_ref, o_ref, *, alpha):
    y = alpha * x_ref[...] + _dot(m_ref[...], w_ref[...])
    o_ref[...] = _layer_norm(y, g_ref[...], b_ref[...])


def out_ln(merged, w_out, x, g, b, l, alpha, tm):
    n, d = x.shape
    row = pl.BlockSpec((tm, d), lambda i: (i, 0))
    lay = lambda shape: pl.BlockSpec((None,) + shape, lambda i: (l, 0, 0))
    return pl.pallas_call(
        functools.partial(_out_ln_kernel, alpha=alpha),
        grid=(n // tm,),
        in_specs=[row, lay((d, d)), row, lay((1, d)), lay((1, d))],
        out_specs=row,
        out_shape=jax.ShapeDtypeStruct((n, d), F32),
        compiler_params=_cp(("parallel",)),
        name="out_ln",
    )(merged, w_out, x, g, b)


FFN_HALO = 16


def _ffn_kernel(x_ref, halo_ref, wh_ref, wg_ref, wd_ref, cw_ref, cb_ref, st_ref, g_ref, b_ref,
                o_ref, ob_ref, hl_ref, xcat_ref, hs_ref, acc_ref, *, tm, alpha, chained, tiles_per_seq):
    i = pl.program_id(0)
    f = pl.program_id(1)

    @pl.when(f == 0)
    def _():
        acc_ref[...] = jnp.zeros(acc_ref.shape, F32)
        xcat_ref[FFN_HALO:, :] = x_ref[...].astype(BF16)
        if chained:
            seq_start = (i % tiles_per_seq) == 0
            xcat_ref[0:FFN_HALO, :] = jnp.where(seq_start, 0.0, halo_ref[...]).astype(BF16)

    w0 = cw_ref[0:1, :]
    w1 = cw_ref[1:2, :]
    w2 = cw_ref[2:3, :]
    if chained:
        hs_ref[...] = _dot(xcat_ref[...], wh_ref[...])
        h = hs_ref[FFN_HALO:, :]
        hm1 = hs_ref[pl.ds(FFN_HALO - 1, tm), :]
        hm2 = hs_ref[pl.ds(FFN_HALO - 2, tm), :]
        hl_ref[0] = hs_ref[tm + FFN_HALO - SUBLANES:, :]
        conv = cb_ref[...] + w0 * hm2 + w1 * hm1 + w2 * h
    else:
        t = SUBLANES
        nseq = tm // t
        tf = wh_ref.shape[1]
        h = _dot(xcat_ref[FFN_HALO:, :], wh_ref[...]).reshape(nseq, t, tf)
        r = lax.broadcasted_iota(jnp.int32, (nseq, t, tf), 1)
        b0 = st_ref[:, 0:1, :]
        b1 = st_ref[:, 1:2, :]
        hm1 = jnp.where(r >= 1, pltpu.roll(h, 1, 1), b1)
        hm2 = jnp.where(r >= 2, pltpu.roll(h, 2, 1), jnp.where(r == 1, b1, b0))
        hl_ref[...] = h[:, t - (CONV_W - 1):, :]
        conv = (cb_ref[...] + w0 * hm2 + w1 * hm1 + w2 * h).reshape(tm, tf)
    gate = _dot(xcat_ref[FFN_HALO:, :], wg_ref[...])
    a = (_gelu_tanh(conv) * gate).astype(BF16)
    acc_ref[...] += _dot(a, wd_ref[...])

    @pl.when(f == pl.num_programs(1) - 1)
    def _():
        y = _layer_norm(alpha * x_ref[...] + acc_ref[...], g_ref[...], b_ref[...])
        o_ref[...] = y
        ob_ref[...] = y.astype(ob_ref.dtype)


def conv_ffn(x, w_up, w_down, cw, cb, g, b, l, state, ls, alpha, nb, t, chained, tm, tf):
    n, d = x.shape
    ff = w_down.shape[1]
    nf = ff // tf
    lay = lambda shape: pl.BlockSpec((None,) + shape, lambda i, f: (l, 0, 0))
    hb = tm // FFN_HALO
    if chained:
        tiles_per_seq = t // tm
        st = jnp.zeros((1, 1, CONV_W - 1, tf), F32)
        st_spec = pl.BlockSpec((None, 1, CONV_W - 1, tf), lambda i, f: (0, 0, 0, 0))
        hl_shape = jax.ShapeDtypeStruct((n // tm, SUBLANES, ff), F32)
        hl_spec = pl.BlockSpec((1, SUBLANES, tf), lambda i, f: (i, 0, f))
    else:
        assert t == SUBLANES
        tiles_per_seq = 1
        st = state
        st_spec = pl.BlockSpec((None, tm // t, CONV_W - 1, tf), lambda i, f: (ls, i, 0, f))
        hl_shape = jax.ShapeDtypeStruct((nb, CONV_W - 1, ff), F32)
        hl_spec = pl.BlockSpec((tm // t, CONV_W - 1, tf), lambda i, f: (i, 0, f))
    o, ob, hl = pl.pallas_call(
        functools.partial(_ffn_kernel, tm=tm, alpha=alpha, chained=chained, tiles_per_seq=tiles_per_seq),
        grid=(n // tm, nf),
        in_specs=[
            pl.BlockSpec((tm, d), lambda i, f: (i, 0)),
            pl.BlockSpec((FFN_HALO, d), lambda i, f: (jnp.maximum(i * hb - 1, 0), 0)),
            pl.BlockSpec((None, d, tf), lambda i, f: (l, 0, f)),
            pl.BlockSpec((None, d, tf), lambda i, f: (l, 0, nf + f)),
            pl.BlockSpec((None, tf, d), lambda i, f: (l, f, 0)),
            pl.BlockSpec((None, CONV_W, tf), lambda i, f: (l, 0, f)),
            pl.BlockSpec((None, 1, tf), lambda i, f: (l, 0, f)),
            st_spec, lay((1, d)), lay((1, d)),
        ],
        out_specs=[pl.BlockSpec((tm, d), lambda i, f: (i, 0)), pl.BlockSpec((tm, d), lambda i, f: (i, 0)), hl_spec],
        out_shape=[jax.ShapeDtypeStruct((n, d), F32), jax.ShapeDtypeStruct((n, d), BF16), hl_shape],
        scratch_shapes=[pltpu.VMEM((tm + FFN_HALO, d), BF16), pltpu.VMEM((tm + FFN_HALO, tf), F32),
                        pltpu.VMEM((tm, d), F32)],
        compiler_params=_cp(("parallel", "arbitrary")),
        name="conv_ffn",
    )(x, x, w_up, w_up, w_down, cw, cb, st, g, b)
    if chained:
        hl = hl[tiles_per_seq - 1::tiles_per_seq, SUBLANES - (CONV_W - 1):, :]
    return o, ob, hl


def _rope_tables(pos0, t):
    half = MLA_ROPE // 2
    inv = ROPE_THETA ** (-jnp.arange(half, dtype=F32) / half)
    ang = (pos0 + jnp.arange(t)).astype(F32)[:, None] * inv
    cos, sin = jnp.cos(ang), jnp.sin(ang)
    cosf = jnp.concatenate([cos, cos], -1)
    sinf = jnp.concatenate([-sin, sin], -1)
    return jnp.tile(cosf, (1, MLA_HEADS)), jnp.tile(sinf, (1, MLA_HEADS))


def _rot_half_cols(w):
    half = w.shape[-1] // 2
    return jnp.concatenate([w[..., half:], w[..., :half]], -1)


IN_SIZES = (GLA_HEADS * GLA_DK, GLA_HEADS * GLA_DK, GLA_HEADS * GLA_DV, MIX_W, GLA_GATE_RANK,
            MIX_W, MIX_W, MLA_Q_RANK, MLA_KV_RANK, MLA_ROPE)
IN_GATE_OFF = sum(IN_SIZES)
IN_DEST = (Z_Q, Z_K, Z_V, Z_OG, Z_GA, Z_SU, Z_PU, Z_MQ, Z_MKV, Z_MKR)


def _prep_weights(p):
    depth, d_model, _ = p['w_in'].shape
    hk = GLA_HEADS * GLA_DK
    w_t = jnp.swapaxes(p['w_in'], 1, 2)
    groups, off = {}, 0
    for size, dst in zip(IN_SIZES, IN_DEST):
        groups[dst] = w_t[:, off:off + size, :]
        off += size
    half = MLA_ROPE // 2
    mkr = groups[Z_MKR]
    groups[Z_MKR_ROT] = jnp.concatenate([mkr[:, half:], mkr[:, :half]], 1)
    pad0 = Z_GA + GLA_GATE_RANK
    groups[pad0] = jnp.zeros((depth, Z_COLS - pad0, d_model), F32)
    w_z = jnp.concatenate([groups[k] for k in sorted(groups)], 1).astype(BF16)
    w_gate = w_t[:, IN_GATE_OFF:, :].astype(BF16)
    out = dict(w_z=w_z, w_gate=w_gate, b_gate=p['b_gates'].reshape(depth, 1, N_BRANCH * d_model))
    out['gla_wg'] = jnp.zeros((depth, 128, hk), F32).at[:, :GLA_GATE_RANK].set(p['w_gla_gate'])
    out['gla_bg'] = p['b_gla_gate'].reshape(depth, 1, hk)
    out['gla_ng'] = p['gla_norm_g'].reshape(depth, 1, MIX_W)

    g2 = S5_GROUPS // 2
    eye = jnp.eye(g2, dtype=F32)

    def blockdiag_in(bm):
        bm = bm.reshape(depth, 2, g2, S5_STATE, S5_GROUP)
        w = jnp.einsum('lhgnj,gk->lhgjkn', bm, eye)
        return w.reshape(depth, 2, g2 * S5_GROUP, g2 * S5_STATE).astype(BF16)

    def blockdiag_out(cm):
        cm = cm.reshape(depth, 2, g2, S5_GROUP, S5_STATE)
        w = jnp.einsum('lhgjn,gk->lhgnkj', cm, eye)
        return w.reshape(depth, 2, g2 * S5_STATE, g2 * S5_GROUP).astype(BF16)

    out['s5_wbr'], out['s5_wbi'] = blockdiag_in(p['s5_b_re']), blockdiag_in(p['s5_b_im'])
    out['s5_wcr'], out['s5_wci'] = blockdiag_out(p['s5_c_re']), blockdiag_out(p['s5_c_im'])
    out['s5_d'] = p['s5_d'].reshape(depth, 1, MIX_W)
    out['s5_wglu'] = p['w_s5_glu'].astype(BF16)
    out['s5_bglu'] = p['b_s5_glu'].reshape(depth, 1, MIX_W)
    out['pool_w'] = p['w_pool'].astype(BF16)
    out['pool_sc'] = p['pool_scale'].reshape(depth, 1, MIX_W)
    wq = p['w_q_up'].reshape(depth, MLA_Q_RANK, MLA_HEADS, MLA_NOPE + MLA_ROPE)
    flat = lambda w: w.reshape(depth, MLA_Q_RANK, -1)
    wq_rope = wq[..., MLA_NOPE:]
    out['mla_wq'] = jnp.concatenate(
        [flat(wq[..., :MLA_NOPE]), flat(wq_rope), flat(_rot_half_cols(wq_rope))], 2).astype(BF16)
    wkv = p['w_kv_up'].reshape(depth, MLA_KV_RANK, MLA_HEADS, MLA_NOPE + MLA_V)
    out['mla_wuk'] = wkv[..., :MLA_NOPE].transpose(0, 2, 3, 1).astype(BF16)
    out['mla_wuv'] = wkv[..., MLA_NOPE:].transpose(0, 2, 1, 3).astype(BF16)
    out['mla_qg'] = p['mla_q_norm'].reshape(depth, 1, MLA_Q_RANK)
    out['mla_kg'] = p['mla_kv_norm'].reshape(depth, 1, MLA_KV_RANK)
    out['w_branch'] = p['w_branch'].astype(BF16)
    out['w_out'] = p['w_out'].astype(BF16)
    out['w_up'] = p['w_ffn_up'].astype(BF16)
    out['w_down'] = p['w_ffn_down'].astype(BF16)
    out['conv_w'] = p['ffn_conv_w']
    out['conv_b'] = p['ffn_conv_b'].reshape(depth, 1, -1)
    for nm in ('ln1_g', 'ln1_b', 'ln2_g', 'ln2_b'):
        out[nm] = p[nm].reshape(depth, 1, d_model)
    return out


def _layer(x, xb, w, tabs, st, nb, t, pos0, l, ls, cache, alpha, chained):
    n, d = x.shape
    z = matmul(xb, w['w_z'], l, _tile(n, 1024), Z_COLS // 2)
    gla_w = (w['gla_wg'], w['gla_bg'], w['gla_ng'], l)
    s5_w = (tabs[0], tabs[1], w['s5_wbr'], w['s5_wbi'], w['s5_wcr'], w['s5_wci'], w['s5_d'], w['s5_wglu'],
            w['s5_bglu'], l)
    cos4, sin4 = _rope_tables(pos0, t)
    mla_w = (w['mla_qg'], w['mla_kg'], w['mla_wq'], w['mla_wuk'], l)
    if chained:
        gl = _tile(t, 64)
        ya, s_gla = gla(z, *gla_w, st['gla'], ls, nb, t, gl, _tile(t // gl, 4), True)
        yb, h_re, h_im = s5(z, *s5_w, st['s5_re'], st['s5_im'], ls, nb, t, True, SUBLANES * S5_NJ_CHAIN)
        yc, pbuf = pool(z, w['pool_w'], w['pool_sc'], l, st['pool'], ls, nb, t, pos0, True, _tile(t, 256))
        qcat, kcat, c_new, kr_new = mla_prep(z, cos4, sin4, *mla_w, _tile(t, 256), BF16)
        yd = mla_causal(qcat, kcat, w['mla_wuv'], l, nb, t, _tile(t, 512))
    else:
        ya, s_gla = gla(z, *gla_w, st['gla'], ls, nb, t, t, _tile(nb, 8), False)
        yb, h_re, h_im = s5(z, *s5_w, st['s5_re'], st['s5_im'], ls, nb, t, False, _tile(n, 256))
        yc, pbuf = pool(z, w['pool_w'], w['pool_sc'], l, st['pool'], ls, nb, t, pos0, False, _tile(n, 128))
        prep_rows = _tile(n, 128)
        tile_rows = lambda a: jnp.tile(a, (prep_rows // t, 1))
        qcat, kcat, c_new, kr_new = mla_prep(z, tile_rows(cos4), tile_rows(sin4), *mla_w, prep_rows, F32)
        yd = mla_paged(qcat, kcat, cache[0], cache[1], cache[2], w['mla_wuv'], l)
    merged = merge(xb, (ya, yb, yc, yd), w['w_gate'], w['b_gate'], w['w_branch'], l, _tile(n, 1024), _tile(d, 512))
    x1 = out_ln(merged, w['w_out'], x, w['ln1_g'], w['ln1_b'], l, alpha, _tile(n, 512))
    ff = w['w_down'].shape[1]
    tm = _tile(t, 512) if chained else _tile(n, 512)
    x2, x2b, cbuf = conv_ffn(x1, w['w_up'], w['w_down'], w['conv_w'], w['conv_b'], w['ln2_g'], w['ln2_b'], l,
                             st['conv'], ls, alpha, nb, t, chained, tm, _tile(ff, 512))
    new = (c_new.reshape(nb, t, MLA_KV_RANK), kr_new.reshape(nb, t, MLA_ROPE), s_gla, h_re, h_im, pbuf, cbuf)
    return x2, x2b, new


def _trunk(x3, w, tabs, states, shared_state, pos0, cache, alpha, chained):
    nb, t, d = x3.shape
    depth = w['w_z'].shape[0]
    x = x3.reshape(nb * t, d)
    xb = x.astype(BF16)
    outs = [[] for _ in range(7)]
    for l in range(depth):
        x, xb, new = _layer(x, xb, w, tabs, states, nb, t, pos0, l, 0 if shared_state else l, cache, alpha,
                            chained)
        for lst, val in zip(outs, new):
            lst.append(val)
    return (x.reshape(nb, t, d),) + tuple(jnp.stack(o, 0) for o in outs)


def kernel(x_prompt, x_sample, cache_kv_latent, cache_k_rope, page_table, state_gla, state_s5_re, state_s5_im,
           state_pool, state_ffn_conv, ln1_g, ln1_b, w_in, b_gates, w_gla_gate, b_gla_gate, gla_norm_g,
           s5_a_re, s5_a_im, s5_log_dt, s5_b_re, s5_b_im, s5_c_re, s5_c_im, s5_d, w_s5_glu, b_s5_glu,
           w_pool, pool_scale, mla_q_norm, w_q_up, mla_kv_norm, w_kv_up, w_branch, w_out, ln2_g, ln2_b,
           w_ffn_up, ffn_conv_w, ffn_conv_b, w_ffn_down):
    params = {'ln1_g': ln1_g, 'ln1_b': ln1_b, 'w_in': w_in, 'b_gates': b_gates, 'w_gla_gate': w_gla_gate,
              'b_gla_gate': b_gla_gate, 'gla_norm_g': gla_norm_g, 's5_b_re': s5_b_re, 's5_b_im': s5_b_im,
              's5_c_re': s5_c_re, 's5_c_im': s5_c_im, 's5_d': s5_d, 'w_s5_glu': w_s5_glu, 'b_s5_glu': b_s5_glu,
              'w_pool': w_pool, 'pool_scale': pool_scale, 'mla_q_norm': mla_q_norm, 'w_q_up': w_q_up,
              'mla_kv_norm': mla_kv_norm, 'w_kv_up': w_kv_up, 'w_branch': w_branch, 'w_out': w_out,
              'ln2_g': ln2_g, 'ln2_b': ln2_b, 'w_ffn_up': w_ffn_up, 'ffn_conv_w': ffn_conv_w,
              'ffn_conv_b': ffn_conv_b, 'w_ffn_down': w_ffn_down}
    depth = w_in.shape[0]
    d_ff = ffn_conv_w.shape[-1]
    alpha = (2 * depth) ** 0.25
    w = _prep_weights(params)
    tabs = s5_tables(s5_a_re, s5_a_im, s5_log_dt, S5_NJ_CHAIN)
    pad_pool = lambda a: jnp.pad(a, ((0, 0), (0, 0), (1, 0), (0, 0)))

    nb_p = x_prompt.shape[0]
    zeros = lambda *shape: jnp.zeros((1, nb_p) + shape, F32)
    st_p = dict(gla=zeros(GLA_HEADS, GLA_DK, GLA_DV), s5_re=zeros(S5_W), s5_im=zeros(S5_W),
                pool=zeros(HALO, MIX_W), conv=None)
    res_p = _trunk(x_prompt, w, tabs, st_p, True, 0, None, alpha, True)

    nb_s = x_sample.shape[0]
    past_len = page_table.shape[1] * PAGE_SIZE
    st_s = dict(gla=state_gla, s5_re=state_s5_re.reshape(depth, nb_s, S5_W),
                s5_im=state_s5_im.reshape(depth, nb_s, S5_W), pool=pad_pool(state_pool), conv=state_ffn_conv)
    cache_krt = jnp.swapaxes(cache_k_rope, 2, 3)
    res_s = _trunk(x_sample, w, tabs, st_s, False, past_len, (cache_kv_latent, cache_krt, page_table), alpha, False)
    return (res_p[0], res_s[0]) + res_p[1:] + res_s[1:]
```

```python
import functools
import math

import jax
import jax.numpy as jnp
from jax import lax
from jax.experimental import pallas as pl
from jax.experimental.pallas import tpu as pltpu

F32 = jnp.float32
BF16 = jnp.bfloat16
HIGHEST = lax.Precision.HIGHEST

MIX_W = 512
N_BRANCH = 4
GLA_HEADS, GLA_DK, GLA_DV, GLA_GATE_RANK, GLA_TAU = 4, 64, 128, 16, 16.0
S5_GROUP, S5_GROUPS, S5_STATE = 16, 32, 64
S5_W = S5_GROUPS * S5_STATE
POOL_WINDOWS = (2, 4, 8, 16)
POOL_GROUP = 128
POOL_BUF = 15
MLA_HEADS, MLA_NOPE, MLA_ROPE, MLA_V, MLA_Q_RANK, MLA_KV_RANK = 4, 128, 64, 128, 384, 128
MLA_SCALE = (MLA_NOPE + MLA_ROPE) ** -0.5
MLA_QK = 256
ROPE_THETA = 10000.0
PAGE_SIZE = 128
CONV_W = 3
LN_EPS = 1e-5
RMS_EPS = 1e-6

Z_Q, Z_K, Z_V, Z_OG, Z_SU, Z_PU, Z_MQ, Z_MKV, Z_MKR, Z_MKR_ROT, Z_GA = (
    0, 256, 512, 1024, 1536, 2048, 2560, 2944, 3072, 3136, 3200)
Z_COLS = 3328

V7X_VMEM_BYTES = 64 * 1024 * 1024
VMEM_LIMIT = 48 * 1024 * 1024
SUBLANES = 8


def _cp(sem, vmem=VMEM_LIMIT):
    return pltpu.CompilerParams(dimension_semantics=sem, vmem_limit_bytes=vmem)


def _tile(dim, pref):
    t = min(dim, pref)
    while dim % t:
        t //= 2
    return t


def _sigmoid(x):
    return 1.0 / (1.0 + jnp.exp(-x))


def _gelu_tanh(x):
    c = math.sqrt(2.0 / math.pi)
    return 0.5 * x * (1.0 + jnp.tanh(c * (x + 0.044715 * (x * x * x))))


def _log_sigmoid(x):
    return jnp.minimum(x, 0.0) - jnp.log(1.0 + jnp.exp(-jnp.abs(x)))


def _layer_norm(y, g, b):
    mu = jnp.mean(y, -1, keepdims=True)
    d = y - mu
    var = jnp.mean(d * d, -1, keepdims=True)
    return d * lax.rsqrt(var + LN_EPS) * g + b


def _rms(x, g):
    return x * lax.rsqrt(jnp.mean(x * x, -1, keepdims=True) + RMS_EPS) * g


def _dot(a, b):
    return jnp.dot(a, b, preferred_element_type=F32)


def _dot_t(a, b):
    return lax.dot_general(a, b, (((1,), (1,)), ((), ())), preferred_element_type=F32)


def _dot_ta(a, b, **kw):
    return lax.dot_general(a, b, (((0,), (0,)), ((), ())), preferred_element_type=F32, **kw)


def _cmul(ar, ai, br, bi):
    return ar * br - ai * bi, ar * bi + ai * br


def _mm_kernel(x_ref, w_ref, o_ref):
    o_ref[...] = _dot_t(x_ref[...], w_ref[...]).astype(o_ref.dtype)


def matmul(x, w, l, tm, tn, out_dtype=F32):
    m, k = x.shape
    n = w.shape[1]
    return pl.pallas_call(
        _mm_kernel,
        grid=(m // tm, n // tn),
        in_specs=[pl.BlockSpec((tm, k), lambda i, j: (i, 0)),
                  pl.BlockSpec((None, tn, k), lambda i, j: (l, j, 0))],
        out_specs=pl.BlockSpec((tm, tn), lambda i, j: (i, j)),
        out_shape=jax.ShapeDtypeStruct((m, n), out_dtype),
        compiler_params=_cp(("parallel", "parallel")),
        name="in_proj",
    )(x, w)


GLA_SAFE_LOGDECAY = 60.0


def _gla_kernel(q_ref, k_ref, v_ref, og_ref, ga_ref, wg_ref, bg_ref, ng_ref, s0_ref,
                ya_ref, st_ref, la_ref, o_ref, *, L, nsub, chain):
    R = nsub * L
    assert chain or L == SUBLANES

    @pl.when(pl.program_id(1) == 0)
    def _():
        st_ref[...] = s0_ref[...]

    pre = jnp.dot(ga_ref[...], wg_ref[...], precision=HIGHEST, preferred_element_type=F32) + bg_ref[...]
    log_a = _log_sigmoid(pre) * (1.0 / GLA_TAU)
    hk = GLA_HEADS * GLA_DK
    blk_of = lambda x: sum((x >= j * L).astype(jnp.int32) for j in range(1, nsub))
    row = lax.broadcasted_iota(jnp.int32, (R, R), 0)
    col = lax.broadcasted_iota(jnp.int32, (R, R), 1)
    tri = jnp.where(row >= col, 1.0, 0.0)
    if nsub > 1:
        tri = jnp.where(blk_of(row) == blk_of(col), tri, 0.0)
    causal = tri > 0.5
    a_hi = log_a.astype(BF16)
    rem = log_a - a_hi.astype(F32)
    a_mid = rem.astype(BF16)
    a_lo = (rem - a_mid.astype(F32)).astype(BF16)
    a3 = jnp.concatenate([a_hi, a_mid, a_lo], 1)
    sum3 = lambda m, ax: (lax.slice_in_dim(m, 0, hk, axis=ax) + lax.slice_in_dim(m, hk, 2 * hk, axis=ax)
                          + lax.slice_in_dim(m, 2 * hk, 3 * hk, axis=ax))
    bc = sum3(_dot(tri.astype(BF16), a3), 1)
    if nsub > 1:
        rowb = lax.broadcasted_iota(jnp.int32, (R, nsub * GLA_DV), 0)
        colb = lax.broadcasted_iota(jnp.int32, (R, nsub * GLA_DV), 1)
        blk_col = sum((colb >= j * GLA_DV).astype(jnp.int32) for j in range(1, nsub))
        member = jnp.where(blk_of(rowb) == blk_col, 1.0, 0.0).astype(BF16)
    else:
        member = jnp.ones((R, GLA_DV), BF16)
    a_tot_all = jnp.exp(sum3(_dot_ta(a3, member), 0))
    tot = jnp.concatenate([jnp.broadcast_to(bc[(j + 1) * L - 1:(j + 1) * L, :], (L, hk)) for j in range(nsub)], 0)
    scale = GLA_DK ** -0.5
    safe = jnp.max(jnp.abs(bc)) < GLA_SAFE_LOGDECAY
    la_ref[...] = log_a

    @pl.when(safe)
    def _():
        q = q_ref[...] * scale
        k = k_ref[...]
        q_in = q * jnp.exp(bc)
        k_in = k * jnp.exp(-bc)
        k_out = k * jnp.exp(tot - bc)
        for h in range(GLA_HEADS):
            ks = slice(h * GLA_DK, (h + 1) * GLA_DK)
            vs = slice(h * GLA_DV, (h + 1) * GLA_DV)
            v = v_ref[:, vs]
            scores = jnp.where(causal, _dot_t(q_in[:, ks].astype(BF16), k_in[:, ks].astype(BF16)), 0.0)
            o = _dot(scores.astype(BF16), v.astype(BF16))
            o_inter = []
            state = st_ref[0, h]
            for j in range(nsub):
                rs = slice(j * L, (j + 1) * L)
                if not chain:
                    state = st_ref[j, h]
                o_inter.append(_dot(q_in[rs, ks].astype(BF16), state.astype(BF16)))
                a_tot = a_tot_all[ks, j * GLA_DV:(j + 1) * GLA_DV]
                state = a_tot * state + _dot_ta(k_out[rs, ks].astype(BF16), v[rs].astype(BF16))
                if not chain:
                    st_ref[j, h] = state
            if chain:
                st_ref[0, h] = state
            o_ref[:, vs] = o + (o_inter[0] if nsub == 1 else jnp.concatenate(o_inter, 0))

    @pl.when(jnp.logical_not(safe))
    def _():
        rr = lax.broadcasted_iota(jnp.int32, (SUBLANES, SUBLANES * GLA_DV), 0)
        cc = lax.broadcasted_iota(jnp.int32, (SUBLANES, SUBLANES * GLA_DV), 1)
        spread = jnp.where(sum((cc >= r * GLA_DV).astype(jnp.int32) for r in range(1, SUBLANES)) == rr, 1.0, 0.0)
        columns = lambda x8: _dot_ta(x8, spread, precision=HIGHEST)
        for h in range(GLA_HEADS):
            ks = slice(h * GLA_DK, (h + 1) * GLA_DK)
            vs = slice(h * GLA_DV, (h + 1) * GLA_DV)
            def eight_tokens(i8, state):
                rows = pl.ds(pl.multiple_of(i8 * SUBLANES, SUBLANES), SUBLANES)
                a_cols = jnp.exp(columns(la_ref[rows, ks]))
                k_cols = columns(k_ref[rows, ks])
                q_cols = columns(q_ref[rows, ks] * scale)
                v8 = v_ref[rows, vs]
                outs = []
                for r in range(SUBLANES):
                    lanes = slice(r * GLA_DV, (r + 1) * GLA_DV)
                    state = a_cols[:, lanes] * state + k_cols[:, lanes] * v8[r:r + 1, :]
                    outs.append(jnp.sum(q_cols[:, lanes] * state, 0, keepdims=True))
                o_ref[rows, vs] = jnp.concatenate(outs, 0)
                return state

            if chain:
                st_ref[0, h] = lax.fori_loop(0, R // SUBLANES, eight_tokens, st_ref[0, h])
            else:
                def one_sequence(j, carry):
                    st_ref[j, h] = eight_tokens(j, st_ref[j, h])
                    return carry

                lax.fori_loop(0, nsub, one_sequence, 0)

    for h in range(GLA_HEADS):
        vs = slice(h * GLA_DV, (h + 1) * GLA_DV)
        o = o_ref[:, vs]
        o = o * lax.rsqrt(jnp.mean(o * o, -1, keepdims=True) + RMS_EPS) * ng_ref[:, vs]
        g = og_ref[:, vs]
        ya_ref[:, vs] = (o * (g * _sigmoid(g))).astype(ya_ref.dtype)


def gla(z, wg_pad, bg, ng, l, s0, ls, nb, t, L, nsub, chain):
    R = nsub * L
    if chain:
        nc = t // R
        grid = (nb, nc)
        nst = 1
    else:
        assert L == t
        nc = 1
        grid = (nb // nsub, 1)
        nst = nsub
    rb = lambda b, c: b * nc + c
    hk = GLA_HEADS * GLA_DK
    st_blk = (nst, GLA_HEADS, GLA_DK, GLA_DV)
    return pl.pallas_call(
        functools.partial(_gla_kernel, L=L, nsub=nsub, chain=chain),
        grid=grid,
        in_specs=[
            pl.BlockSpec((R, hk), lambda b, c: (rb(b, c), Z_Q // hk)),
            pl.BlockSpec((R, hk), lambda b, c: (rb(b, c), Z_K // hk)),
            pl.BlockSpec((R, MIX_W), lambda b, c: (rb(b, c), Z_V // MIX_W)),
            pl.BlockSpec((R, MIX_W), lambda b, c: (rb(b, c), Z_OG // MIX_W)),
            pl.BlockSpec((R, 128), lambda b, c: (rb(b, c), Z_GA // 128)),
            pl.BlockSpec((None, 128, hk), lambda b, c: (l, 0, 0)),
            pl.BlockSpec((None, 1, hk), lambda b, c: (l, 0, 0)),
            pl.BlockSpec((None, 1, MIX_W), lambda b, c: (l, 0, 0)),
            pl.BlockSpec((None,) + st_blk, lambda b, c: (ls, b, 0, 0, 0)),
        ],
        out_specs=[pl.BlockSpec((R, MIX_W), lambda b, c: (rb(b, c), 0)),
                   pl.BlockSpec(st_blk, lambda b, c: (b, 0, 0, 0))],
        out_shape=[jax.ShapeDtypeStruct((nb * t, MIX_W), BF16),
                   jax.ShapeDtypeStruct((nb, GLA_HEADS, GLA_DK, GLA_DV), F32)],
        scratch_shapes=[pltpu.VMEM((R, hk), F32), pltpu.VMEM((R, MIX_W), F32)],
        compiler_params=_cp(("parallel", "arbitrary")),
        name="gla",
    )(z, z, z, z, z, wg_pad, bg, ng, s0)


S5_NJ_CHAIN = 32
TAB_A, TAB_T1, TAB_T2, TAB_T4, TAB_F = 0, 2, 4, 6, 8
N_TAB = 10
S5_LANES = 512


def _s5_tab_kernel(are_ref, aim_ref, ldt_ref, tab_ref, pj_ref, *, nj):
    lr = are_ref[0]
    li = aim_ref[0]
    dt = jnp.exp(ldt_ref[0])
    mag = jnp.exp(lr * dt)
    a_re = mag * jnp.cos(li * dt)
    a_im = mag * jnp.sin(li * dt)
    shp = (SUBLANES, S5_W)
    r = lax.broadcasted_iota(jnp.int32, shp, 0)
    tab_ref[0, TAB_A] = jnp.broadcast_to(a_re, shp)
    tab_ref[0, TAB_A + 1] = jnp.broadcast_to(a_im, shp)
    pr, pi = a_re, a_im
    for j in range(nj):
        pj_ref[0, 0, j * SUBLANES:(j + 1) * SUBLANES, :] = jnp.broadcast_to(pr, shp)
        pj_ref[0, 1, j * SUBLANES:(j + 1) * SUBLANES, :] = jnp.broadcast_to(pi, shp)
        if j + 1 < nj:
            pr, pi = _cmul(pr, pi, a_re, a_im)
    br, bi = pr, pi
    for s, idx in ((1, TAB_T1), (2, TAB_T2), (4, TAB_T4)):
        tab_ref[0, idx] = jnp.where(r >= s, br, 0.0)
        tab_ref[0, idx + 1] = jnp.where(r >= s, bi, 0.0)
        br, bi = _cmul(br, bi, br, bi)
    den = lr * lr + li * li
    nr, ni = a_re - 1.0, a_im
    tab_ref[0, TAB_F] = jnp.broadcast_to((nr * lr + ni * li) / den, shp)
    tab_ref[0, TAB_F + 1] = jnp.broadcast_to((ni * lr - nr * li) / den, shp)


def s5_tables(a_re, a_im, log_dt, nj):
    depth = a_re.shape[0]
    flat = lambda a: a.reshape(depth, 1, S5_W)
    ldt = jnp.broadcast_to(log_dt[:, :, None], (depth, S5_GROUPS, S5_STATE))
    spec = pl.BlockSpec((1, 1, S5_W), lambda l: (l, 0, 0))
    return pl.pallas_call(
        functools.partial(_s5_tab_kernel, nj=nj),
        grid=(depth,),
        in_specs=[spec, spec, spec],
        out_specs=[pl.BlockSpec((1, N_TAB, SUBLANES, S5_W), lambda l: (l, 0, 0, 0)),
                   pl.BlockSpec((1, 2, nj * SUBLANES, S5_W), lambda l: (l, 0, 0, 0))],
        out_shape=[jax.ShapeDtypeStruct((depth, N_TAB, SUBLANES, S5_W), F32),
                   jax.ShapeDtypeStruct((depth, 2, nj * SUBLANES, S5_W), F32)],
        compiler_params=_cp(("parallel",)),
        name="s5_tables",
    )(flat(a_re), flat(a_im), flat(ldt))


def _s5_kernel(u0_ref, u1_ref, u2_ref, u3_ref, tab_ref, pj_ref, wbr_ref, wbi_ref, wcr_ref, wci_ref, d_ref,
               wglu_ref, bglu_ref, h0r_ref, h0i_ref, yb_ref, hr_out_ref, hi_out_ref,
               up_ref, xr_ref, xi_ref, cr_ref, ci_ref, *, R, nj, chained):
    half_u = MIX_W // 2
    half_s = S5_W // 2
    grp = SUBLANES * nj
    ngrp = R // grp
    ncol = MIX_W // 128
    for c, u_ref in enumerate((u0_ref, u1_ref, u2_ref, u3_ref)):
        for g in range(ngrp):
            for j in range(nj):
                up_ref[c, g * grp + j * SUBLANES:g * grp + (j + 1) * SUBLANES, :] = (
                    u_ref[pl.ds(g * grp + j, SUBLANES, stride=nj), :])
    u = jnp.concatenate([up_ref[c] for c in range(ncol)], 1)
    ub = u.astype(BF16)
    fr = tab_ref[TAB_F][0:1]
    fi = tab_ref[TAB_F + 1][0:1]
    for hf in range(2):
        us = ub[:, hf * half_u:(hf + 1) * half_u]
        ss = slice(hf * half_s, (hf + 1) * half_s)
        bur = _dot(us, wbr_ref[hf])
        bui = _dot(us, wbi_ref[hf])
        xr_ref[:, ss] = fr[:, ss] * bur - fi[:, ss] * bui
        xi_ref[:, ss] = fr[:, ss] * bui + fi[:, ss] * bur

    if chained:
        @pl.when(pl.program_id(1) == 0)
        def _():
            cr_ref[...] = h0r_ref[0]
            ci_ref[...] = h0i_ref[0]

    row8 = lax.broadcasted_iota(jnp.int32, (SUBLANES, S5_LANES), 0)
    for lc in range(S5_W // S5_LANES):
        ls = slice(lc * S5_LANES, (lc + 1) * S5_LANES)
        ar = tab_ref[TAB_A, :, ls]
        ai = tab_ref[TAB_A + 1, :, ls]
        for g in range(ngrp):
            rows = lambda j: slice(g * grp + j * SUBLANES, g * grp + (j + 1) * SUBLANES)
            if chained:
                hr = jnp.zeros((SUBLANES, S5_LANES), F32)
                hi = jnp.zeros((SUBLANES, S5_LANES), F32)
            else:
                hr = h0r_ref[g * SUBLANES:(g + 1) * SUBLANES, ls]
                hi = h0i_ref[g * SUBLANES:(g + 1) * SUBLANES, ls]
            for j in range(nj):
                pr, pi = _cmul(ar, ai, hr, hi)
                hr = pr + xr_ref[rows(j), ls]
                hi = pi + xi_ref[rows(j), ls]
                xr_ref[rows(j), ls] = hr
                xi_ref[rows(j), ls] = hi
            if not chained:
                hr_out_ref[g * SUBLANES:(g + 1) * SUBLANES, ls] = hr
                hi_out_ref[g * SUBLANES:(g + 1) * SUBLANES, ls] = hi
                continue
            yr = jnp.where(row8 == 0, cr_ref[:, ls], pltpu.roll(hr, 1, 0))
            yi = jnp.where(row8 == 0, ci_ref[:, ls], pltpu.roll(hi, 1, 0))
            for s, idx in ((1, TAB_T1), (2, TAB_T2), (4, TAB_T4)):
                pr, pi = _cmul(tab_ref[idx, :, ls], tab_ref[idx + 1, :, ls],
                               pltpu.roll(yr, s, 0), pltpu.roll(yi, s, 0))
                yr, yi = yr + pr, yi + pi
            for j in range(nj):
                pr, pi = _cmul(pj_ref[0, rows(j), ls], pj_ref[1, rows(j), ls], yr, yi)
                hr = xr_ref[rows(j), ls] + pr
                hi = xi_ref[rows(j), ls] + pi
                xr_ref[rows(j), ls] = hr
                xi_ref[rows(j), ls] = hi
            cr_ref[:, ls] = hr[SUBLANES - 1:SUBLANES]
            ci_ref[:, ls] = hi[SUBLANES - 1:SUBLANES]

    if chained:
        @pl.when(pl.program_id(1) == pl.num_programs(1) - 1)
        def _():
            hr_out_ref[0] = cr_ref[...]
            hi_out_ref[0] = ci_ref[...]

    ys = []
    for hf in range(2):
        ss = slice(hf * half_s, (hf + 1) * half_s)
        y = _dot(xr_ref[:, ss].astype(BF16), wcr_ref[hf]) - _dot(xi_ref[:, ss].astype(BF16), wci_ref[hf])
        us = slice(hf * half_u, (hf + 1) * half_u)
        ys.append(_gelu_tanh(y + d_ref[:, us] * u[:, us]))
    y = jnp.concatenate(ys, 1)
    y = y * _sigmoid(_dot(y.astype(BF16), wglu_ref[...]) + bglu_ref[...])
    for c in range(ncol):
        up_ref[c] = y[:, c * 128:(c + 1) * 128]
        for g in range(ngrp):
            for r in range(SUBLANES):
                yb_ref[g * grp + r * nj:g * grp + (r + 1) * nj, c * 128:(c + 1) * 128] = (
                    up_ref[c, pl.ds(g * grp + r, nj, stride=SUBLANES), :].astype(yb_ref.dtype))


def s5(z, tab, pj, wbr, wbi, wcr, wci, d, wglu, bglu, l, h0r, h0i, ls, nb, t, chained, R):
    depth_s = h0r.shape[0]
    if chained:
        nj = S5_NJ_CHAIN
        assert R == SUBLANES * nj
        nc = t // R
        grid = (nb, nc)
        rb = lambda b, c: b * nc + c
        h0 = (h0r.reshape(depth_s, nb, 1, S5_W), h0i.reshape(depth_s, nb, 1, S5_W))
        st_in = pl.BlockSpec((None, 1, 1, S5_W), lambda b, c: (ls, b, 0, 0))
        st_out = pl.BlockSpec((1, 1, S5_W), lambda b, c: (b, 0, 0))
        st_shape = jax.ShapeDtypeStruct((nb, 1, S5_W), F32)
        out_dtype = BF16
    else:
        assert t == SUBLANES
        nj = SUBLANES
        nseq = R // SUBLANES
        grid = (nb // nseq, 1)
        rb = lambda b, c: b
        h0 = (h0r, h0i)
        st_in = pl.BlockSpec((None, nseq, S5_W), lambda b, c: (ls, b, 0))
        st_out = pl.BlockSpec((nseq, S5_W), lambda b, c: (b, 0))
        st_shape = jax.ShapeDtypeStruct((nb, S5_W), F32)
        out_dtype = F32
    lay = lambda shape: pl.BlockSpec((None,) + shape, lambda b, c: (l,) + (0,) * len(shape))
    yb, hr, hi = pl.pallas_call(
        functools.partial(_s5_kernel, R=R, nj=nj, chained=chained),
        grid=grid,
        in_specs=[pl.BlockSpec((R, 128), lambda b, c, k=k: (rb(b, c), Z_SU // 128 + k))
                  for k in range(MIX_W // 128)] + [
            lay((N_TAB, SUBLANES, S5_W)),
            lay((2, pj.shape[2], S5_W)),
            lay((2, MIX_W // 2, S5_W // 2)), lay((2, MIX_W // 2, S5_W // 2)),
            lay((2, S5_W // 2, MIX_W // 2)), lay((2, S5_W // 2, MIX_W // 2)),
            lay((1, MIX_W)), lay((MIX_W, MIX_W)), lay((1, MIX_W)),
            st_in, st_in,
        ],
        out_specs=[pl.BlockSpec((R, MIX_W), lambda b, c: (rb(b, c), 0)), st_out, st_out],
        out_shape=[jax.ShapeDtypeStruct((nb * t, MIX_W), out_dtype), st_shape, st_shape],
        scratch_shapes=[pltpu.VMEM((MIX_W // 128, R, 128), F32), pltpu.VMEM((R, S5_W), F32),
                        pltpu.VMEM((R, S5_W), F32), pltpu.VMEM((1, S5_W), F32), pltpu.VMEM((1, S5_W), F32)],
        compiler_params=_cp(("parallel", "arbitrary")),
        name="s5",
    )(z, z, z, z, tab, pj, wbr, wbi, wcr, wci, d, wglu, bglu, *h0)
    return yb, hr.reshape(nb, S5_GROUPS, S5_STATE), hi.reshape(nb, S5_GROUPS, S5_STATE)


HALO = 16


def _pool_windows(ext_ref, base, n, pos, wp_ref, sc_ref, out_ref, out_rows):
    for g, w in enumerate(POOL_WINDOWS):
        cs = slice(g * POOL_GROUP, (g + 1) * POOL_GROUP)
        u = ext_ref[pl.ds(base, n), cs]
        s = u
        for j in range(1, w):
            s = s + ext_ref[pl.ds(base - j, n), cs]
        cnt = jnp.minimum(pos + 1, w).astype(F32)
        dlt = s / cnt - u
        y = _dot(dlt.astype(BF16), wp_ref[g]) * sc_ref[:, cs]
        out_ref[out_rows, cs] = y.astype(out_ref.dtype)


def _pool_chain_kernel(u_ref, wp_ref, sc_ref, pre_ref, yc_ref, buf_ref, ext_ref, *, L, pos0):
    c = pl.program_id(1)

    @pl.when(c == 0)
    def _():
        ext_ref[0:HALO, :] = pre_ref[0]

    ext_ref[HALO:HALO + L, :] = u_ref[...]
    pos = pos0 + c * L + lax.broadcasted_iota(jnp.int32, (L, 1), 0)
    _pool_windows(ext_ref, HALO, L, pos, wp_ref, sc_ref, yc_ref, slice(None))
    tail = ext_ref[L:L + HALO, :]
    ext_ref[0:HALO, :] = tail

    @pl.when(c == pl.num_programs(1) - 1)
    def _():
        buf_ref[0] = tail


def _pool_seq_kernel(u_ref, wp_ref, sc_ref, pre_ref, yc_ref, buf_ref, ext_ref, *, nseq, pos0):
    t = SUBLANES
    pos = pos0 + lax.broadcasted_iota(jnp.int32, (t, 1), 0)
    for j in range(nseq):
        ext_ref[0:HALO, :] = pre_ref[j]
        ext_ref[HALO:HALO + t, :] = u_ref[j * t:(j + 1) * t, :]
        _pool_windows(ext_ref, HALO, t, pos, wp_ref, sc_ref, yc_ref, slice(j * t, (j + 1) * t))
        buf_ref[j] = ext_ref[t:t + HALO, :]


def pool(z, wp, sc, l, pre, ls, nb, t, pos0, chained, L):
    lay = lambda shape: pl.BlockSpec((None,) + shape, lambda b, c: (l,) + (0,) * len(shape))
    if chained:
        nc = t // L
        grid = (nb, nc)
        kern = functools.partial(_pool_chain_kernel, L=L, pos0=pos0)
        u_spec = pl.BlockSpec((L, MIX_W), lambda b, c: (b * nc + c, Z_PU // MIX_W))
        y_spec = pl.BlockSpec((L, MIX_W), lambda b, c: (b * nc + c, 0))
        nst = 1
        ext_rows = HALO + L
    else:
        assert t == SUBLANES
        nst = L // t
        grid = (nb // nst, 1)
        kern = functools.partial(_pool_seq_kernel, nseq=nst, pos0=pos0)
        u_spec = pl.BlockSpec((L, MIX_W), lambda b, c: (b, Z_PU // MIX_W))
        y_spec = pl.BlockSpec((L, MIX_W), lambda b, c: (b, 0))
        ext_rows = HALO + t
    yc, buf = pl.pallas_call(
        kern,
        grid=grid,
        in_specs=[u_spec, lay((len(POOL_WINDOWS), POOL_GROUP, POOL_GROUP)), lay((1, MIX_W)),
                  pl.BlockSpec((None, nst, HALO, MIX_W), lambda b, c: (ls, b, 0, 0))],
        out_specs=[y_spec, pl.BlockSpec((nst, HALO, MIX_W), lambda b, c: (b, 0, 0))],
        out_shape=[jax.ShapeDtypeStruct((nb * t, MIX_W), BF16),
                   jax.ShapeDtypeStruct((nb, HALO, MIX_W), F32)],
        scratch_shapes=[pltpu.VMEM((ext_rows, MIX_W), F32)],
        compiler_params=_cp(("parallel", "arbitrary")),
        name="pool",
    )(z, wp, sc, pre)
    return yc, buf[:, 1:, :]


def _mla_prep_kernel(zq_ref, zr_ref, cq_ref, sq_ref, qg_ref, kg_ref, wq_ref, wuk_ref,
                     qcat_ref, kcat_ref, c_ref, kr_ref):
    nq = MLA_HEADS * MLA_NOPE
    nr = MLA_HEADS * MLA_ROPE
    mq = zq_ref[:, 0:MLA_Q_RANK]
    mkv = zq_ref[:, MLA_Q_RANK:MLA_Q_RANK + MLA_KV_RANK]
    qd = _dot(_rms(mq, qg_ref[...]).astype(BF16), wq_ref[...])
    cos4 = cq_ref[...]
    sin4 = sq_ref[...]
    q_rope = (qd[:, nq:nq + nr] * cos4 + qd[:, nq + nr:nq + 2 * nr] * sin4) * MLA_SCALE
    zpad = jnp.zeros((qd.shape[0], MLA_QK - MLA_KV_RANK - MLA_ROPE), qcat_ref.dtype)
    for h in range(MLA_HEADS):
        q_lat = _dot(qd[:, h * MLA_NOPE:(h + 1) * MLA_NOPE].astype(BF16), wuk_ref[h]) * MLA_SCALE
        qcat_ref[h, :, 0:MLA_KV_RANK] = q_lat.astype(qcat_ref.dtype)
        qcat_ref[h, :, MLA_KV_RANK:MLA_KV_RANK + MLA_ROPE] = (
            q_rope[:, h * MLA_ROPE:(h + 1) * MLA_ROPE].astype(qcat_ref.dtype))
        qcat_ref[h, :, MLA_KV_RANK + MLA_ROPE:MLA_QK] = zpad
    c_new = _rms(mkv, kg_ref[...])
    kr_new = (zr_ref[:, 0:MLA_ROPE] * cos4[:, 0:MLA_ROPE]
              + zr_ref[:, MLA_ROPE:2 * MLA_ROPE] * sin4[:, 0:MLA_ROPE])
    c_ref[...] = c_new
    kr_ref[...] = kr_new
    kcat_ref[:, 0:MLA_KV_RANK] = c_new.astype(kcat_ref.dtype)
    kcat_ref[:, MLA_KV_RANK:MLA_KV_RANK + MLA_ROPE] = kr_new.astype(kcat_ref.dtype)
    kcat_ref[:, MLA_KV_RANK + MLA_ROPE:MLA_QK] = zpad


def mla_prep(z, cos4, sin4, qg, kg, wq, wuk, l, L, cat_dtype):
    n = z.shape[0]
    nt = cos4.shape[0] // L
    lay = lambda shape: pl.BlockSpec((None,) + shape, lambda i: (l,) + (0,) * len(shape))
    tab_spec = pl.BlockSpec((L, MLA_HEADS * MLA_ROPE), lambda i: (i % nt, 0))
    return pl.pallas_call(
        _mla_prep_kernel,
        grid=(n // L,),
        in_specs=[
            pl.BlockSpec((L, MIX_W), lambda i: (i, Z_MQ // MIX_W)),
            pl.BlockSpec((L, 128), lambda i: (i, Z_MKR // 128)),
            tab_spec, tab_spec,
            lay((1, MLA_Q_RANK)), lay((1, MLA_KV_RANK)),
            lay((MLA_Q_RANK, MLA_HEADS * (MLA_NOPE + 2 * MLA_ROPE))),
            lay((MLA_HEADS, MLA_NOPE, MLA_KV_RANK)),
        ],
        out_specs=[
            pl.BlockSpec((MLA_HEADS, L, MLA_QK), lambda i: (0, i, 0)),
            pl.BlockSpec((L, MLA_QK), lambda i: (i, 0)),
            pl.BlockSpec((L, MLA_KV_RANK), lambda i: (i, 0)),
            pl.BlockSpec((L, MLA_ROPE), lambda i: (i, 0)),
        ],
        out_shape=[
            jax.ShapeDtypeStruct((MLA_HEADS, n, MLA_QK), cat_dtype),
            jax.ShapeDtypeStruct((n, MLA_QK), cat_dtype),
            jax.ShapeDtypeStruct((n, MLA_KV_RANK), F32),
            jax.ShapeDtypeStruct((n, MLA_ROPE), F32),
        ],
        compiler_params=_cp(("parallel",)),
        name="mla_prep",
    )(z, z, cos4, sin4, qg, kg, wq, wuk)


NEG_BIG = -1e30


def _causal_mask(tq, tk):
    row = lax.broadcasted_iota(jnp.int32, (MLA_HEADS * tq, tk), 0)
    key = lax.broadcasted_iota(jnp.int32, (MLA_HEADS * tq, tk), 1)
    head = sum((row >= h * tq).astype(jnp.int32) for h in range(1, MLA_HEADS))
    return key <= row - head * tq


def _mla_causal_kernel(q_ref, k_ref, wuv_ref, yd_ref, m_ref, l_ref, acc_ref, *, tb):
    qi = pl.program_id(1)
    rows = MLA_HEADS * tb
    m_ref[...] = jnp.full(m_ref.shape, NEG_BIG, F32)
    l_ref[...] = jnp.zeros(l_ref.shape, F32)
    acc_ref[...] = jnp.zeros(acc_ref.shape, F32)
    q = q_ref[...].reshape(rows, MLA_QK)

    def block(kj, masked):
        k = k_ref[pl.ds(pl.multiple_of(kj * tb, tb), tb), :]
        s = _dot_t(q, k)
        if masked:
            s = jnp.where(_causal_mask(tb, tb), s, NEG_BIG)
        m_old = m_ref[...]
        m_new = jnp.maximum(m_old, jnp.max(s, -1, keepdims=True))
        alpha = jnp.exp(m_old - m_new)
        p = jnp.exp(s - jnp.tile(m_new, (1, tb // 128)))
        l_ref[...] = alpha * l_ref[...] + jnp.sum(p, -1, keepdims=True)
        p16 = p.astype(BF16)
        half = tb // 2
        pv = (_dot(p16[:, :half], k[:half, 0:MLA_KV_RANK]) + _dot(p16[:, half:], k[half:, 0:MLA_KV_RANK]))
        acc_ref[...] = alpha * acc_ref[...] + pv
        m_ref[...] = m_new

    def body(kj, carry):
        block(kj, False)
        return carry

    lax.fori_loop(0, qi, body, 0)
    block(qi, True)
    o = acc_ref[...] / l_ref[...]
    for h in range(MLA_HEADS):
        oh = o[h * tb:(h + 1) * tb, :].astype(BF16)
        yd_ref[:, h * MLA_V:(h + 1) * MLA_V] = _dot(oh, wuv_ref[h]).astype(yd_ref.dtype)


def mla_causal(qcat, kcat, wuv, l, nb, t, tb):
    nq = t // tb
    rows = MLA_HEADS * tb
    return pl.pallas_call(
        functools.partial(_mla_causal_kernel, tb=tb),
        grid=(nb, nq),
        in_specs=[
            pl.BlockSpec((MLA_HEADS, tb, MLA_QK), lambda b, i: (0, b * nq + i, 0)),
            pl.BlockSpec((t, MLA_QK), lambda b, i: (b, 0)),
            pl.BlockSpec((None, MLA_HEADS, MLA_KV_RANK, MLA_V), lambda b, i: (l, 0, 0, 0)),
        ],
        out_specs=pl.BlockSpec((tb, MIX_W), lambda b, i: (b * nq + i, 0)),
        out_shape=jax.ShapeDtypeStruct((nb * t, MIX_W), BF16),
        scratch_shapes=[pltpu.VMEM((rows, 128), F32), pltpu.VMEM((rows, 128), F32),
                        pltpu.VMEM((rows, MLA_KV_RANK), F32)],
        compiler_params=_cp(("parallel", "arbitrary")),
        name="mla_causal",
    )(qcat, kcat, wuv)


PAGED_GROUP = 4


def _mla_paged_kernel(pt_ref, q_ref, kn_ref, cache_c_ref, cache_krt_ref, wuv_ref, yd_ref,
                      cbuf_ref, krbuf_ref, sem_ref, cb16_ref, s_ref, *, layer, n_pages, kchunk):
    g = pl.program_id(0)
    ng = pl.num_programs(0)
    t = SUBLANES
    hr = MLA_HEADS * t
    past = n_pages * PAGE_SIZE
    nchunks = past // kchunk

    def page_copies(base, slot, p, page_of):
        out = []
        for ab in range(2):
            page = page_of(base + ab, p)
            out.append(pltpu.make_async_copy(
                cache_c_ref.at[layer, page],
                cbuf_ref.at[slot, pl.ds(p * PAGE_SIZE, PAGE_SIZE), pl.ds(ab * MLA_KV_RANK, MLA_KV_RANK)],
                sem_ref.at[0, slot]))
            out.append(pltpu.make_async_copy(
                cache_krt_ref.at[layer, page],
                krbuf_ref.at[slot, pl.ds(ab * MLA_ROPE, MLA_ROPE), pl.ds(p * PAGE_SIZE, PAGE_SIZE)],
                sem_ref.at[1, slot]))
        return out

    table_page = lambda seq, p: pt_ref[seq * n_pages + p]

    def issue(base, slot):
        for p in range(n_pages):
            for cp in page_copies(base, slot, p, table_page):
                cp.start()

    def wait_all(slot):
        pltpu.make_async_copy(cbuf_ref.at[slot], cbuf_ref.at[slot], sem_ref.at[0, slot]).wait()
        pltpu.make_async_copy(krbuf_ref.at[slot], krbuf_ref.at[slot], sem_ref.at[1, slot]).wait()

    @pl.when(g == 0)
    def _():
        issue(0, 0)

    qf = q_ref[...]
    knf = kn_ref[...]
    mask_new = _causal_mask(t, t)
    next_base = jnp.minimum(PAGED_GROUP * (g + 1), PAGED_GROUP * (ng - 1))
    o_all = []

    for half in range(2):
        slot = half
        wait_all(slot)
        fill_base, fill_slot = (PAGED_GROUP * g + 2, 1) if half == 0 else (next_base, 0)
        issue(fill_base, fill_slot)
        qs = [qf[:, (2 * half + ab) * t:(2 * half + ab + 1) * t, :].reshape(hr, MLA_QK) for ab in range(2)]
        kns = [knf[(2 * half + ab) * t:(2 * half + ab + 1) * t, :].astype(BF16) for ab in range(2)]
        z_lat = jnp.zeros((hr, MLA_KV_RANK), F32)
        z_rope = jnp.zeros((hr, MLA_ROPE), F32)
        rope = slice(MLA_KV_RANK, MLA_KV_RANK + MLA_ROPE)
        q_lat = jnp.concatenate([jnp.concatenate([qs[0][:, 0:MLA_KV_RANK], z_lat], 1),
                                 jnp.concatenate([z_lat, qs[1][:, 0:MLA_KV_RANK]], 1)], 0).astype(BF16)
        q_rope = jnp.concatenate([jnp.concatenate([qs[0][:, rope], z_rope], 1),
                                  jnp.concatenate([z_rope, qs[1][:, rope]], 1)], 0).astype(BF16)
        for chunk in range(nchunks):
            ks = slice(chunk * kchunk, (chunk + 1) * kchunk)
            cb = cbuf_ref[slot, ks, :].astype(BF16)
            cb16_ref[ks, :] = cb
            krb = krbuf_ref[slot, :, ks].astype(BF16)
            s_ref[:, ks] = _dot_t(q_lat, cb) + _dot(q_rope, krb)
        s_new = jnp.concatenate(
            [jnp.where(mask_new, _dot_t(qs[ab].astype(BF16), kns[ab]), NEG_BIG) for ab in range(2)], 0)
        s_past = s_ref[...]
        m = jnp.maximum(jnp.max(s_past, -1, keepdims=True), jnp.max(s_new, -1, keepdims=True))
        p_past = jnp.exp(s_past - m)
        p_new = jnp.exp(s_new - m)
        l = jnp.sum(p_past, -1, keepdims=True) + jnp.sum(p_new, -1, keepdims=True)
        p16 = p_past.astype(BF16)
        o_pair = None
        for chunk in range(nchunks):
            ks = slice(chunk * kchunk, (chunk + 1) * kchunk)
            part = _dot(p16[:, ks], cb16_ref[ks, :])
            o_pair = part if o_pair is None else o_pair + part
        for ab in range(2):
            rs = slice(ab * hr, (ab + 1) * hr)
            o_ab = (o_pair[rs, ab * MLA_KV_RANK:(ab + 1) * MLA_KV_RANK]
                    + _dot(p_new[rs].astype(BF16), kns[ab][:, 0:MLA_KV_RANK]))
            o_all.append(o_ab / l[rs])

    for h in range(MLA_HEADS):
        oh = jnp.concatenate([o[h * t:(h + 1) * t] for o in o_all], 0).astype(BF16)
        yd_ref[:, h * MLA_V:(h + 1) * MLA_V] = _dot(oh, wuv_ref[h]).astype(yd_ref.dtype)

    @pl.when(g == ng - 1)
    def _():
        wait_all(0)


def mla_paged(qcat, kcat, cache_c, cache_krt, page_table, wuv, layer):
    nb, n_pages = page_table.shape
    assert nb % PAGED_GROUP == 0
    t = SUBLANES
    past = n_pages * PAGE_SIZE
    kchunk = _tile(past, 2048)
    rows = PAGED_GROUP * t
    grid_spec = pltpu.PrefetchScalarGridSpec(
        num_scalar_prefetch=1,
        grid=(nb // PAGED_GROUP,),
        in_specs=[
            pl.BlockSpec((MLA_HEADS, rows, MLA_QK), lambda g, pt: (0, g, 0)),
            pl.BlockSpec((rows, MLA_QK), lambda g, pt: (g, 0)),
            pl.BlockSpec(memory_space=pl.ANY),
            pl.BlockSpec(memory_space=pl.ANY),
            pl.BlockSpec((None, MLA_HEADS, MLA_KV_RANK, MLA_V), lambda g, pt: (layer, 0, 0, 0)),
        ],
        out_specs=pl.BlockSpec((rows, MIX_W), lambda g, pt: (g, 0)),
        scratch_shapes=[
            pltpu.VMEM((2, past, 2 * MLA_KV_RANK), F32),
            pltpu.VMEM((2, 2 * MLA_ROPE, past), F32),
            pltpu.SemaphoreType.DMA((2, 2)),
            pltpu.VMEM((past, 2 * MLA_KV_RANK), BF16),
            pltpu.VMEM((2 * MLA_HEADS * t, past), F32),
        ],
    )
    return pl.pallas_call(
        functools.partial(_mla_paged_kernel, layer=layer, n_pages=n_pages, kchunk=kchunk),
        grid_spec=grid_spec,
        out_shape=jax.ShapeDtypeStruct((nb * t, MIX_W), F32),
        compiler_params=_cp(("arbitrary",)),
        name="mla_paged",
    )(page_table.reshape(-1), qcat, kcat, cache_c, cache_krt, wuv)


def _merge_kernel(x_ref, *refs):
    brs, wgs, bgs = refs[0:N_BRANCH], refs[N_BRANCH:2 * N_BRANCH], refs[2 * N_BRANCH:3 * N_BRANCH]
    wb_ref, o_ref = refs[3 * N_BRANCH:]
    x = x_ref[...]
    acc = None
    for g in range(N_BRANCH):
        gate = _sigmoid(_dot_t(x, wgs[g][...]) + bgs[g][...])
        term = _dot(brs[g][...].astype(BF16), wb_ref[g]) * gate
        acc = term if acc is None else acc + term
    o_ref[...] = acc.astype(o_ref.dtype)


def merge(xb, ys, wg, bg, wb, l, tm, tn):
    n, d = xb.shape
    nj = d // tn
    row = lambda shape: pl.BlockSpec(shape, lambda j, i: (i, 0))
    gate_w = [pl.BlockSpec((None, tn, d), lambda j, i, g=g: (l, g * nj + j, 0)) for g in range(N_BRANCH)]
    gate_b = [pl.BlockSpec((None, 1, tn), lambda j, i, g=g: (l, 0, g * nj + j)) for g in range(N_BRANCH)]
    return pl.pallas_call(
        _merge_kernel,
        grid=(nj, n // tm),
        in_specs=[row((tm, d))] + [row((tm, MIX_W))] * N_BRANCH + gate_w + gate_b
        + [pl.BlockSpec((None, N_BRANCH, MIX_W, tn), lambda j, i: (l, 0, 0, j))],
        out_specs=pl.BlockSpec((tm, tn), lambda j, i: (i, j)),
        out_shape=jax.ShapeDtypeStruct((n, d), BF16),
        compiler_params=_cp(("parallel", "parallel")),
        name="merge",
    )(xb, *ys, *([wg] * N_BRANCH), *([bg] * N_BRANCH), wb)


def _out_ln_kernel(m_ref, w_ref, x_ref, g_ref, b_ref, o_ref, *, alpha):
    y = alpha * x_ref[...] + _dot(m_ref[...], w_ref[...])
    o_ref[...] = _layer_norm(y, g_ref[...], b_ref[...])


def out_ln(merged, w_out, x, g, b, l, alpha, tm):
    n, d = x.shape
    row = pl.BlockSpec((tm, d), lambda i: (i, 0))
    lay = lambda shape: pl.BlockSpec((None,) + shape, lambda i: (l, 0, 0))
    return pl.pallas_call(
        functools.partial(_out_ln_kernel, alpha=alpha),
        grid=(n // tm,),
        in_specs=[row, lay((d, d)), row, lay((1, d)), lay((1, d))],
        out_specs=row,
        out_shape=jax.ShapeDtypeStruct((n, d), F32),
        compiler_params=_cp(("parallel",)),
        name="out_ln",
    )(merged, w_out, x, g, b)


FFN_HALO = 16


def _ffn_kernel(x_ref, halo_ref, wh_ref, wg_ref, wd_ref, cw_ref, cb_ref, st_ref, g_ref, b_ref,
                o_ref, ob_ref, hl_ref, xcat_ref, hs_ref, acc_ref, *, tm, alpha, chained, tiles_per_seq):
    i = pl.program_id(0)
    f = pl.program_id(1)

    @pl.when(f == 0)
    def _():
        acc_ref[...] = jnp.zeros(acc_ref.shape, F32)
        xcat_ref[FFN_HALO:, :] = x_ref[...].astype(BF16)
        if chained:
            seq_start = (i % tiles_per_seq) == 0
            xcat_ref[0:FFN_HALO, :] = jnp.where(seq_start, 0.0, halo_ref[...]).astype(BF16)

    w0 = cw_ref[0:1, :]
    w1 = cw_ref[1:2, :]
    w2 = cw_ref[2:3, :]
    if chained:
        hs_ref[...] = _dot(xcat_ref[...], wh_ref[...])
        h = hs_ref[FFN_HALO:, :]
        hm1 = hs_ref[pl.ds(FFN_HALO - 1, tm), :]
        hm2 = hs_ref[pl.ds(FFN_HALO - 2, tm), :]
        hl_ref[0] = hs_ref[tm + FFN_HALO - SUBLANES:, :]
        conv = cb_ref[...] + w0 * hm2 + w1 * hm1 + w2 * h
    else:
        t = SUBLANES
        nseq = tm // t
        tf = wh_ref.shape[1]
        h = _dot(xcat_ref[FFN_HALO:, :], wh_ref[...]).reshape(nseq, t, tf)
        r = lax.broadcasted_iota(jnp.int32, (nseq, t, tf), 1)
        b0 = st_ref[:, 0:1, :]
        b1 = st_ref[:, 1:2, :]
        hm1 = jnp.where(r >= 1, pltpu.roll(h, 1, 1), b1)
        hm2 = jnp.where(r >= 2, pltpu.roll(h, 2, 1), jnp.where(r == 1, b1, b0))
        hl_ref[...] = h[:, t - (CONV_W - 1):, :]
        conv = (cb_ref[...] + w0 * hm2 + w1 * hm1 + w2 * h).reshape(tm, tf)
    gate = _dot(xcat_ref[FFN_HALO:, :], wg_ref[...])
    a = (_gelu_tanh(conv) * gate).astype(BF16)
    acc_ref[...] += _dot(a, wd_ref[...])

    @pl.when(f == pl.num_programs(1) - 1)
    def _():
        y = _layer_norm(alpha * x_ref[...] + acc_ref[...], g_ref[...], b_ref[...])
        o_ref[...] = y
        ob_ref[...] = y.astype(ob_ref.dtype)


def conv_ffn(x, w_up, w_down, cw, cb, g, b, l, state, ls, alpha, nb, t, chained, tm, tf):
    n, d = x.shape
    ff = w_down.shape[1]
    nf = ff // tf
    lay = lambda shape: pl.BlockSpec((None,) + shape, lambda i, f: (l, 0, 0))
    hb = tm // FFN_HALO
    if chained:
        tiles_per_seq = t // tm
        st = jnp.zeros((1, 1, CONV_W - 1, tf), F32)
        st_spec = pl.BlockSpec((None, 1, CONV_W - 1, tf), lambda i, f: (0, 0, 0, 0))
        hl_shape = jax.ShapeDtypeStruct((n // tm, SUBLANES, ff), F32)
        hl_spec = pl.BlockSpec((1, SUBLANES, tf), lambda i, f: (i, 0, f))
    else:
        assert t == SUBLANES
        tiles_per_seq = 1
        st = state
        st_spec = pl.BlockSpec((None, tm // t, CONV_W - 1, tf), lambda i, f: (ls, i, 0, f))
        hl_shape = jax.ShapeDtypeStruct((nb, CONV_W - 1, ff), F32)
        hl_spec = pl.BlockSpec((tm // t, CONV_W - 1, tf), lambda i, f: (i, 0, f))
    o, ob, hl = pl.pallas_call(
        functools.partial(_ffn_kernel, tm=tm, alpha=alpha, chained=chained, tiles_per_seq=tiles_per_seq),
        grid=(n // tm, nf),
        in_specs=[
            pl.BlockSpec((tm, d), lambda i, f: (i, 0)),
            pl.BlockSpec((FFN_HALO, d), lambda i, f: (jnp.maximum(i * hb - 1, 0), 0)),
            pl.BlockSpec((None, d, tf), lambda i, f: (l, 0, f)),
            pl.BlockSpec((None, d, tf), lambda i, f: (l, 0, nf + f)),
            pl.BlockSpec((None, tf, d), lambda i, f: (l, f, 0)),
            pl.BlockSpec((None, CONV_W, tf), lambda i, f: (l, 0, f)),
            pl.BlockSpec((None, 1, tf), lambda i, f: (l, 0, f)),
            st_spec, lay((1, d)), lay((1, d)),
        ],
        out_specs=[pl.BlockSpec((tm, d), lambda i, f: (i, 0)), pl.BlockSpec((tm, d), lambda i, f: (i, 0)), hl_spec],
        out_shape=[jax.ShapeDtypeStruct((n, d), F32), jax.ShapeDtypeStruct((n, d), BF16), hl_shape],
        scratch_shapes=[pltpu.VMEM((tm + FFN_HALO, d), BF16), pltpu.VMEM((tm + FFN_HALO, tf), F32),
                        pltpu.VMEM((tm, d), F32)],
        compiler_params=_cp(("parallel", "arbitrary")),
        name="conv_ffn",
    )(x, x, w_up, w_up, w_down, cw, cb, st, g, b)
    if chained:
        hl = hl[tiles_per_seq - 1::tiles_per_seq, SUBLANES - (CONV_W - 1):, :]
    return o, ob, hl


def _rope_tables(pos0, t):
    half = MLA_ROPE // 2
    inv = ROPE_THETA ** (-jnp.arange(half, dtype=F32) / half)
    ang = (pos0 + jnp.arange(t)).astype(F32)[:, None] * inv
    cos, sin = jnp.cos(ang), jnp.sin(ang)
    cosf = jnp.concatenate([cos, cos], -1)
    sinf = jnp.concatenate([-sin, sin], -1)
    return jnp.tile(cosf, (1, MLA_HEADS)), jnp.tile(sinf, (1, MLA_HEADS))


def _rot_half_cols(w):
    half = w.shape[-1] // 2
    return jnp.concatenate([w[..., half:], w[..., :half]], -1)


IN_SIZES = (GLA_HEADS * GLA_DK, GLA_HEADS * GLA_DK, GLA_HEADS * GLA_DV, MIX_W, GLA_GATE_RANK,
            MIX_W, MIX_W, MLA_Q_RANK, MLA_KV_RANK, MLA_ROPE)
IN_GATE_OFF = sum(IN_SIZES)
IN_DEST = (Z_Q, Z_K, Z_V, Z_OG, Z_GA, Z_SU, Z_PU, Z_MQ, Z_MKV, Z_MKR)


def _prep_weights(p):
    depth, d_model, _ = p['w_in'].shape
    hk = GLA_HEADS * GLA_DK
    w_t = jnp.swapaxes(p['w_in'], 1, 2)
    groups, off = {}, 0
    for size, dst in zip(IN_SIZES, IN_DEST):
        groups[dst] = w_t[:, off:off + size, :]
        off += size
    half = MLA_ROPE // 2
    mkr = groups[Z_MKR]
    groups[Z_MKR_ROT] = jnp.concatenate([mkr[:, half:], mkr[:, :half]], 1)
    pad0 = Z_GA + GLA_GATE_RANK
    groups[pad0] = jnp.zeros((depth, Z_COLS - pad0, d_model), F32)
    w_z = jnp.concatenate([groups[k] for k in sorted(groups)], 1).astype(BF16)
    w_gate = w_t[:, IN_GATE_OFF:, :].astype(BF16)
    out = dict(w_z=w_z, w_gate=w_gate, b_gate=p['b_gates'].reshape(depth, 1, N_BRANCH * d_model))
    out['gla_wg'] = jnp.zeros((depth, 128, hk), F32).at[:, :GLA_GATE_RANK].set(p['w_gla_gate'])
    out['gla_bg'] = p['b_gla_gate'].reshape(depth, 1, hk)
    out['gla_ng'] = p['gla_norm_g'].reshape(depth, 1, MIX_W)

    g2 = S5_GROUPS // 2
    eye = jnp.eye(g2, dtype=F32)

    def blockdiag_in(bm):
        bm = bm.reshape(depth, 2, g2, S5_STATE, S5_GROUP)
        w = jnp.einsum('lhgnj,gk->lhgjkn', bm, eye)
        return w.reshape(depth, 2, g2 * S5_GROUP, g2 * S5_STATE).astype(BF16)

    def blockdiag_out(cm):
        cm = cm.reshape(depth, 2, g2, S5_GROUP, S5_STATE)
        w = jnp.einsum('lhgjn,gk->lhgnkj', cm, eye)
        return w.reshape(depth, 2, g2 * S5_STATE, g2 * S5_GROUP).astype(BF16)

    out['s5_wbr'], out['s5_wbi'] = blockdiag_in(p['s5_b_re']), blockdiag_in(p['s5_b_im'])
    out['s5_wcr'], out['s5_wci'] = blockdiag_out(p['s5_c_re']), blockdiag_out(p['s5_c_im'])
    out['s5_d'] = p['s5_d'].reshape(depth, 1, MIX_W)
    out['s5_wglu'] = p['w_s5_glu'].astype(BF16)
    out['s5_bglu'] = p['b_s5_glu'].reshape(depth, 1, MIX_W)
    out['pool_w'] = p['w_pool'].astype(BF16)
    out['pool_sc'] = p['pool_scale'].reshape(depth, 1, MIX_W)
    wq = p['w_q_up'].reshape(depth, MLA_Q_RANK, MLA_HEADS, MLA_NOPE + MLA_ROPE)
    flat = lambda w: w.reshape(depth, MLA_Q_RANK, -1)
    wq_rope = wq[..., MLA_NOPE:]
    out['mla_wq'] = jnp.concatenate(
        [flat(wq[..., :MLA_NOPE]), flat(wq_rope), flat(_rot_half_cols(wq_rope))], 2).astype(BF16)
    wkv = p['w_kv_up'].reshape(depth, MLA_KV_RANK, MLA_HEADS, MLA_NOPE + MLA_V)
    out['mla_wuk'] = wkv[..., :MLA_NOPE].transpose(0, 2, 3, 1).astype(BF16)
    out['mla_wuv'] = wkv[..., MLA_NOPE:].transpose(0, 2, 1, 3).astype(BF16)
    out['mla_qg'] = p['mla_q_norm'].reshape(depth, 1, MLA_Q_RANK)
    out['mla_kg'] = p['mla_kv_norm'].reshape(depth, 1, MLA_KV_RANK)
    out['w_branch'] = p['w_branch'].astype(BF16)
    out['w_out'] = p['w_out'].astype(BF16)
    out['w_up'] = p['w_ffn_up'].astype(BF16)
    out['w_down'] = p['w_ffn_down'].astype(BF16)
    out['conv_w'] = p['ffn_conv_w']
    out['conv_b'] = p['ffn_conv_b'].reshape(depth, 1, -1)
    for nm in ('ln1_g', 'ln1_b', 'ln2_g', 'ln2_b'):
        out[nm] = p[nm].reshape(depth, 1, d_model)
    return out


def _layer(x, xb, w, tabs, st, nb, t, pos0, l, ls, cache, alpha, chained):
    n, d = x.shape
    z = matmul(xb, w['w_z'], l, _tile(n, 1024), Z_COLS // 2)
    gla_w = (w['gla_wg'], w['gla_bg'], w['gla_ng'], l)
    s5_w = (tabs[0], tabs[1], w['s5_wbr'], w['s5_wbi'], w['s5_wcr'], w['s5_wci'], w['s5_d'], w['s5_wglu'],
            w['s5_bglu'], l)
    cos4, sin4 = _rope_tables(pos0, t)
    mla_w = (w['mla_qg'], w['mla_kg'], w['mla_wq'], w['mla_wuk'], l)
    if chained:
        gl = _tile(t, 64)
        ya, s_gla = gla(z, *gla_w, st['gla'], ls, nb, t, gl, _tile(t // gl, 4), True)
        yb, h_re, h_im = s5(z, *s5_w, st['s5_re'], st['s5_im'], ls, nb, t, True, SUBLANES * S5_NJ_CHAIN)
        yc, pbuf = pool(z, w['pool_w'], w['pool_sc'], l, st['pool'], ls, nb, t, pos0, True, _tile(t, 256))
        qcat, kcat, c_new, kr_new = mla_prep(z, cos4, sin4, *mla_w, _tile(t, 256), BF16)
        yd = mla_causal(qcat, kcat, w['mla_wuv'], l, nb, t, _tile(t, 512))
    else:
        ya, s_gla = gla(z, *gla_w, st['gla'], ls, nb, t, t, _tile(nb, 8), False)
        yb, h_re, h_im = s5(z, *s5_w, st['s5_re'], st['s5_im'], ls, nb, t, False, _tile(n, 256))
        yc, pbuf = pool(z, w['pool_w'], w['pool_sc'], l, st['pool'], ls, nb, t, pos0, False, _tile(n, 128))
        prep_rows = _tile(n, 128)
        tile_rows = lambda a: jnp.tile(a, (prep_rows // t, 1))
        qcat, kcat, c_new, kr_new = mla_prep(z, tile_rows(cos4), tile_rows(sin4), *mla_w, prep_rows, F32)
        yd = mla_paged(qcat, kcat, cache[0], cache[1], cache[2], w['mla_wuv'], l)
    merged = merge(xb, (ya, yb, yc, yd), w['w_gate'], w['b_gate'], w['w_branch'], l, _tile(n, 1024), _tile(d, 512))
    x1 = out_ln(merged, w['w_out'], x, w['ln1_g'], w['ln1_b'], l, alpha, _tile(n, 512))
    ff = w['w_down'].shape[1]
    tm = _tile(t, 512) if chained else _tile(n, 512)
    x2, x2b, cbuf = conv_ffn(x1, w['w_up'], w['w_down'], w['conv_w'], w['conv_b'], w['ln2_g'], w['ln2_b'], l,
                             st['conv'], ls, alpha, nb, t, chained, tm, _tile(ff, 512))
    new = (c_new.reshape(nb, t, MLA_KV_RANK), kr_new.reshape(nb, t, MLA_ROPE), s_gla, h_re, h_im, pbuf, cbuf)
    return x2, x2b, new


def _trunk(x3, w, tabs, states, shared_state, pos0, cache, alpha, chained):
    nb, t, d = x3.shape
    depth = w['w_z'].shape[0]
    x = x3.reshape(nb * t, d)
    xb = x.astype(BF16)
    outs = [[] for _ in range(7)]
    for l in range(depth):
        x, xb, new = _layer(x, xb, w, tabs, states, nb, t, pos0, l, 0 if shared_state else l, cache, alpha,
                            chained)
        for lst, val in zip(outs, new):
            lst.append(val)
    return (x.reshape(nb, t, d),) + tuple(jnp.stack(o, 0) for o in outs)


def kernel(x_prompt, x_sample, cache_kv_latent, cache_k_rope, page_table, state_gla, state_s5_re, state_s5_im,
           state_pool, state_ffn_conv, ln1_g, ln1_b, w_in, b_gates, w_gla_gate, b_gla_gate, gla_norm_g,
           s5_a_re, s5_a_im, s5_log_dt, s5_b_re, s5_b_im, s5_c_re, s5_c_im, s5_d, w_s5_glu, b_s5_glu,
           w_pool, pool_scale, mla_q_norm, w_q_up, mla_kv_norm, w_kv_up, w_branch, w_out, ln2_g, ln2_b,
           w_ffn_up, ffn_conv_w, ffn_conv_b, w_ffn_down):
    params = {'ln1_g': ln1_g, 'ln1_b': ln1_b, 'w_in': w_in, 'b_gates': b_gates, 'w_gla_gate': w_gla_gate,
              'b_gla_gate': b_gla_gate, 'gla_norm_g': gla_norm_g, 's5_b_re': s5_b_re, 's5_b_im': s5_b_im,
              's5_c_re': s5_c_re, 's5_c_im': s5_c_im, 's5_d': s5_d, 'w_s5_glu': w_s5_glu, 'b_s5_glu': b_s5_glu,
              'w_pool': w_pool, 'pool_scale': pool_scale, 'mla_q_norm': mla_q_norm, 'w_q_up': w_q_up,
              'mla_kv_norm': mla_kv_norm, 'w_kv_up': w_kv_up, 'w_branch': w_branch, 'w_out': w_out,
              'ln2_g': ln2_g, 'ln2_b': ln2_b, 'w_ffn_up': w_ffn_up, 'ffn_conv_w': ffn_conv_w,
              'ffn_conv_b': ffn_conv_b, 'w_ffn_down': w_ffn_down}
    depth = w_in.shape[0]
    d_ff = ffn_conv_w.shape[-1]
    alpha = (2 * depth) ** 0.25
    w = _prep_weights(params)
    tabs = s5_tables(s5_a_re, s5_a_im, s5_log_dt, S5_NJ_CHAIN)
    pad_pool = lambda a: jnp.pad(a, ((0, 0), (0, 0), (1, 0), (0, 0)))

    nb_p = x_prompt.shape[0]
    zeros = lambda *shape: jnp.zeros((1, nb_p) + shape, F32)
    st_p = dict(gla=zeros(GLA_HEADS, GLA_DK, GLA_DV), s5_re=zeros(S5_W), s5_im=zeros(S5_W),
                pool=zeros(HALO, MIX_W), conv=None)
    res_p = _trunk(x_prompt, w, tabs, st_p, True, 0, None, alpha, True)

    nb_s = x_sample.shape[0]
    past_len = page_table.shape[1] * PAGE_SIZE
    st_s = dict(gla=state_gla, s5_re=state_s5_re.reshape(depth, nb_s, S5_W),
                s5_im=state_s5_im.reshape(depth, nb_s, S5_W), pool=pad_pool(state_pool), conv=state_ffn_conv)
    cache_krt = jnp.swapaxes(cache_k_rope, 2, 3)
    res_s = _trunk(x_sample, w, tabs, st_s, False, past_len, (cache_kv_latent, cache_krt, page_table), alpha, False)
    return (res_p[0], res_s[0]) + res_p[1:] + res_s[1:]
```

```python
import functools
import math

import jax
import jax.numpy as jnp
from jax import lax
from jax.experimental import pallas as pl
from jax.experimental.pallas import tpu as pltpu

F32 = jnp.float32
BF16 = jnp.bfloat16
HIGHEST = lax.Precision.HIGHEST

MIX_W = 512
N_BRANCH = 4
GLA_HEADS, GLA_DK, GLA_DV, GLA_GATE_RANK, GLA_TAU = 4, 64, 128, 16, 16.0
S5_GROUP, S5_GROUPS, S5_STATE = 16, 32, 64
S5_W = S5_GROUPS * S5_STATE
POOL_WINDOWS = (2, 4, 8, 16)
POOL_GROUP = 128
POOL_BUF = 15
MLA_HEADS, MLA_NOPE, MLA_ROPE, MLA_V, MLA_Q_RANK, MLA_KV_RANK = 4, 128, 64, 128, 384, 128
MLA_SCALE = (MLA_NOPE + MLA_ROPE) ** -0.5
MLA_QK = 256
ROPE_THETA = 10000.0
PAGE_SIZE = 128
CONV_W = 3
LN_EPS = 1e-5
RMS_EPS = 1e-6

Z_Q, Z_K, Z_V, Z_OG, Z_SU, Z_PU, Z_MQ, Z_MKV, Z_MKR, Z_MKR_ROT, Z_GA = (
    0, 256, 512, 1024, 1536, 2048, 2560, 2944, 3072, 3136, 3200)
Z_COLS = 3328

V7X_VMEM_BYTES = 64 * 1024 * 1024
VMEM_LIMIT = 48 * 1024 * 1024
SUBLANES = 8


def _cp(sem, vmem=VMEM_LIMIT):
    return pltpu.CompilerParams(dimension_semantics=sem, vmem_limit_bytes=vmem)


def _tile(dim, pref):
    t = min(dim, pref)
    while dim % t:
        t //= 2
    return t


def _sigmoid(x):
    return 1.0 / (1.0 + jnp.exp(-x))


def _gelu_tanh(x):
    c = math.sqrt(2.0 / math.pi)
    return 0.5 * x * (1.0 + jnp.tanh(c * (x + 0.044715 * (x * x * x))))


def _log_sigmoid(x):
    return jnp.minimum(x, 0.0) - jnp.log(1.0 + jnp.exp(-jnp.abs(x)))


def _layer_norm(y, g, b):
    mu = jnp.mean(y, -1, keepdims=True)
    d = y - mu
    var = jnp.mean(d * d, -1, keepdims=True)
    return d * lax.rsqrt(var + LN_EPS) * g + b


def _rms(x, g):
    return x * lax.rsqrt(jnp.mean(x * x, -1, keepdims=True) + RMS_EPS) * g


def _dot(a, b):
    return jnp.dot(a, b, preferred_element_type=F32)


def _dot_t(a, b):
    return lax.dot_general(a, b, (((1,), (1,)), ((), ())), preferred_element_type=F32)


def _dot_ta(a, b, **kw):
    return lax.dot_general(a, b, (((0,), (0,)), ((), ())), preferred_element_type=F32, **kw)


def _cmul(ar, ai, br, bi):
    return ar * br - ai * bi, ar * bi + ai * br


def _mm_kernel(x_ref, w_ref, o_ref):
    o_ref[...] = _dot_t(x_ref[...], w_ref[...]).astype(o_ref.dtype)


def matmul(x, w, l, tm, tn, out_dtype=F32):
    m, k = x.shape
    n = w.shape[1]
    return pl.pallas_call(
        _mm_kernel,
        grid=(m // tm, n // tn),
        in_specs=[pl.BlockSpec((tm, k), lambda i, j: (i, 0)),
                  pl.BlockSpec((None, tn, k), lambda i, j: (l, j, 0))],
        out_specs=pl.BlockSpec((tm, tn), lambda i, j: (i, j)),
        out_shape=jax.ShapeDtypeStruct((m, n), out_dtype),
        compiler_params=_cp(("parallel", "parallel")),
        name="in_proj",
    )(x, w)


GLA_SAFE_LOGDECAY = 60.0


def _gla_kernel(q_ref, k_ref, v_ref, og_ref, ga_ref, wg_ref, bg_ref, ng_ref, s0_ref,
                ya_ref, st_ref, la_ref, o_ref, *, L, nsub, chain):
    R = nsub * L
    assert chain or L == SUBLANES

    @pl.when(pl.program_id(1) == 0)
    def _():
        st_ref[...] = s0_ref[...]

    pre = jnp.dot(ga_ref[...], wg_ref[...], precision=HIGHEST, preferred_element_type=F32) + bg_ref[...]
    log_a = _log_sigmoid(pre) * (1.0 / GLA_TAU)
    hk = GLA_HEADS * GLA_DK
    blk_of = lambda x: sum((x >= j * L).astype(jnp.int32) for j in range(1, nsub))
    row = lax.broadcasted_iota(jnp.int32, (R, R), 0)
    col = lax.broadcasted_iota(jnp.int32, (R, R), 1)
    tri = jnp.where(row >= col, 1.0, 0.0)
    if nsub > 1:
        tri = jnp.where(blk_of(row) == blk_of(col), tri, 0.0)
    causal = tri > 0.5
    a_hi = log_a.astype(BF16)
    rem = log_a - a_hi.astype(F32)
    a_mid = rem.astype(BF16)
    a_lo = (rem - a_mid.astype(F32)).astype(BF16)
    a3 = jnp.concatenate([a_hi, a_mid, a_lo], 1)
    sum3 = lambda m, ax: (lax.slice_in_dim(m, 0, hk, axis=ax) + lax.slice_in_dim(m, hk, 2 * hk, axis=ax)
                          + lax.slice_in_dim(m, 2 * hk, 3 * hk, axis=ax))
    bc = sum3(_dot(tri.astype(BF16), a3), 1)
    if nsub > 1:
        rowb = lax.broadcasted_iota(jnp.int32, (R, nsub * GLA_DV), 0)
        colb = lax.broadcasted_iota(jnp.int32, (R, nsub * GLA_DV), 1)
        blk_col = sum((colb >= j * GLA_DV).astype(jnp.int32) for j in range(1, nsub))
        member = jnp.where(blk_of(rowb) == blk_col, 1.0, 0.0).astype(BF16)
    else:
        member = jnp.ones((R, GLA_DV), BF16)
    a_tot_all = jnp.exp(sum3(_dot_ta(a3, member), 0))
    tot = jnp.concatenate([jnp.broadcast_to(bc[(j + 1) * L - 1:(j + 1) * L, :], (L, hk)) for j in range(nsub)], 0)
    scale = GLA_DK ** -0.5
    safe = jnp.max(jnp.abs(bc)) < GLA_SAFE_LOGDECAY
    la_ref[...] = log_a

    @pl.when(safe)
    def _():
        q = q_ref[...] * scale
        k = k_ref[...]
        q_in = q * jnp.exp(bc)
        k_in = k * jnp.exp(-bc)
        k_out = k * jnp.exp(tot - bc)
        for h in range(GLA_HEADS):
            ks = slice(h * GLA_DK, (h + 1) * GLA_DK)
            vs = slice(h * GLA_DV, (h + 1) * GLA_DV)
            v = v_ref[:, vs]
            scores = jnp.where(causal, _dot_t(q_in[:, ks].astype(BF16), k_in[:, ks].astype(BF16)), 0.0)
            o = _dot(scores.astype(BF16), v.astype(BF16))
            o_inter = []
            state = st_ref[0, h]
            for j in range(nsub):
                rs = slice(j * L, (j + 1) * L)
                if not chain:
                    state = st_ref[j, h]
                o_inter.append(_dot(q_in[rs, ks].astype(BF16), state.astype(BF16)))
                a_tot = a_tot_all[ks, j * GLA_DV:(j + 1) * GLA_DV]
                state = a_tot * state + _dot_ta(k_out[rs, ks].astype(BF16), v[rs].astype(BF16))
                if not chain:
                    st_ref[j, h] = state
            if chain:
                st_ref[0, h] = state
            o_ref[:, vs] = o + (o_inter[0] if nsub == 1 else jnp.concatenate(o_inter, 0))

    @pl.when(jnp.logical_not(safe))
    def _():
        rr = lax.broadcasted_iota(jnp.int32, (SUBLANES, SUBLANES * GLA_DV), 0)
        cc = lax.broadcasted_iota(jnp.int32, (SUBLANES, SUBLANES * GLA_DV), 1)
        spread = jnp.where(sum((cc >= r * GLA_DV).astype(jnp.int32) for r in range(1, SUBLANES)) == rr, 1.0, 0.0)
        columns = lambda x8: _dot_ta(x8, spread, precision=HIGHEST)
        for h in range(GLA_HEADS):
            ks = slice(h * GLA_DK, (h + 1) * GLA_DK)
            vs = slice(h * GLA_DV, (h + 1) * GLA_DV)
            def eight_tokens(i8, state):
                rows = pl.ds(pl.multiple_of(i8 * SUBLANES, SUBLANES), SUBLANES)
                a_cols = jnp.exp(columns(la_ref[rows, ks]))
                k_cols = columns(k_ref[rows, ks])
                q_cols = columns(q_ref[rows, ks] * scale)
                v8 = v_ref[rows, vs]
                outs = []
                for r in range(SUBLANES):
                    lanes = slice(r * GLA_DV, (r + 1) * GLA_DV)
                    state = a_cols[:, lanes] * state + k_cols[:, lanes] * v8[r:r + 1, :]
                    outs.append(jnp.sum(q_cols[:, lanes] * state, 0, keepdims=True))
                o_ref[rows, vs] = jnp.concatenate(outs, 0)
                return state

            if chain:
                st_ref[0, h] = lax.fori_loop(0, R // SUBLANES, eight_tokens, st_ref[0, h])
            else:
                def one_sequence(j, carry):
                    st_ref[j, h] = eight_tokens(j, st_ref[j, h])
                    return carry

                lax.fori_loop(0, nsub, one_sequence, 0)

    for h in range(GLA_HEADS):
        vs = slice(h * GLA_DV, (h + 1) * GLA_DV)
        o = o_ref[:, vs]
        o = o * lax.rsqrt(jnp.mean(o * o, -1, keepdims=True) + RMS_EPS) * ng_ref[:, vs]
        g = og_ref[:, vs]
        ya_ref[:, vs] = (o * (g * _sigmoid(g))).astype(ya_ref.dtype)


def gla(z, wg_pad, bg, ng, l, s0, ls, nb, t, L, nsub, chain):
    R = nsub * L
    if chain:
        nc = t // R
        grid = (nb, nc)
        nst = 1
    else:
        assert L == t
        nc = 1
        grid = (nb // nsub, 1)
        nst = nsub
    rb = lambda b, c: b * nc + c
    hk = GLA_HEADS * GLA_DK
    st_blk = (nst, GLA_HEADS, GLA_DK, GLA_DV)
    return pl.pallas_call(
        functools.partial(_gla_kernel, L=L, nsub=nsub, chain=chain),
        grid=grid,
        in_specs=[
            pl.BlockSpec((R, hk), lambda b, c: (rb(b, c), Z_Q // hk)),
            pl.BlockSpec((R, hk), lambda b, c: (rb(b, c), Z_K // hk)),
            pl.BlockSpec((R, MIX_W), lambda b, c: (rb(b, c), Z_V // MIX_W)),
            pl.BlockSpec((R, MIX_W), lambda b, c: (rb(b, c), Z_OG // MIX_W)),
            pl.BlockSpec((R, 128), lambda b, c: (rb(b, c), Z_GA // 128)),
            pl.BlockSpec((None, 128, hk), lambda b, c: (l, 0, 0)),
            pl.BlockSpec((None, 1, hk), lambda b, c: (l, 0, 0)),
            pl.BlockSpec((None, 1, MIX_W), lambda b, c: (l, 0, 0)),
            pl.BlockSpec((None,) + st_blk, lambda b, c: (ls, b, 0, 0, 0)),
        ],
        out_specs=[pl.BlockSpec((R, MIX_W), lambda b, c: (rb(b, c), 0)),
                   pl.BlockSpec(st_blk, lambda b, c: (b, 0, 0, 0))],
        out_shape=[jax.ShapeDtypeStruct((nb * t, MIX_W), BF16),
                   jax.ShapeDtypeStruct((nb, GLA_HEADS, GLA_DK, GLA_DV), F32)],
        scratch_shapes=[pltpu.VMEM((R, hk), F32), pltpu.VMEM((R, MIX_W), F32)],
        compiler_params=_cp(("parallel", "arbitrary")),
        name="gla",
    )(z, z, z, z, z, wg_pad, bg, ng, s0)


S5_NJ_CHAIN = 32
TAB_A, TAB_T1, TAB_T2, TAB_T4, TAB_F = 0, 2, 4, 6, 8
N_TAB = 10
S5_LANES = 512


def _s5_tab_kernel(are_ref, aim_ref, ldt_ref, tab_ref, pj_ref, *, nj):
    lr = are_ref[0]
    li = aim_ref[0]
    dt = jnp.exp(ldt_ref[0])
    mag = jnp.exp(lr * dt)
    a_re = mag * jnp.cos(li * dt)
    a_im = mag * jnp.sin(li * dt)
    shp = (SUBLANES, S5_W)
    r = lax.broadcasted_iota(jnp.int32, shp, 0)
    tab_ref[0, TAB_A] = jnp.broadcast_to(a_re, shp)
    tab_ref[0, TAB_A + 1] = jnp.broadcast_to(a_im, shp)
    pr, pi = a_re, a_im
    for j in range(nj):
        pj_ref[0, 0, j * SUBLANES:(j + 1) * SUBLANES, :] = jnp.broadcast_to(pr, shp)
        pj_ref[0, 1, j * SUBLANES:(j + 1) * SUBLANES, :] = jnp.broadcast_to(pi, shp)
        if j + 1 < nj:
            pr, pi = _cmul(pr, pi, a_re, a_im)
    br, bi = pr, pi
    for s, idx in ((1, TAB_T1), (2, TAB_T2), (4, TAB_T4)):
        tab_ref[0, idx] = jnp.where(r >= s, br, 0.0)
        tab_ref[0, idx + 1] = jnp.where(r >= s, bi, 0.0)
        br, bi = _cmul(br, bi, br, bi)
    den = lr * lr + li * li
    nr, ni = a_re - 1.0, a_im
    tab_ref[0, TAB_F] = jnp.broadcast_to((nr * lr + ni * li) / den, shp)
    tab_ref[0, TAB_F + 1] = jnp.broadcast_to((ni * lr - nr * li) / den, shp)


def s5_tables(a_re, a_im, log_dt, nj):
    depth = a_re.shape[0]
    flat = lambda a: a.reshape(depth, 1, S5_W)
    ldt = jnp.broadcast_to(log_dt[:, :, None], (depth, S5_GROUPS, S5_STATE))
    spec = pl.BlockSpec((1, 1, S5_W), lambda l: (l, 0, 0))
    return pl.pallas_call(
        functools.partial(_s5_tab_kernel, nj=nj),
        grid=(depth,),
        in_specs=[spec, spec, spec],
        out_specs=[pl.BlockSpec((1, N_TAB, SUBLANES, S5_W), lambda l: (l, 0, 0, 0)),
                   pl.BlockSpec((1, 2, nj * SUBLANES, S5_W), lambda l: (l, 0, 0, 0))],
        out_shape=[jax.ShapeDtypeStruct((depth, N_TAB, SUBLANES, S5_W), F32),
                   jax.ShapeDtypeStruct((depth, 2, nj * SUBLANES, S5_W), F32)],
        compiler_params=_cp(("parallel",)),
        name="s5_tables",
    )(flat(a_re), flat(a_im), flat(ldt))


def _s5_kernel(u0_ref, u1_ref, u2_ref, u3_ref, tab_ref, pj_ref, wbr_ref, wbi_ref, wcr_ref, wci_ref, d_ref,
               wglu_ref, bglu_ref, h0r_ref, h0i_ref, yb_ref, hr_out_ref, hi_out_ref,
               up_ref, xr_ref, xi_ref, cr_ref, ci_ref, *, R, nj, chained):
    half_u = MIX_W // 2
    half_s = S5_W // 2
    grp = SUBLANES * nj
    ngrp = R // grp
    ncol = MIX_W // 128
    for c, u_ref in enumerate((u0_ref, u1_ref, u2_ref, u3_ref)):
        for g in range(ngrp):
            for j in range(nj):
                up_ref[c, g * grp + j * SUBLANES:g * grp + (j + 1) * SUBLANES, :] = (
                    u_ref[pl.ds(g * grp + j, SUBLANES, stride=nj), :])
    u = jnp.concatenate([up_ref[c] for c in range(ncol)], 1)
    ub = u.astype(BF16)
    fr = tab_ref[TAB_F][0:1]
    fi = tab_ref[TAB_F + 1][0:1]
    for hf in range(2):
        us = ub[:, hf * half_u:(hf + 1) * half_u]
        ss = slice(hf * half_s, (hf + 1) * half_s)
        bur = _dot(us, wbr_ref[hf])
        bui = _dot(us, wbi_ref[hf])
        xr_ref[:, ss] = fr[:, ss] * bur - fi[:, ss] * bui
        xi_ref[:, ss] = fr[:, ss] * bui + fi[:, ss] * bur

    if chained:
        @pl.when(pl.program_id(1) == 0)
        def _():
            cr_ref[...] = h0r_ref[0]
            ci_ref[...] = h0i_ref[0]

    row8 = lax.broadcasted_iota(jnp.int32, (SUBLANES, S5_LANES), 0)
    for lc in range(S5_W // S5_LANES):
        ls = slice(lc * S5_LANES, (lc + 1) * S5_LANES)
        ar = tab_ref[TAB_A, :, ls]
        ai = tab_ref[TAB_A + 1, :, ls]
        for g in range(ngrp):
            rows = lambda j: slice(g * grp + j * SUBLANES, g * grp + (j + 1) * SUBLANES)
            if chained:
                hr = jnp.zeros((SUBLANES, S5_LANES), F32)
                hi = jnp.zeros((SUBLANES, S5_LANES), F32)
            else:
                hr = h0r_ref[g * SUBLANES:(g + 1) * SUBLANES, ls]
                hi = h0i_ref[g * SUBLANES:(g + 1) * SUBLANES, ls]
            for j in range(nj):
                pr, pi = _cmul(ar, ai, hr, hi)
                hr = pr + xr_ref[rows(j), ls]
                hi = pi + xi_ref[rows(j), ls]
                xr_ref[rows(j), ls] = hr
                xi_ref[rows(j), ls] = hi
            if not chained:
                hr_out_ref[g * SUBLANES:(g + 1) * SUBLANES, ls] = hr
                hi_out_ref[g * SUBLANES:(g + 1) * SUBLANES, ls] = hi
                continue
            yr = jnp.where(row8 == 0, cr_ref[:, ls], pltpu.roll(hr, 1, 0))
            yi = jnp.where(row8 == 0, ci_ref[:, ls], pltpu.roll(hi, 1, 0))
            for s, idx in ((1, TAB_T1), (2, TAB_T2), (4, TAB_T4)):
                pr, pi = _cmul(tab_ref[idx, :, ls], tab_ref[idx + 1, :, ls],
                               pltpu.roll(yr, s, 0), pltpu.roll(yi, s, 0))
                yr, yi = yr + pr, yi + pi
            for j in range(nj):
                pr, pi = _cmul(pj_ref[0, rows(j), ls], pj_ref[1, rows(j), ls], yr, yi)
                hr = xr_ref[rows(j), ls] + pr
                hi = xi_ref[rows(j), ls] + pi
                xr_ref[rows(j), ls] = hr
                xi_ref[rows(j), ls] = hi
            cr_ref[:, ls] = hr[SUBLANES - 1:SUBLANES]
            ci_ref[:, ls] = hi[SUBLANES - 1:SUBLANES]

    if chained:
        @pl.when(pl.program_id(1) == pl.num_programs(1) - 1)
        def _():
            hr_out_ref[0] = cr_ref[...]
            hi_out_ref[0] = ci_ref[...]

    ys = []
    for hf in range(2):
        ss = slice(hf * half_s, (hf + 1) * half_s)
        y = _dot(xr_ref[:, ss].astype(BF16), wcr_ref[hf]) - _dot(xi_ref[:, ss].astype(BF16), wci_ref[hf])
        us = slice(hf * half_u, (hf + 1) * half_u)
        ys.append(_gelu_tanh(y + d_ref[:, us] * u[:, us]))
    y = jnp.concatenate(ys, 1)
    y = y * _sigmoid(_dot(y.astype(BF16), wglu_ref[...]) + bglu_ref[...])
    for c in range(ncol):
        up_ref[c] = y[:, c * 128:(c + 1) * 128]
        for g in range(ngrp):
            for r in range(SUBLANES):
                yb_ref[g * grp + r * nj:g * grp + (r + 1) * nj, c * 128:(c + 1) * 128] = (
                    up_ref[c, pl.ds(g * grp + r, nj, stride=SUBLANES), :].astype(yb_ref.dtype))


def s5(z, tab, pj, wbr, wbi, wcr, wci, d, wglu, bglu, l, h0r, h0i, ls, nb, t, chained, R):
    depth_s = h0r.shape[0]
    if chained:
        nj = S5_NJ_CHAIN
        assert R == SUBLANES * nj
        nc = t // R
        grid = (nb, nc)
        rb = lambda b, c: b * nc + c
        h0 = (h0r.reshape(depth_s, nb, 1, S5_W), h0i.reshape(depth_s, nb, 1, S5_W))
        st_in = pl.BlockSpec((None, 1, 1, S5_W), lambda b, c: (ls, b, 0, 0))
        st_out = pl.BlockSpec((1, 1, S5_W), lambda b, c: (b, 0, 0))
        st_shape = jax.ShapeDtypeStruct((nb, 1, S5_W), F32)
        out_dtype = BF16
    else:
        assert t == SUBLANES
        nj = SUBLANES
        nseq = R // SUBLANES
        grid = (nb // nseq, 1)
        rb = lambda b, c: b
        h0 = (h0r, h0i)
        st_in = pl.BlockSpec((None, nseq, S5_W), lambda b, c: (ls, b, 0))
        st_out = pl.BlockSpec((nseq, S5_W), lambda b, c: (b, 0))
        st_shape = jax.ShapeDtypeStruct((nb, S5_W), F32)
        out_dtype = F32
    lay = lambda shape: pl.BlockSpec((None,) + shape, lambda b, c: (l,) + (0,) * len(shape))
    yb, hr, hi = pl.pallas_call(
        functools.partial(_s5_kernel, R=R, nj=nj, chained=chained),
        grid=grid,
        in_specs=[pl.BlockSpec((R, 128), lambda b, c, k=k: (rb(b, c), Z_SU // 128 + k))
                  for k in range(MIX_W // 128)] + [
            lay((N_TAB, SUBLANES, S5_W)),
            lay((2, pj.shape[2], S5_W)),
            lay((2, MIX_W // 2, S5_W // 2)), lay((2, MIX_W // 2, S5_W // 2)),
            lay((2, S5_W // 2, MIX_W // 2)), lay((2, S5_W // 2, MIX_W // 2)),
            lay((1, MIX_W)), lay((MIX_W, MIX_W)), lay((1, MIX_W)),
            st_in, st_in,
        ],
        out_specs=[pl.BlockSpec((R, MIX_W), lambda b, c: (rb(b, c), 0)), st_out, st_out],
        out_shape=[jax.ShapeDtypeStruct((nb * t, MIX_W), out_dtype), st_shape, st_shape],
        scratch_shapes=[pltpu.VMEM((MIX_W // 128, R, 128), F32), pltpu.VMEM((R, S5_W), F32),
                        pltpu.VMEM((R, S5_W), F32), pltpu.VMEM((1, S5_W), F32), pltpu.VMEM((1, S5_W), F32)],
        compiler_params=_cp(("parallel", "arbitrary")),
        name="s5",
    )(z, z, z, z, tab, pj, wbr, wbi, wcr, wci, d, wglu, bglu, *h0)
    return yb, hr.reshape(nb, S5_GROUPS, S5_STATE), hi.reshape(nb, S5_GROUPS, S5_STATE)


HALO = 16


def _pool_windows(ext_ref, base, n, pos, wp_ref, sc_ref, out_ref, out_rows):
    for g, w in enumerate(POOL_WINDOWS):
        cs = slice(g * POOL_GROUP, (g + 1) * POOL_GROUP)
        u = ext_ref[pl.ds(base, n), cs]
        s = u
        for j in range(1, w):
            s = s + ext_ref[pl.ds(base - j, n), cs]
        cnt = jnp.minimum(pos + 1, w).astype(F32)
        dlt = s / cnt - u
        y = _dot(dlt.astype(BF16), wp_ref[g]) * sc_ref[:, cs]
        out_ref[out_rows, cs] = y.astype(out_ref.dtype)


def _pool_chain_kernel(u_ref, wp_ref, sc_ref, pre_ref, yc_ref, buf_ref, ext_ref, *, L, pos0):
    c = pl.program_id(1)

    @pl.when(c == 0)
    def _():
        ext_ref[0:HALO, :] = pre_ref[0]

    ext_ref[HALO:HALO + L, :] = u_ref[...]
    pos = pos0 + c * L + lax.broadcasted_iota(jnp.int32, (L, 1), 0)
    _pool_windows(ext_ref, HALO, L, pos, wp_ref, sc_ref, yc_ref, slice(None))
    tail = ext_ref[L:L + HALO, :]
    ext_ref[0:HALO, :] = tail

    @pl.when(c == pl.num_programs(1) - 1)
    def _():
        buf_ref[0] = tail


def _pool_seq_kernel(u_ref, wp_ref, sc_ref, pre_ref, yc_ref, buf_ref, ext_ref, *, nseq, pos0):
    t = SUBLANES
    pos = pos0 + lax.broadcasted_iota(jnp.int32, (t, 1), 0)
    for j in range(nseq):
        ext_ref[0:HALO, :] = pre_ref[j]
        ext_ref[HALO:HALO + t, :] = u_ref[j * t:(j + 1) * t, :]
        _pool_windows(ext_ref, HALO, t, pos, wp_ref, sc_ref, yc_ref, slice(j * t, (j + 1) * t))
        buf_ref[j] = ext_ref[t:t + HALO, :]


def pool(z, wp, sc, l, pre, ls, nb, t, pos0, chained, L):
    lay = lambda shape: pl.BlockSpec((None,) + shape, lambda b, c: (l,) + (0,) * len(shape))
    if chained:
        nc = t // L
        grid = (nb, nc)
        kern = functools.partial(_pool_chain_kernel, L=L, pos0=pos0)
        u_spec = pl.BlockSpec((L, MIX_W), lambda b, c: (b * nc + c, Z_PU // MIX_W))
        y_spec = pl.BlockSpec((L, MIX_W), lambda b, c: (b * nc + c, 0))
        nst = 1
        ext_rows = HALO + L
    else:
        assert t == SUBLANES
        nst = L // t
        grid = (nb // nst, 1)
        kern = functools.partial(_pool_seq_kernel, nseq=nst, pos0=pos0)
        u_spec = pl.BlockSpec((L, MIX_W), lambda b, c: (b, Z_PU // MIX_W))
        y_spec = pl.BlockSpec((L, MIX_W), lambda b, c: (b, 0))
        ext_rows = HALO + t
    yc, buf = pl.pallas_call(
        kern,
        grid=grid,
        in_specs=[u_spec, lay((len(POOL_WINDOWS), POOL_GROUP, POOL_GROUP)), lay((1, MIX_W)),
                  pl.BlockSpec((None, nst, HALO, MIX_W), lambda b, c: (ls, b, 0, 0))],
        out_specs=[y_spec, pl.BlockSpec((nst, HALO, MIX_W), lambda b, c: (b, 0, 0))],
        out_shape=[jax.ShapeDtypeStruct((nb * t, MIX_W), BF16),
                   jax.ShapeDtypeStruct((nb, HALO, MIX_W), F32)],
        scratch_shapes=[pltpu.VMEM((ext_rows, MIX_W), F32)],
        compiler_params=_cp(("parallel", "arbitrary")),
        name="pool",
    )(z, wp, sc, pre)
    return yc, buf[:, 1:, :]


def _mla_prep_kernel(zq_ref, zr_ref, cq_ref, sq_ref, qg_ref, kg_ref, wq_ref, wuk_ref,
                     qcat_ref, kcat_ref, c_ref, kr_ref):
    nq = MLA_HEADS * MLA_NOPE
    nr = MLA_HEADS * MLA_ROPE
    mq = zq_ref[:, 0:MLA_Q_RANK]
    mkv = zq_ref[:, MLA_Q_RANK:MLA_Q_RANK + MLA_KV_RANK]
    qd = _dot(_rms(mq, qg_ref[...]).astype(BF16), wq_ref[...])
    cos4 = cq_ref[...]
    sin4 = sq_ref[...]
    q_rope = (qd[:, nq:nq + nr] * cos4 + qd[:, nq + nr:nq + 2 * nr] * sin4) * MLA_SCALE
    zpad = jnp.zeros((qd.shape[0], MLA_QK - MLA_KV_RANK - MLA_ROPE), qcat_ref.dtype)
    for h in range(MLA_HEADS):
        q_lat = _dot(qd[:, h * MLA_NOPE:(h + 1) * MLA_NOPE].astype(BF16), wuk_ref[h]) * MLA_SCALE
        qcat_ref[h, :, 0:MLA_KV_RANK] = q_lat.astype(qcat_ref.dtype)
        qcat_ref[h, :, MLA_KV_RANK:MLA_KV_RANK + MLA_ROPE] = (
            q_rope[:, h * MLA_ROPE:(h + 1) * MLA_ROPE].astype(qcat_ref.dtype))
        qcat_ref[h, :, MLA_KV_RANK + MLA_ROPE:MLA_QK] = zpad
    c_new = _rms(mkv, kg_ref[...])
    kr_new = (zr_ref[:, 0:MLA_ROPE] * cos4[:, 0:MLA_ROPE]
              + zr_ref[:, MLA_ROPE:2 * MLA_ROPE] * sin4[:, 0:MLA_ROPE])
    c_ref[...] = c_new
    kr_ref[...] = kr_new
    kcat_ref[:, 0:MLA_KV_RANK] = c_new.astype(kcat_ref.dtype)
    kcat_ref[:, MLA_KV_RANK:MLA_KV_RANK + MLA_ROPE] = kr_new.astype(kcat_ref.dtype)
    kcat_ref[:, MLA_KV_RANK + MLA_ROPE:MLA_QK] = zpad


def mla_prep(z, cos4, sin4, qg, kg, wq, wuk, l, L, cat_dtype):
    n = z.shape[0]
    nt = cos4.shape[0] // L
    lay = lambda shape: pl.BlockSpec((None,) + shape, lambda i: (l,) + (0,) * len(shape))
    tab_spec = pl.BlockSpec((L, MLA_HEADS * MLA_ROPE), lambda i: (i % nt, 0))
    return pl.pallas_call(
        _mla_prep_kernel,
        grid=(n // L,),
        in_specs=[
            pl.BlockSpec((L, MIX_W), lambda i: (i, Z_MQ // MIX_W)),
            pl.BlockSpec((L, 128), lambda i: (i, Z_MKR // 128)),
            tab_spec, tab_spec,
            lay((1, MLA_Q_RANK)), lay((1, MLA_KV_RANK)),
            lay((MLA_Q_RANK, MLA_HEADS * (MLA_NOPE + 2 * MLA_ROPE))),
            lay((MLA_HEADS, MLA_NOPE, MLA_KV_RANK)),
        ],
        out_specs=[
            pl.BlockSpec((MLA_HEADS, L, MLA_QK), lambda i: (0, i, 0)),
            pl.BlockSpec((L, MLA_QK), lambda i: (i, 0)),
            pl.BlockSpec((L, MLA_KV_RANK), lambda i: (i, 0)),
            pl.BlockSpec((L, MLA_ROPE), lambda i: (i, 0)),
        ],
        out_shape=[
            jax.ShapeDtypeStruct((MLA_HEADS, n, MLA_QK), cat_dtype),
            jax.ShapeDtypeStruct((n, MLA_QK), cat_dtype),
            jax.ShapeDtypeStruct((n, MLA_KV_RANK), F32),
            jax.ShapeDtypeStruct((n, MLA_ROPE), F32),
        ],
        compiler_params=_cp(("parallel",)),
        name="mla_prep",
    )(z, z, cos4, sin4, qg, kg, wq, wuk)


NEG_BIG = -1e30


def _causal_mask(tq, tk):
    row = lax.broadcasted_iota(jnp.int32, (MLA_HEADS * tq, tk), 0)
    key = lax.broadcasted_iota(jnp.int32, (MLA_HEADS * tq, tk), 1)
    head = sum((row >= h * tq).astype(jnp.int32) for h in range(1, MLA_HEADS))
    return key <= row - head * tq


def _mla_causal_kernel(q_ref, k_ref, wuv_ref, yd_ref, m_ref, l_ref, acc_ref, *, tb):
    qi = pl.program_id(1)
    rows = MLA_HEADS * tb
    m_ref[...] = jnp.full(m_ref.shape, NEG_BIG, F32)
    l_ref[...] = jnp.zeros(l_ref.shape, F32)
    acc_ref[...] = jnp.zeros(acc_ref.shape, F32)
    q = q_ref[...].reshape(rows, MLA_QK)

    def block(kj, masked):
        k = k_ref[pl.ds(pl.multiple_of(kj * tb, tb), tb), :]
        s = _dot_t(q, k)
        if masked:
            s = jnp.where(_causal_mask(tb, tb), s, NEG_BIG)
        m_old = m_ref[...]
        m_new = jnp.maximum(m_old, jnp.max(s, -1, keepdims=True))
        alpha = jnp.exp(m_old - m_new)
        p = jnp.exp(s - jnp.tile(m_new, (1, tb // 128)))
        l_ref[...] = alpha * l_ref[...] + jnp.sum(p, -1, keepdims=True)
        p16 = p.astype(BF16)
        half = tb // 2
        pv = (_dot(p16[:, :half], k[:half, 0:MLA_KV_RANK]) + _dot(p16[:, half:], k[half:, 0:MLA_KV_RANK]))
        acc_ref[...] = alpha * acc_ref[...] + pv
        m_ref[...] = m_new

    def body(kj, carry):
        block(kj, False)
        return carry

    lax.fori_loop(0, qi, body, 0)
    block(qi, True)
    o = acc_ref[...] / l_ref[...]
    for h in range(MLA_HEADS):
        oh = o[h * tb:(h + 1) * tb, :].astype(BF16)
        yd_ref[:, h * MLA_V:(h + 1) * MLA_V] = _dot(oh, wuv_ref[h]).astype(yd_ref.dtype)


def mla_causal(qcat, kcat, wuv, l, nb, t, tb):
    nq = t // tb
    rows = MLA_HEADS * tb
    return pl.pallas_call(
        functools.partial(_mla_causal_kernel, tb=tb),
        grid=(nb, nq),
        in_specs=[
            pl.BlockSpec((MLA_HEADS, tb, MLA_QK), lambda b, i: (0, b * nq + i, 0)),
            pl.BlockSpec((t, MLA_QK), lambda b, i: (b, 0)),
            pl.BlockSpec((None, MLA_HEADS, MLA_KV_RANK, MLA_V), lambda b, i: (l, 0, 0, 0)),
        ],
        out_specs=pl.BlockSpec((tb, MIX_W), lambda b, i: (b * nq + i, 0)),
        out_shape=jax.ShapeDtypeStruct((nb * t, MIX_W), BF16),
        scratch_shapes=[pltpu.VMEM((rows, 128), F32), pltpu.VMEM((rows, 128), F32),
                        pltpu.VMEM((rows, MLA_KV_RANK), F32)],
        compiler_params=_cp(("parallel", "arbitrary")),
        name="mla_causal",
    )(qcat, kcat, wuv)


PAGED_GROUP = 4


def _mla_paged_kernel(pt_ref, q_ref, kn_ref, cache_c_ref, cache_krt_ref, wuv_ref, yd_ref,
                      cbuf_ref, krbuf_ref, sem_ref, cb16_ref, s_ref, *, layer, n_pages, kchunk):
    g = pl.program_id(0)
    ng = pl.num_programs(0)
    t = SUBLANES
    hr = MLA_HEADS * t
    past = n_pages * PAGE_SIZE
    nchunks = past // kchunk

    def page_copies(base, slot, p, page_of):
        out = []
        for ab in range(2):
            page = page_of(base + ab, p)
            out.append(pltpu.make_async_copy(
                cache_c_ref.at[layer, page],
                cbuf_ref.at[slot, pl.ds(p * PAGE_SIZE, PAGE_SIZE), pl.ds(ab * MLA_KV_RANK, MLA_KV_RANK)],
                sem_ref.at[0, slot]))
            out.append(pltpu.make_async_copy(
                cache_krt_ref.at[layer, page],
                krbuf_ref.at[slot, pl.ds(ab * MLA_ROPE, MLA_ROPE), pl.ds(p * PAGE_SIZE, PAGE_SIZE)],
                sem_ref.at[1, slot]))
        return out

    table_page = lambda seq, p: pt_ref[seq * n_pages + p]

    def issue(base, slot):
        for p in range(n_pages):
            for n, cp in enumerate(page_copies(base, slot, p, table_page)):
                cp.start(priority=n % 2)

    def wait_all(slot):
        pltpu.make_async_copy(cbuf_ref.at[slot], cbuf_ref.at[slot], sem_ref.at[0, slot]).wait()
        pltpu.make_async_copy(krbuf_ref.at[slot], krbuf_ref.at[slot], sem_ref.at[1, slot]).wait()

    @pl.when(g == 0)
    def _():
        issue(0, 0)

    qf = q_ref[...]
    knf = kn_ref[...]
    mask_new = _causal_mask(t, t)
    next_base = jnp.minimum(PAGED_GROUP * (g + 1), PAGED_GROUP * (ng - 1))
    o_all = []

    for half in range(2):
        slot = half
        wait_all(slot)
        fill_base, fill_slot = (PAGED_GROUP * g + 2, 1) if half == 0 else (next_base, 0)
        issue(fill_base, fill_slot)
        qs = [qf[:, (2 * half + ab) * t:(2 * half + ab + 1) * t, :].reshape(hr, MLA_QK) for ab in range(2)]
        kns = [knf[(2 * half + ab) * t:(2 * half + ab + 1) * t, :].astype(BF16) for ab in range(2)]
        z_lat = jnp.zeros((hr, MLA_KV_RANK), F32)
        z_rope = jnp.zeros((hr, MLA_ROPE), F32)
        rope = slice(MLA_KV_RANK, MLA_KV_RANK + MLA_ROPE)
        q_lat = jnp.concatenate([jnp.concatenate([qs[0][:, 0:MLA_KV_RANK], z_lat], 1),
                                 jnp.concatenate([z_lat, qs[1][:, 0:MLA_KV_RANK]], 1)], 0).astype(BF16)
        q_rope = jnp.concatenate([jnp.concatenate([qs[0][:, rope], z_rope], 1),
                                  jnp.concatenate([z_rope, qs[1][:, rope]], 1)], 0).astype(BF16)
        for chunk in range(nchunks):
            ks = slice(chunk * kchunk, (chunk + 1) * kchunk)
            cb = cbuf_ref[slot, ks, :].astype(BF16)
            cb16_ref[ks, :] = cb
            krb = krbuf_ref[slot, :, ks].astype(BF16)
            s_ref[:, ks] = _dot_t(q_lat, cb) + _dot(q_rope, krb)
        s_new = jnp.concatenate(
            [jnp.where(mask_new, _dot_t(qs[ab].astype(BF16), kns[ab]), NEG_BIG) for ab in range(2)], 0)
        s_past = s_ref[...]
        m = jnp.maximum(jnp.max(s_past, -1, keepdims=True), jnp.max(s_new, -1, keepdims=True))
        p_past = jnp.exp(s_past - m)
        p_new = jnp.exp(s_new - m)
        l = jnp.sum(p_past, -1, keepdims=True) + jnp.sum(p_new, -1, keepdims=True)
        p16 = p_past.astype(BF16)
        o_pair = None
        for chunk in range(nchunks):
            ks = slice(chunk * kchunk, (chunk + 1) * kchunk)
            part = _dot(p16[:, ks], cb16_ref[ks, :])
            o_pair = part if o_pair is None else o_pair + part
        for ab in range(2):
            rs = slice(ab * hr, (ab + 1) * hr)
            o_ab = (o_pair[rs, ab * MLA_KV_RANK:(ab + 1) * MLA_KV_RANK]
                    + _dot(p_new[rs].astype(BF16), kns[ab][:, 0:MLA_KV_RANK]))
            o_all.append(o_ab / l[rs])

    for h in range(MLA_HEADS):
        oh = jnp.concatenate([o[h * t:(h + 1) * t] for o in o_all], 0).astype(BF16)
        yd_ref[:, h * MLA_V:(h + 1) * MLA_V] = _dot(oh, wuv_ref[h]).astype(yd_ref.dtype)

    @pl.when(g == ng - 1)
    def _():
        wait_all(0)


def mla_paged(qcat, kcat, cache_c, cache_krt, page_table, wuv, layer):
    nb, n_pages = page_table.shape
    assert nb % PAGED_GROUP == 0
    t = SUBLANES
    past = n_pages * PAGE_SIZE
    kchunk = _tile(past, 2048)
    rows = PAGED_GROUP * t
    grid_spec = pltpu.PrefetchScalarGridSpec(
        num_scalar_prefetch=1,
        grid=(nb // PAGED_GROUP,),
        in_specs=[
            pl.BlockSpec((MLA_HEADS, rows, MLA_QK), lambda g, pt: (0, g, 0)),
            pl.BlockSpec((rows, MLA_QK), lambda g, pt: (g, 0)),
            pl.BlockSpec(memory_space=pl.ANY),
            pl.BlockSpec(memory_space=pl.ANY),
            pl.BlockSpec((None, MLA_HEADS, MLA_KV_RANK, MLA_V), lambda g, pt: (layer, 0, 0, 0)),
        ],
        out_specs=pl.BlockSpec((rows, MIX_W), lambda g, pt: (g, 0)),
        scratch_shapes=[
            pltpu.VMEM((2, past, 2 * MLA_KV_RANK), F32),
            pltpu.VMEM((2, 2 * MLA_ROPE, past), F32),
            pltpu.SemaphoreType.DMA((2, 2)),
            pltpu.VMEM((past, 2 * MLA_KV_RANK), BF16),
            pltpu.VMEM((2 * MLA_HEADS * t, past), F32),
        ],
    )
    return pl.pallas_call(
        functools.partial(_mla_paged_kernel, layer=layer, n_pages=n_pages, kchunk=kchunk),
        grid_spec=grid_spec,
        out_shape=jax.ShapeDtypeStruct((nb * t, MIX_W), F32),
        compiler_params=_cp(("arbitrary",)),
        name="mla_paged",
    )(page_table.reshape(-1), qcat, kcat, cache_c, cache_krt, wuv)


def _merge_kernel(x_ref, *refs):
    brs, wgs, bgs = refs[0:N_BRANCH], refs[N_BRANCH:2 * N_BRANCH], refs[2 * N_BRANCH:3 * N_BRANCH]
    wb_ref, o_ref = refs[3 * N_BRANCH:]
    x = x_ref[...]
    acc = None
    for g in range(N_BRANCH):
        gate = _sigmoid(_dot_t(x, wgs[g][...]) + bgs[g][...])
        term = _dot(brs[g][...].astype(BF16), wb_ref[g]) * gate
        acc = term if acc is None else acc + term
    o_ref[...] = acc.astype(o_ref.dtype)


def merge(xb, ys, wg, bg, wb, l, tm, tn):
    n, d = xb.shape
    nj = d // tn
    row = lambda shape: pl.BlockSpec(shape, lambda j, i: (i, 0))
    gate_w = [pl.BlockSpec((None, tn, d), lambda j, i, g=g: (l, g * nj + j, 0)) for g in range(N_BRANCH)]
    gate_b = [pl.BlockSpec((None, 1, tn), lambda j, i, g=g: (l, 0, g * nj + j)) for g in range(N_BRANCH)]
    return pl.pallas_call(
        _merge_kernel,
        grid=(nj, n // tm),
        in_specs=[row((tm, d))] + [row((tm, MIX_W))] * N_BRANCH + gate_w + gate_b
        + [pl.BlockSpec((None, N_BRANCH, MIX_W, tn), lambda j, i: (l, 0, 0, j))],
        out_specs=pl.BlockSpec((tm, tn), lambda j, i: (i, j)),
        out_shape=jax.ShapeDtypeStruct((n, d), BF16),
        compiler_params=_cp(("parallel", "parallel")),
        name="merge",
    )(xb, *ys, *([wg] * N_BRANCH), *([bg] * N_BRANCH), wb)


def _out_ln_kernel(m_ref, w_ref, x_ref, g_ref, b_ref, o_ref, *, alpha):
    y = alpha * x_ref[...] + _dot(m_ref[...], w_ref[...])
    o_ref[...] = _layer_norm(y, g_ref[...], b_ref[...])


def out_ln(merged, w_out, x, g, b, l, alpha, tm):
    n, d = x.shape
    row = pl.BlockSpec((tm, d), lambda i: (i, 0))
    lay = lambda shape: pl.BlockSpec((None,) + shape, lambda i: (l, 0, 0))
    return pl.pallas_call(
        functools.partial(_out_ln_kernel, alpha=alpha),
        grid=(n // tm,),
        in_specs=[row, lay((d, d)), row, lay((1, d)), lay((1, d))],
        out_specs=row,
        out_shape=jax.ShapeDtypeStruct((n, d), F32),
        compiler_params=_cp(("parallel",)),
        name="out_ln",
    )(merged, w_out, x, g, b)


FFN_HALO = 16


def _ffn_kernel(x_ref, halo_ref, wh_ref, wg_ref, wd_ref, cw_ref, cb_ref, st_ref, g_ref, b_ref,
                o_ref, ob_ref, hl_ref, xcat_ref, hs_ref, acc_ref, *, tm, alpha, chained, tiles_per_seq):
    i = pl.program_id(0)
    f = pl.program_id(1)

    @pl.when(f == 0)
    def _():
        acc_ref[...] = jnp.zeros(acc_ref.shape, F32)
        xcat_ref[FFN_HALO:, :] = x_ref[...].astype(BF16)
        if chained:
            seq_start = (i % tiles_per_seq) == 0
            xcat_ref[0:FFN_HALO, :] = jnp.where(seq_start, 0.0, halo_ref[...]).astype(BF16)

    w0 = cw_ref[0:1, :]
    w1 = cw_ref[1:2, :]
    w2 = cw_ref[2:3, :]
    if chained:
        hs_ref[...] = _dot(xcat_ref[...], wh_ref[...])
        h = hs_ref[FFN_HALO:, :]
        hm1 = hs_ref[pl.ds(FFN_HALO - 1, tm), :]
        hm2 = hs_ref[pl.ds(FFN_HALO - 2, tm), :]
        hl_ref[0] = hs_ref[tm + FFN_HALO - SUBLANES:, :]
        conv = cb_ref[...] + w0 * hm2 + w1 * hm1 + w2 * h
    else:
        t = SUBLANES
        nseq = tm // t
        tf = wh_ref.shape[1]
        h = _dot(xcat_ref[FFN_HALO:, :], wh_ref[...]).reshape(nseq, t, tf)
        r = lax.broadcasted_iota(jnp.int32, (nseq, t, tf), 1)
        b0 = st_ref[:, 0:1, :]
        b1 = st_ref[:, 1:2, :]
        hm1 = jnp.where(r >= 1, pltpu.roll(h, 1, 1), b1)
        hm2 = jnp.where(r >= 2, pltpu.roll(h, 2, 1), jnp.where(r == 1, b1, b0))
        hl_ref[...] = h[:, t - (CONV_W - 1):, :]
        conv = (cb_ref[...] + w0 * hm2 + w1 * hm1 + w2 * h).reshape(tm, tf)
    gate = _dot(xcat_ref[FFN_HALO:, :], wg_ref[...])
    a = (_gelu_tanh(conv) * gate).astype(BF16)
    acc_ref[...] += _dot(a, wd_ref[...])

    @pl.when(f == pl.num_programs(1) - 1)
    def _():
        y = _layer_norm(alpha * x_ref[...] + acc_ref[...], g_ref[...], b_ref[...])
        o_ref[...] = y
        ob_ref[...] = y.astype(ob_ref.dtype)


def conv_ffn(x, w_up, w_down, cw, cb, g, b, l, state, ls, alpha, nb, t, chained, tm, tf):
    n, d = x.shape
    ff = w_down.shape[1]
    nf = ff // tf
    lay = lambda shape: pl.BlockSpec((None,) + shape, lambda i, f: (l, 0, 0))
    hb = tm // FFN_HALO
    if chained:
        tiles_per_seq = t // tm
        st = jnp.zeros((1, 1, CONV_W - 1, tf), F32)
        st_spec = pl.BlockSpec((None, 1, CONV_W - 1, tf), lambda i, f: (0, 0, 0, 0))
        hl_shape = jax.ShapeDtypeStruct((n // tm, SUBLANES, ff), F32)
        hl_spec = pl.BlockSpec((1, SUBLANES, tf), lambda i, f: (i, 0, f))
    else:
        assert t == SUBLANES
        tiles_per_seq = 1
        st = state
        st_spec = pl.BlockSpec((None, tm // t, CONV_W - 1, tf), lambda i, f: (ls, i, 0, f))
        hl_shape = jax.ShapeDtypeStruct((nb, CONV_W - 1, ff), F32)
        hl_spec = pl.BlockSpec((tm // t, CONV_W - 1, tf), lambda i, f: (i, 0, f))
    o, ob, hl = pl.pallas_call(
        functools.partial(_ffn_kernel, tm=tm, alpha=alpha, chained=chained, tiles_per_seq=tiles_per_seq),
        grid=(n // tm, nf),
        in_specs=[
            pl.BlockSpec((tm, d), lambda i, f: (i, 0)),
            pl.BlockSpec((FFN_HALO, d), lambda i, f: (jnp.maximum(i * hb - 1, 0), 0)),
            pl.BlockSpec((None, d, tf), lambda i, f: (l, 0, f)),
            pl.BlockSpec((None, d, tf), lambda i, f: (l, 0, nf + f)),
            pl.BlockSpec((None, tf, d), lambda i, f: (l, f, 0)),
            pl.BlockSpec((None, CONV_W, tf), lambda i, f: (l, 0, f)),
            pl.BlockSpec((None, 1, tf), lambda i, f: (l, 0, f)),
            st_spec, lay((1, d)), lay((1, d)),
        ],
        out_specs=[pl.BlockSpec((tm, d), lambda i, f: (i, 0)), pl.BlockSpec((tm, d), lambda i, f: (i, 0)), hl_spec],
        out_shape=[jax.ShapeDtypeStruct((n, d), F32), jax.ShapeDtypeStruct((n, d), BF16), hl_shape],
        scratch_shapes=[pltpu.VMEM((tm + FFN_HALO, d), BF16), pltpu.VMEM((tm + FFN_HALO, tf), F32),
                        pltpu.VMEM((tm, d), F32)],
        compiler_params=_cp(("parallel", "arbitrary")),
        name="conv_ffn",
    )(x, x, w_up, w_up, w_down, cw, cb, st, g, b)
    if chained:
        hl = hl[tiles_per_seq - 1::tiles_per_seq, SUBLANES - (CONV_W - 1):, :]
    return o, ob, hl


def _rope_tables(pos0, t):
    half = MLA_ROPE // 2
    inv = ROPE_THETA ** (-jnp.arange(half, dtype=F32) / half)
    ang = (pos0 + jnp.arange(t)).astype(F32)[:, None] * inv
    cos, sin = jnp.cos(ang), jnp.sin(ang)
    cosf = jnp.concatenate([cos, cos], -1)
    sinf = jnp.concatenate([-sin, sin], -1)
    return jnp.tile(cosf, (1, MLA_HEADS)), jnp.tile(sinf, (1, MLA_HEADS))


def _rot_half_cols(w):
    half = w.shape[-1] // 2
    return jnp.concatenate([w[..., half:], w[..., :half]], -1)


IN_SIZES = (GLA_HEADS * GLA_DK, GLA_HEADS * GLA_DK, GLA_HEADS * GLA_DV, MIX_W, GLA_GATE_RANK,
            MIX_W, MIX_W, MLA_Q_RANK, MLA_KV_RANK, MLA_ROPE)
IN_GATE_OFF = sum(IN_SIZES)
IN_DEST = (Z_Q, Z_K, Z_V, Z_OG, Z_GA, Z_SU, Z_PU, Z_MQ, Z_MKV, Z_MKR)


def _prep_weights(p):
    depth, d_model, _ = p['w_in'].shape
    hk = GLA_HEADS * GLA_DK
    w_t = jnp.swapaxes(p['w_in'], 1, 2)
    groups, off = {}, 0
    for size, dst in zip(IN_SIZES, IN_DEST):
        groups[dst] = w_t[:, off:off + size, :]
        off += size
    half = MLA_ROPE // 2
    mkr = groups[Z_MKR]
    groups[Z_MKR_ROT] = jnp.concatenate([mkr[:, half:], mkr[:, :half]], 1)
    pad0 = Z_GA + GLA_GATE_RANK
    groups[pad0] = jnp.zeros((depth, Z_COLS - pad0, d_model), F32)
    w_z = jnp.concatenate([groups[k] for k in sorted(groups)], 1).astype(BF16)
    w_gate = w_t[:, IN_GATE_OFF:, :].astype(BF16)
    out = dict(w_z=w_z, w_gate=w_gate, b_gate=p['b_gates'].reshape(depth, 1, N_BRANCH * d_model))
    out['gla_wg'] = jnp.zeros((depth, 128, hk), F32).at[:, :GLA_GATE_RANK].set(p['w_gla_gate'])
    out['gla_bg'] = p['b_gla_gate'].reshape(depth, 1, hk)
    out['gla_ng'] = p['gla_norm_g'].reshape(depth, 1, MIX_W)

    g2 = S5_GROUPS // 2
    eye = jnp.eye(g2, dtype=F32)

    def blockdiag_in(bm):
        bm = bm.reshape(depth, 2, g2, S5_STATE, S5_GROUP)
        w = jnp.einsum('lhgnj,gk->lhgjkn', bm, eye)
        return w.reshape(depth, 2, g2 * S5_GROUP, g2 * S5_STATE).astype(BF16)

    def blockdiag_out(cm):
        cm = cm.reshape(depth, 2, g2, S5_GROUP, S5_STATE)
        w = jnp.einsum('lhgjn,gk->lhgnkj', cm, eye)
        return w.reshape(depth, 2, g2 * S5_STATE, g2 * S5_GROUP).astype(BF16)

    out['s5_wbr'], out['s5_wbi'] = blockdiag_in(p['s5_b_re']), blockdiag_in(p['s5_b_im'])
    out['s5_wcr'], out['s5_wci'] = blockdiag_out(p['s5_c_re']), blockdiag_out(p['s5_c_im'])
    out['s5_d'] = p['s5_d'].reshape(depth, 1, MIX_W)
    out['s5_wglu'] = p['w_s5_glu'].astype(BF16)
    out['s5_bglu'] = p['b_s5_glu'].reshape(depth, 1, MIX_W)
    out['pool_w'] = p['w_pool'].astype(BF16)
    out['pool_sc'] = p['pool_scale'].reshape(depth, 1, MIX_W)
    wq = p['w_q_up'].reshape(depth, MLA_Q_RANK, MLA_HEADS, MLA_NOPE + MLA_ROPE)
    flat = lambda w: w.reshape(depth, MLA_Q_RANK, -1)
    wq_rope = wq[..., MLA_NOPE:]
    out['mla_wq'] = jnp.concatenate(
        [flat(wq[..., :MLA_NOPE]), flat(wq_rope), flat(_rot_half_cols(wq_rope))], 2).astype(BF16)
    wkv = p['w_kv_up'].reshape(depth, MLA_KV_RANK, MLA_HEADS, MLA_NOPE + MLA_V)
    out['mla_wuk'] = wkv[..., :MLA_NOPE].transpose(0, 2, 3, 1).astype(BF16)
    out['mla_wuv'] = wkv[..., MLA_NOPE:].transpose(0, 2, 1, 3).astype(BF16)
    out['mla_qg'] = p['mla_q_norm'].reshape(depth, 1, MLA_Q_RANK)
    out['mla_kg'] = p['mla_kv_norm'].reshape(depth, 1, MLA_KV_RANK)
    out['w_branch'] = p['w_branch'].astype(BF16)
    out['w_out'] = p['w_out'].astype(BF16)
    out['w_up'] = p['w_ffn_up'].astype(BF16)
    out['w_down'] = p['w_ffn_down'].astype(BF16)
    out['conv_w'] = p['ffn_conv_w']
    out['conv_b'] = p['ffn_conv_b'].reshape(depth, 1, -1)
    for nm in ('ln1_g', 'ln1_b', 'ln2_g', 'ln2_b'):
        out[nm] = p[nm].reshape(depth, 1, d_model)
    return out


def _layer(x, xb, w, tabs, st, nb, t, pos0, l, ls, cache, alpha, chained):
    n, d = x.shape
    z = matmul(xb, w['w_z'], l, _tile(n, 1024), Z_COLS // 2)
    gla_w = (w['gla_wg'], w['gla_bg'], w['gla_ng'], l)
    s5_w = (tabs[0], tabs[1], w['s5_wbr'], w['s5_wbi'], w['s5_wcr'], w['s5_wci'], w['s5_d'], w['s5_wglu'],
            w['s5_bglu'], l)
    cos4, sin4 = _rope_tables(pos0, t)
    mla_w = (w['mla_qg'], w['mla_kg'], w['mla_wq'], w['mla_wuk'], l)
    if chained:
        gl = _tile(t, 64)
        ya, s_gla = gla(z, *gla_w, st['gla'], ls, nb, t, gl, _tile(t // gl, 4), True)
        yb, h_re, h_im = s5(z, *s5_w, st['s5_re'], st['s5_im'], ls, nb, t, True, SUBLANES * S5_NJ_CHAIN)
        yc, pbuf = pool(z, w['pool_w'], w['pool_sc'], l, st['pool'], ls, nb, t, pos0, True, _tile(t, 256))
        qcat, kcat, c_new, kr_new = mla_prep(z, cos4, sin4, *mla_w, _tile(t, 256), BF16)
        yd = mla_causal(qcat, kcat, w['mla_wuv'], l, nb, t, _tile(t, 512))
    else:
        ya, s_gla = gla(z, *gla_w, st['gla'], ls, nb, t, t, _tile(nb, 8), False)
        yb, h_re, h_im = s5(z, *s5_w, st['s5_re'], st['s5_im'], ls, nb, t, False, _tile(n, 256))
        yc, pbuf = pool(z, w['pool_w'], w['pool_sc'], l, st['pool'], ls, nb, t, pos0, False, _tile(n, 128))
        prep_rows = _tile(n, 128)
        tile_rows = lambda a: jnp.tile(a, (prep_rows // t, 1))
        qcat, kcat, c_new, kr_new = mla_prep(z, tile_rows(cos4), tile_rows(sin4), *mla_w, prep_rows, F32)
        yd = mla_paged(qcat, kcat, cache[0], cache[1], cache[2], w['mla_wuv'], l)
    merged = merge(xb, (ya, yb, yc, yd), w['w_gate'], w['b_gate'], w['w_branch'], l, _tile(n, 1024), _tile(d, 512))
    x1 = out_ln(merged, w['w_out'], x, w['ln1_g'], w['ln1_b'], l, alpha, _tile(n, 512))
    ff = w['w_down'].shape[1]
    tm = _tile(t, 512) if chained else _tile(n, 512)
    x2, x2b, cbuf = conv_ffn(x1, w['w_up'], w['w_down'], w['conv_w'], w['conv_b'], w['ln2_g'], w['ln2_b'], l,
                             st['conv'], ls, alpha, nb, t, chained, tm, _tile(ff, 512))
    new = (c_new.reshape(nb, t, MLA_KV_RANK), kr_new.reshape(nb, t, MLA_ROPE), s_gla, h_re, h_im, pbuf, cbuf)
    return x2, x2b, new


def _trunk(x3, w, tabs, states, shared_state, pos0, cache, alpha, chained):
    nb, t, d = x3.shape
    depth = w['w_z'].shape[0]
    x = x3.reshape(nb * t, d)
    xb = x.astype(BF16)
    outs = [[] for _ in range(7)]
    for l in range(depth):
        x, xb, new = _layer(x, xb, w, tabs, states, nb, t, pos0, l, 0 if shared_state else l, cache, alpha,
                            chained)
        for lst, val in zip(outs, new):
            lst.append(val)
    return (x.reshape(nb, t, d),) + tuple(jnp.stack(o, 0) for o in outs)


def kernel(x_prompt, x_sample, cache_kv_latent, cache_k_rope, page_table, state_gla, state_s5_re, state_s5_im,
           state_pool, state_ffn_conv, ln1_g, ln1_b, w_in, b_gates, w_gla_gate, b_gla_gate, gla_norm_g,
           s5_a_re, s5_a_im, s5_log_dt, s5_b_re, s5_b_im, s5_c_re, s5_c_im, s5_d, w_s5_glu, b_s5_glu,
           w_pool, pool_scale, mla_q_norm, w_q_up, mla_kv_norm, w_kv_up, w_branch, w_out, ln2_g, ln2_b,
           w_ffn_up, ffn_conv_w, ffn_conv_b, w_ffn_down):
    params = {'ln1_g': ln1_g, 'ln1_b': ln1_b, 'w_in': w_in, 'b_gates': b_gates, 'w_gla_gate': w_gla_gate,
              'b_gla_gate': b_gla_gate, 'gla_norm_g': gla_norm_g, 's5_b_re': s5_b_re, 's5_b_im': s5_b_im,
              's5_c_re': s5_c_re, 's5_c_im': s5_c_im, 's5_d': s5_d, 'w_s5_glu': w_s5_glu, 'b_s5_glu': b_s5_glu,
              'w_pool': w_pool, 'pool_scale': pool_scale, 'mla_q_norm': mla_q_norm, 'w_q_up': w_q_up,
              'mla_kv_norm': mla_kv_norm, 'w_kv_up': w_kv_up, 'w_branch': w_branch, 'w_out': w_out,
              'ln2_g': ln2_g, 'ln2_b': ln2_b, 'w_ffn_up': w_ffn_up, 'ffn_conv_w': ffn_conv_w,
              'ffn_conv_b': ffn_conv_b, 'w_ffn_down': w_ffn_down}
    depth = w_in.shape[0]
    d_ff = ffn_conv_w.shape[-1]
    alpha = (2 * depth) ** 0.25
    w = _prep_weights(params)
    tabs = s5_tables(s5_a_re, s5_a_im, s5_log_dt, S5_NJ_CHAIN)
    pad_pool = lambda a: jnp.pad(a, ((0, 0), (0, 0), (1, 0), (0, 0)))

    nb_p = x_prompt.shape[0]
    zeros = lambda *shape: jnp.zeros((1, nb_p) + shape, F32)
    st_p = dict(gla=zeros(GLA_HEADS, GLA_DK, GLA_DV), s5_re=zeros(S5_W), s5_im=zeros(S5_W),
                pool=zeros(HALO, MIX_W), conv=None)
    res_p = _trunk(x_prompt, w, tabs, st_p, True, 0, None, alpha, True)

    nb_s = x_sample.shape[0]
    past_len = page_table.shape[1] * PAGE_SIZE
    st_s = dict(gla=state_gla, s5_re=state_s5_re.reshape(depth, nb_s, S5_W),
                s5_im=state_s5_im.reshape(depth, nb_s, S5_W), pool=pad_pool(state_pool), conv=state_ffn_conv)
    cache_krt = jnp.swapaxes(cache_k_rope, 2, 3)
    res_s = _trunk(x_sample, w, tabs, st_s, False, past_len, (cache_kv_latent, cache_krt, page_table), alpha, False)
    return (res_p[0], res_s[0]) + res_p[1:] + res_s[1:]
```
